```python
import math
import jax, jax.numpy as jnp
from jax import lax
import numpy as np

D_MODEL = 2048
BATCH = 2
SEQ = 4096
DEPTH = 2
DEC_BATCH = 32
DEC_SEQ = 4
PAST_LEN = 8192
PAGE_SIZE = 128

N_BRANCH = 4
BRANCH_W = D_MODEL // N_BRANCH
CHUNK = 128
A_GROUPS = 4
A_GDIM = BRANCH_W // A_GROUPS
CONV_W = 31
H_C = 4
HD = BRANCH_W // H_C
Q_BLOCK = 128
FORGET_BIAS = 3.0
N_MEM = 256
H_M = 4
HD_M = BRANCH_W // H_M
IN_COLS = 12 * BRANCH_W + H_C + N_BRANCH * D_MODEL
ALPHA = (2 * DEPTH) ** 0.25
BETA = (8 * DEPTH) ** -0.25
LN_EPS = 1e-5

kernel_name = 'hybrid_gated_fox_gmlp_conv_step'


def layer_norm(x, g, b):
    xf = x.astype(jnp.float32)
    mu = jnp.mean(xf, axis=-1, keepdims=True)
    var = jnp.mean(jnp.square(xf - mu), axis=-1, keepdims=True)
    return ((xf - mu) * lax.rsqrt(var + LN_EPS) * g.astype(jnp.float32) + b.astype(jnp.float32)).astype(x.dtype)


def split_columns(z):
    widths = (BRANCH_W,) * 9 + (H_C,) + (BRANCH_W,) * 3
    idx, s = [], 0
    for w in widths:
        s += w
        idx.append(s)
    return jnp.split(z, idx, axis=-1)


def chunk_sgu(u_pre, v_pre, ln_g, ln_b, w_s, b_s):
    bn, t, _ = u_pre.shape
    c = min(t, CHUNK)
    nc = t // c
    u = jax.nn.gelu(u_pre)
    v = layer_norm(jax.nn.gelu(v_pre), ln_g, ln_b)
    mask = jnp.tril(jnp.ones((c, c), dtype=bool))
    ws = jnp.where(mask[None], w_s[:, :c, :c], 0.0)
    vc = v.reshape(bn, nc, c, A_GROUPS, A_GDIM)
    s = jnp.einsum('gts,bnsgd->bntgd', ws, vc) + b_s[:, :c].T[None, None, :, :, None]
    return u * s.reshape(bn, t, BRANCH_W), v


def conformer_conv(a, b, buf, w_dw, b_dw, ln_g, ln_b, w_pw, b_pw):
    h = a * jax.nn.sigmoid(b)
    hp = jnp.concatenate([buf.astype(h.dtype), h], axis=1)
    y = lax.conv_general_dilated(hp, w_dw[:, None, :].astype(h.dtype), window_strides=(1,), padding='VALID',
                                 dimension_numbers=('NWC', 'WIO', 'NWC'), feature_group_count=BRANCH_W) + b_dw
    y = jax.nn.silu(layer_norm(y, ln_g, ln_b))
    return y @ w_pw + b_pw, hp[:, -(CONV_W - 1):]


def fox_attention(q, k_all, v_all, c_all, n_past):
    bn, t = q.shape[:2]
    s_len = k_all.shape[1]
    qb = min(t, Q_BLOCK)
    nb = t // qb
    q_blocks = q.reshape(bn, nb, qb, H_C, HD).transpose(1, 0, 2, 3, 4)
    cq_blocks = c_all[:, n_past:].reshape(bn, nb, qb, H_C).transpose(1, 0, 3, 2)
    q_pos = (n_past + jnp.arange(t)).reshape(nb, qb)
    k_pos = jnp.arange(s_len)
    ck = jnp.transpose(c_all, (0, 2, 1))

    def one_block(args):
        q_blk, cq, pos = args
        logits = jnp.einsum('bqhd,bkhd->bhqk', q_blk, k_all, preferred_element_type=jnp.float32) * (HD ** -0.5)
        logits = logits + (cq[..., None] - ck[:, :, None, :])
        logits = jnp.where(k_pos[None, None, None, :] <= pos[None, None, :, None], logits, -jnp.inf)
        p = jax.nn.softmax(logits, axis=-1).astype(v_all.dtype)
        return jnp.einsum('bhqk,bkhd->bqhd', p, v_all)

    out = lax.map(one_block, (q_blocks, cq_blocks, q_pos))
    return out.transpose(1, 0, 2, 3, 4).reshape(bn, t, H_C * HD)


def memory_attention(q, mem_k, mem_v):
    bn, t = q.shape[:2]
    logits = jnp.einsum('bqhd,bmhd->bhqm', q, mem_k, preferred_element_type=jnp.float32) * (HD_M ** -0.5)
    p = jax.nn.softmax(logits, axis=-1).astype(mem_v.dtype)
    return jnp.einsum('bhqm,bmhd->bqhd', p, mem_v).reshape(bn, t, H_M * HD_M)


def hybrid_layer(x, conv_buf, past, mem_k, mem_v, w_in, ln_v_g, ln_v_b, w_s, b_s, w_dw, b_dw,
                 ln_c_g, ln_c_b, w_pw, b_pw, b_f, w_branch, w_out, ln_g, ln_b):
    bn, t, _ = x.shape
    (u_a, v_a, g_a, a_b, b_b, g_b, q_c, k_c, v_c, f_c, g_c, q_m, g_m, gate_pre) = split_columns(x @ w_in)
    out_a, v_rows = chunk_sgu(u_a, v_a, ln_v_g, ln_v_b, w_s, b_s)
    if conv_buf is None:
        conv_buf = jnp.zeros((bn, CONV_W - 1, BRANCH_W), x.dtype)
    out_b, new_buf = conformer_conv(a_b, b_b, conv_buf, w_dw, b_dw, ln_c_g, ln_c_b, w_pw, b_pw)
    q = q_c.reshape(bn, t, H_C, HD)
    k = k_c.reshape(bn, t, H_C, HD)
    v = v_c.reshape(bn, t, H_C, HD)
    logf = jax.nn.log_sigmoid(f_c.astype(jnp.float32) + b_f.astype(jnp.float32))
    if past is None:
        k_all, v_all, logf_all, n_past = k, v, logf, 0
    else:
        pk, pv, plf = past
        k_all = jnp.concatenate([pk.astype(k.dtype), k], axis=1)
        v_all = jnp.concatenate([pv.astype(v.dtype), v], axis=1)
        logf_all = jnp.concatenate([plf.astype(jnp.float32), logf], axis=1)
        n_past = pk.shape[1]
    c_all = jnp.cumsum(logf_all, axis=1)
    out_c = fox_attention(q, k_all, v_all, c_all, n_past)
    out_m = memory_attention(q_m.reshape(bn, t, H_M, HD_M), mem_k, mem_v)
    branches = ((out_a, g_a), (out_b, g_b), (out_c, g_c), (out_m, g_m))
    h = None
    for i, (o, g) in enumerate(branches):
        gate = jax.nn.sigmoid(gate_pre[..., i * D_MODEL:(i + 1) * D_MODEL])
        term = gate * ((o * jax.nn.silu(g)) @ w_branch[i])
        h = term if h is None else h + term
    y = layer_norm(ALPHA * x + h @ w_out, ln_g, ln_b)
    return y, k, v, logf, new_buf, v_rows


def setup_inputs(seed: int = 0) -> dict:
    key = jax.random.key(seed)
    ks = jax.random.split(key, 32)
    n_pages = PAST_LEN // PAGE_SIZE
    n_used = DEC_BATCH * n_pages
    n_pool = n_used + max(1, n_used // 4)

    def nrm(k, shape, scale=1.0):
        return scale * jax.random.normal(k, shape, jnp.float32)

    return {
        'x_prompt': nrm(ks[0], (BATCH, SEQ, D_MODEL)),
        'x_sample': nrm(ks[1], (DEC_BATCH, DEC_SEQ, D_MODEL)),
        'mem_prompt': nrm(ks[2], (BATCH, N_MEM, D_MODEL)),
        'cache_k': nrm(ks[3], (DEPTH, n_pool, PAGE_SIZE, H_C, HD)),
        'cache_v': nrm(ks[4], (DEPTH, n_pool, PAGE_SIZE, H_C, HD)),
        'cache_logf': jax.nn.log_sigmoid(FORGET_BIAS + nrm(ks[5], (DEPTH, n_pool, PAGE_SIZE, H_C))),
        'cache_mem_k': nrm(ks[6], (DEPTH, DEC_BATCH, N_MEM, H_M, HD_M)),
        'cache_mem_v': nrm(ks[7], (DEPTH, DEC_BATCH, N_MEM, H_M, HD_M)),
        'state_conv': nrm(ks[8], (DEPTH, DEC_BATCH, CONV_W - 1, BRANCH_W), 0.5),
        'page_table': jax.random.permutation(ks[9], n_pool)[:n_used].reshape(DEC_BATCH, n_pages).astype(jnp.int32),
        'w_in': nrm(ks[10], (DEPTH, D_MODEL, IN_COLS), D_MODEL ** -0.5),
        'w_mem_k': nrm(ks[11], (DEPTH, D_MODEL, BRANCH_W), D_MODEL ** -0.5),
        'w_mem_v': nrm(ks[12], (DEPTH, D_MODEL, BRANCH_W), D_MODEL ** -0.5),
        'ln_v_g': 1.0 + nrm(ks[13], (DEPTH, BRANCH_W), 0.02),
        'ln_v_b': nrm(ks[14], (DEPTH, BRANCH_W), 0.02),
        'w_s': nrm(ks[15], (DEPTH, A_GROUPS, CHUNK, CHUNK), CHUNK ** -0.5),
        'b_s': 1.0 + nrm(ks[16], (DEPTH, A_GROUPS, CHUNK), 0.1),
        'w_dw': nrm(ks[17], (DEPTH, CONV_W, BRANCH_W), CONV_W ** -0.5),
        'b_dw': nrm(ks[18], (DEPTH, BRANCH_W), 0.02),
        'ln_c_g': 1.0 + nrm(ks[19], (DEPTH, BRANCH_W), 0.02),
        'ln_c_b': nrm(ks[20], (DEPTH, BRANCH_W), 0.02),
        'w_pw': nrm(ks[21], (DEPTH, BRANCH_W, BRANCH_W), BRANCH_W ** -0.5),
        'b_pw': nrm(ks[22], (DEPTH, BRANCH_W), 0.02),
        'b_f': FORGET_BIAS + nrm(ks[23], (DEPTH, H_C), 0.1),
        'w_branch': nrm(ks[24], (DEPTH, N_BRANCH, BRANCH_W, D_MODEL), BETA * BRANCH_W ** -0.5),
        'w_out': nrm(ks[25], (DEPTH, D_MODEL, D_MODEL), BETA * D_MODEL ** -0.5),
        'ln_g': 1.0 + nrm(ks[26], (DEPTH, D_MODEL), 0.02),
        'ln_b': nrm(ks[27], (DEPTH, D_MODEL), 0.02),
    }


def reference(x_prompt, x_sample, mem_prompt, cache_k, cache_v, cache_logf, cache_mem_k, cache_mem_v,
              state_conv, page_table, w_in, w_mem_k, w_mem_v, ln_v_g, ln_v_b, w_s, b_s, w_dw, b_dw,
              ln_c_g, ln_c_b, w_pw, b_pw, b_f, w_branch, w_out, ln_g, ln_b):
    bp = x_prompt.shape[0]
    db = x_sample.shape[0]
    n_pages = page_table.shape[1]
    xp, xs = x_prompt, x_sample
    pk_l, pv_l, plf_l, pconv_l, pmk_l, pmv_l = [], [], [], [], [], []
    sk_l, sv_l, slf_l, sconv_l, schunk_l = [], [], [], [], []
    for l in range(DEPTH):
        lp = (w_in[l], ln_v_g[l], ln_v_b[l], w_s[l], b_s[l], w_dw[l], b_dw[l], ln_c_g[l], ln_c_b[l],
              w_pw[l], b_pw[l], b_f[l], w_branch[l], w_out[l], ln_g[l], ln_b[l])
        mk_p = (mem_prompt @ w_mem_k[l]).reshape(bp, N_MEM, H_M, HD_M)
        mv_p = (mem_prompt @ w_mem_v[l]).reshape(bp, N_MEM, H_M, HD_M)
        xp, pk, pv, plf, pconv, _ = hybrid_layer(xp, None, None, mk_p, mv_p, *lp)
        pk_l.append(pk); pv_l.append(pv); plf_l.append(plf); pconv_l.append(pconv)
        pmk_l.append(mk_p); pmv_l.append(mv_p)
        past = (cache_k[l][page_table].reshape(db, n_pages * PAGE_SIZE, H_C, HD),
                cache_v[l][page_table].reshape(db, n_pages * PAGE_SIZE, H_C, HD),
                cache_logf[l][page_table].reshape(db, n_pages * PAGE_SIZE, H_C))
        xs, sk, sv, slf, sconv, schunk = hybrid_layer(xs, state_conv[l], past, cache_mem_k[l], cache_mem_v[l], *lp)
        sk_l.append(sk); sv_l.append(sv); slf_l.append(slf); sconv_l.append(sconv); schunk_l.append(schunk)
    return (xp, xs,
            jnp.stack(pk_l), jnp.stack(pv_l), jnp.stack(plf_l), jnp.stack(pconv_l),
            jnp.stack(pmk_l), jnp.stack(pmv_l),
            jnp.stack(sk_l), jnp.stack(sv_l), jnp.stack(slf_l), jnp.stack(sconv_l), jnp.stack(schunk_l))
```

```python
import functools
import math

import jax
import jax.numpy as jnp
from jax import lax
from jax.experimental import pallas as pl
from jax.experimental.pallas import tpu as pltpu

F32 = jnp.float32
BF16 = jnp.bfloat16

D_MODEL = 2048
DEPTH = 2
BRANCH_W = 512
N_BRANCH = 4
CHUNK = 128
A_GROUPS = 4
CONV_W = 31
H_C = 4
HD = 128
N_MEM = 256
PAGE_SIZE = 128
LN_EPS = 1e-5
ALPHA = (2 * DEPTH) ** 0.25
ATT_SCALE = HD ** -0.5
NEG = -1e30

LANE = 128
PAGE_ROWS = PAGE_SIZE * H_C
Z_COLS = 28 * BRANCH_W
QC, KC, VC, GC, QM, GM = 24, 28, 32, 36, 40, 44
GATE_BLK = 12
PAGES_PER_STEP = 8
VMEM_LIMIT = 48 * 1024 * 1024


def _cparams(n_axes, vmem=VMEM_LIMIT):
    return pltpu.CompilerParams(dimension_semantics=("arbitrary",) * n_axes, vmem_limit_bytes=vmem)


def _ln(x, g, b):
    mu = jnp.mean(x, axis=-1, keepdims=True)
    xc = x - mu
    var = jnp.mean(xc * xc, axis=-1, keepdims=True)
    return xc * lax.rsqrt(var + LN_EPS) * g + b


def _silu(x):
    return x * jax.nn.sigmoid(x)


def _log_sigmoid(x):
    return jnp.minimum(x, 0.0) - jnp.log1p(jnp.exp(-jnp.abs(x)))


def _dot(a, b):
    return jnp.dot(a, b, preferred_element_type=F32)


def _dot_nt(a, b):
    return lax.dot_general(a, b, (((1,), (1,)), ((), ())), preferred_element_type=F32)


def _dot_exact01(x, m01):
    hi = x.astype(BF16)
    r = x - hi.astype(F32)
    mid = r.astype(BF16)
    lo = (r - mid.astype(F32)).astype(BF16)
    return _dot(hi, m01) + _dot(mid, m01) + _dot(lo, m01)


def _in_proj_kernel(x_ref, w_ref, wft_ref, bf_ref, z_ref, lf_ref, c_ref, xb_ref, carry_ref, *,
                    tm, tiles_per_seq):
    i = pl.program_id(0)
    j = pl.program_id(1)

    @pl.when(j == 0)
    def _():
        xb = x_ref[...].astype(BF16)
        xb_ref[...] = xb
        lf = _log_sigmoid(_dot_nt(wft_ref[...], xb) + bf_ref[:, 0:1])
        lf_ref[...] = lf

        @pl.when(lax.rem(i, tiles_per_seq) == 0)
        def _():
            carry_ref[...] = jnp.zeros_like(carry_ref)

        row = lax.broadcasted_iota(jnp.int32, (LANE, LANE), 0)
        col = lax.broadcasted_iota(jnp.int32, (LANE, LANE), 1)
        upper = jnp.where(row <= col, 1.0, 0.0).astype(BF16)
        carry = carry_ref[...]
        for k in range(tm // LANE):
            cblk = _dot_exact01(lf[:, k * LANE:(k + 1) * LANE], upper) + carry
            c_ref[:, k * LANE:(k + 1) * LANE] = cblk
            carry = jnp.broadcast_to(cblk[:, LANE - 1:LANE], (8, LANE))
        carry_ref[...] = carry

    z_ref[...] = _dot(xb_ref[...], w_ref[...])


def _in_proj(x, w_main, wft, bfb, *, tm, tiles_per_seq):
    m = x.shape[0]
    tn = BRANCH_W
    kern = functools.partial(_in_proj_kernel, tm=tm, tiles_per_seq=tiles_per_seq)
    return pl.pallas_call(
        kern,
        grid=(m // tm, Z_COLS // tn),
        in_specs=[
            pl.BlockSpec((tm, D_MODEL), lambda i, j: (i, 0)),
            pl.BlockSpec((D_MODEL, tn), lambda i, j: (0, j)),
            pl.BlockSpec((8, D_MODEL), lambda i, j: (0, 0)),
            pl.BlockSpec((8, LANE), lambda i, j: (0, 0)),
        ],
        out_specs=[
            pl.BlockSpec((tm, tn), lambda i, j: (i, j)),
            pl.BlockSpec((8, tm), lambda i, j: (0, i)),
            pl.BlockSpec((8, tm), lambda i, j: (0, i)),
        ],
        out_shape=[
            jax.ShapeDtypeStruct((m, Z_COLS), F32),
            jax.ShapeDtypeStruct((8, m), F32),
            jax.ShapeDtypeStruct((8, m), F32),
        ],
        scratch_shapes=[pltpu.VMEM((tm, D_MODEL), BF16), pltpu.VMEM((8, LANE), F32)],
        compiler_params=_cparams(2),
        name="in_proj",
    )(x, w_main, wft, bfb)


def _mm_kernel(x_ref, w_ref, o_ref):
    o_ref[...] = _dot(x_ref[...].astype(BF16), w_ref[...])


def _mm(x, w, *, tm, tn):
    m, k = x.shape
    n = w.shape[1]
    return pl.pallas_call(
        _mm_kernel,
        grid=(m // tm, n // tn),
        in_specs=[pl.BlockSpec((tm, k), lambda i, j: (i, 0)), pl.BlockSpec((k, tn), lambda i, j: (0, j))],
        out_specs=pl.BlockSpec((tm, tn), lambda i, j: (i, j)),
        out_shape=jax.ShapeDtypeStruct((m, n), F32),
        compiler_params=_cparams(2),
        name="mem_kv_proj",
    )(x, w)


def _branch_a_kernel(u_ref, v_ref, g_ref, lng_ref, lnb_ref, ws_ref, bsb_ref, mask_ref, o_ref, *rest,
                     tm, emit_v):
    u = jax.nn.gelu(u_ref[...])
    v = _ln(jax.nn.gelu(v_ref[...]), lng_ref[...], lnb_ref[...])
    if emit_v:
        rest[0][...] = v
    gate = _silu(g_ref[...])
    keep = mask_ref[...] > 0.0
    for g in range(A_GROUPS):
        wg = jnp.where(keep, ws_ref[g], 0.0).astype(BF16)
        cs = slice(g * LANE, (g + 1) * LANE)
        for c in range(tm // CHUNK):
            rs = slice(c * CHUNK, (c + 1) * CHUNK)
            s = _dot(wg, v[rs, cs].astype(BF16)) + bsb_ref[g]
            o_ref[rs, cs] = (u[rs, cs] * s * gate[rs, cs]).astype(o_ref.dtype)


def _branch_a(z, lng, lnb, ws, bsb, mask, *, tm, emit_v):
    m = z.shape[0]
    blk = lambda c: pl.BlockSpec((tm, BRANCH_W), lambda i, c=c: (i, c))
    vec = pl.BlockSpec((1, BRANCH_W), lambda i: (0, 0))
    cube = pl.BlockSpec((A_GROUPS, CHUNK, CHUNK), lambda i: (0, 0, 0))
    out_specs = [pl.BlockSpec((tm, BRANCH_W), lambda i: (i, 0))]
    out_shape = [jax.ShapeDtypeStruct((m, BRANCH_W), BF16)]
    if emit_v:
        out_specs.append(pl.BlockSpec((tm, BRANCH_W), lambda i: (i, 0)))
        out_shape.append(jax.ShapeDtypeStruct((m, BRANCH_W), F32))
    return pl.pallas_call(
        functools.partial(_branch_a_kernel, tm=tm, emit_v=emit_v),
        grid=(m // tm,),
        in_specs=[blk(0), blk(1), blk(2), vec, vec, cube, cube,
                  pl.BlockSpec((CHUNK, CHUNK), lambda i: (0, 0))],
        out_specs=out_specs,
        out_shape=out_shape,
        compiler_params=_cparams(1),
        name="branch_a",
    )(z, z, z, lng, lnb, ws, bsb, mask)


CONV_PAD = 32


def _conv_tail(y, bdw_ref, lng_ref, lnb_ref, wpw_ref, bpw_ref, gate):
    y = _silu(_ln(y + bdw_ref[...], lng_ref[...], lnb_ref[...]))
    return (_dot(y.astype(BF16), wpw_ref[...]) + bpw_ref[...]) * _silu(gate)


def _branch_b_kernel(a_ref, b_ref, g_ref, wdw_ref, bdw_ref, lng_ref, lnb_ref, wpw_ref, bpw_ref,
                     o_ref, nc_ref, hp_ref, *, tm):
    t = pl.program_id(1)

    @pl.when(t == 0)
    def _():
        hp_ref[0:CONV_PAD, :] = jnp.zeros((CONV_PAD, BRANCH_W), F32)

    hp_ref[CONV_PAD:CONV_PAD + tm, :] = a_ref[...] * jax.nn.sigmoid(b_ref[...])
    off = CONV_PAD - (CONV_W - 1)
    acc = hp_ref[pl.ds(off, tm), :] * wdw_ref[0:1, :]
    for j in range(1, CONV_W):
        acc = acc + hp_ref[pl.ds(off + j, tm), :] * wdw_ref[j:j + 1, :]
    o_ref[...] = _conv_tail(acc, bdw_ref, lng_ref, lnb_ref, wpw_ref, bpw_ref, g_ref[...]).astype(o_ref.dtype)
    nc_ref[0] = hp_ref[pl.ds(CONV_PAD + tm - (CONV_W - 1), CONV_W - 1), :]
    hp_ref[0:CONV_PAD, :] = hp_ref[tm:tm + CONV_PAD, :]


def _branch_b(z, wdw, bdw, lng, lnb, wpw, bpw, *, batch, tm):
    m = z.shape[0]
    nt = m // batch // tm
    blk = lambda c: pl.BlockSpec((tm, BRANCH_W), lambda b, t, c=c: (b * nt + t, c))
    vec = pl.BlockSpec((1, BRANCH_W), lambda b, t: (0, 0))
    return pl.pallas_call(
        functools.partial(_branch_b_kernel, tm=tm),
        grid=(batch, nt),
        in_specs=[blk(3), blk(4), blk(5),
                  pl.BlockSpec((CONV_PAD, BRANCH_W), lambda b, t: (0, 0)), vec, vec, vec,
                  pl.BlockSpec((BRANCH_W, BRANCH_W), lambda b, t: (0, 0)), vec],
        out_specs=[pl.BlockSpec((tm, BRANCH_W), lambda b, t: (b * nt + t, 0)),
                   pl.BlockSpec((1, CONV_W - 1, BRANCH_W), lambda b, t: (b, 0, 0))],
        out_shape=[jax.ShapeDtypeStruct((m, BRANCH_W), BF16),
                   jax.ShapeDtypeStruct((batch, CONV_W - 1, BRANCH_W), F32)],
        scratch_shapes=[pltpu.VMEM((tm + CONV_PAD, BRANCH_W), F32)],
        compiler_params=_cparams(2),
        name="branch_b",
    )(z, z, z, wdw, bdw, lng, lnb, wpw, bpw)


def _branch_b_sample_kernel(a_ref, b_ref, g_ref, st_ref, wdw_ref, bdw_ref, lng_ref, lnb_ref, wpw_ref,
                            bpw_ref, o_ref, h_ref, hp_ref, *, t_new):
    n_hist = CONV_W - 1
    h = a_ref[0] * jax.nn.sigmoid(b_ref[0])
    h_ref[0] = h
    hp_ref[CONV_PAD:CONV_PAD + 8, :] = jnp.zeros((8, BRANCH_W), F32)
    hp_ref[0:n_hist, :] = st_ref[0]
    hp_ref[n_hist:n_hist + t_new, :] = h
    acc = hp_ref[pl.ds(0, 8), :] * wdw_ref[0:1, :]
    for j in range(1, CONV_W):
        acc = acc + hp_ref[pl.ds(j, 8), :] * wdw_ref[j:j + 1, :]
    y = _silu(_ln(acc + bdw_ref[...], lng_ref[...], lnb_ref[...]))
    out = _dot(y.astype(BF16), wpw_ref[...]) + bpw_ref[...]
    o_ref[0] = out[0:t_new, :] * _silu(g_ref[0])


def _branch_b_sample(z3, state, wdw, bdw, lng, lnb, wpw, bpw):
    nb, t_new, _ = z3.shape
    blk = lambda c: pl.BlockSpec((1, t_new, BRANCH_W), lambda b, c=c: (b, 0, c))
    vec = pl.BlockSpec((1, BRANCH_W), lambda b: (0, 0))
    row = pl.BlockSpec((1, t_new, BRANCH_W), lambda b: (b, 0, 0))
    return pl.pallas_call(
        functools.partial(_branch_b_sample_kernel, t_new=t_new),
        grid=(nb,),
        in_specs=[blk(3), blk(4), blk(5),
                  pl.BlockSpec((1, CONV_W - 1, BRANCH_W), lambda b: (b, 0, 0)),
                  pl.BlockSpec((CONV_PAD, BRANCH_W), lambda b: (0, 0)), vec, vec, vec,
                  pl.BlockSpec((BRANCH_W, BRANCH_W), lambda b: (0, 0)), vec],
        out_specs=[row, row],
        out_shape=[jax.ShapeDtypeStruct((nb, t_new, BRANCH_W), F32)] * 2,
        scratch_shapes=[pltpu.VMEM((CONV_PAD + 8, BRANCH_W), F32)],
        compiler_params=_cparams(1),
        name="branch_b_sample",
    )(z3, z3, z3, state, wdw, bdw, lng, lnb, wpw, bpw)


def _fox_kernel(q_ref, k_ref, v_ref, c_ref, g_ref, o_ref, m_ref, l_ref, acc_ref, *, tq):
    h = pl.program_id(1)
    qi = pl.program_id(2)
    ki = pl.program_id(3)

    @pl.when(ki == 0)
    def _():
        m_ref[...] = jnp.full_like(m_ref, NEG)
        l_ref[...] = jnp.zeros_like(l_ref)
        acc_ref[...] = jnp.zeros_like(acc_ref)

    def step(diagonal):
        s = _dot_nt(q_ref[...].astype(BF16), k_ref[...].astype(BF16)) * ATT_SCALE - c_ref[pl.ds(h, 1), :]
        if diagonal:
            row = lax.broadcasted_iota(jnp.int32, (tq, tq), 0)
            col = lax.broadcasted_iota(jnp.int32, (tq, tq), 1)
            s = jnp.where(col <= row, s, NEG)
        m_prev = m_ref[...]
        m_new = jnp.maximum(m_prev, jnp.max(s, axis=1, keepdims=True))
        alpha = jnp.exp(m_prev - m_new)
        p = jnp.exp(s - m_new)
        l_ref[...] = alpha * l_ref[...] + jnp.sum(p, axis=1, keepdims=True)
        acc_ref[...] = alpha * acc_ref[...] + _dot(p.astype(BF16), v_ref[...].astype(BF16))
        m_ref[...] = m_new

    @pl.when(ki < qi)
    def _():
        step(False)

    @pl.when(ki == qi)
    def _():
        step(True)
        o_ref[...] = (acc_ref[...] / l_ref[...] * _silu(g_ref[...])).astype(o_ref.dtype)


def _fox_prompt(z, c_t, *, batch, tq):
    m = z.shape[0]
    nq = m // batch // tq
    qmap = lambda c: (lambda b, h, qi, ki: (b * nq + qi, c + h))
    kmap = lambda c: (lambda b, h, qi, ki: (b * nq + jnp.minimum(ki, qi), c + h))
    return pl.pallas_call(
        functools.partial(_fox_kernel, tq=tq),
        grid=(batch, H_C, nq, nq),
        in_specs=[pl.BlockSpec((tq, HD), qmap(QC)),
                  pl.BlockSpec((tq, HD), kmap(KC)),
                  pl.BlockSpec((tq, HD), kmap(VC)),
                  pl.BlockSpec((8, tq), lambda b, h, qi, ki: (0, b * nq + jnp.minimum(ki, qi))),
                  pl.BlockSpec((tq, HD), qmap(GC))],
        out_specs=pl.BlockSpec((tq, HD), lambda b, h, qi, ki: (b * nq + qi, h)),
        out_shape=jax.ShapeDtypeStruct((m, BRANCH_W), BF16),
        scratch_shapes=[pltpu.VMEM((tq, 1), F32), pltpu.VMEM((tq, 1), F32), pltpu.VMEM((tq, HD), F32)],
        compiler_params=_cparams(4),
        name="fox_prompt",
    )(z, z, z, c_t, z)


def _mem_attn_kernel(q_ref, k_ref, v_ref, g_ref, o_ref):
    s = _dot_nt(q_ref[...].astype(BF16), k_ref[...].astype(BF16)) * ATT_SCALE
    p = jnp.exp(s - jnp.max(s, axis=1, keepdims=True))
    o = _dot(p.astype(BF16), v_ref[...].astype(BF16)) / jnp.sum(p, axis=1, keepdims=True)
    o_ref[...] = (o * _silu(g_ref[...])).astype(o_ref.dtype)


def _mem_attn_prompt(z, mkv, *, batch, tq):
    m = z.shape[0]
    nq = m // batch // tq
    return pl.pallas_call(
        _mem_attn_kernel,
        grid=(batch, H_C, nq),
        in_specs=[pl.BlockSpec((tq, HD), lambda b, h, qi: (b * nq + qi, QM + h)),
                  pl.BlockSpec((N_MEM, HD), lambda b, h, qi: (b, h)),
                  pl.BlockSpec((N_MEM, HD), lambda b, h, qi: (b, H_C + h)),
                  pl.BlockSpec((tq, HD), lambda b, h, qi: (b * nq + qi, GM + h))],
        out_specs=pl.BlockSpec((tq, HD), lambda b, h, qi: (b * nq + qi, h)),
        out_shape=jax.ShapeDtypeStruct((m, BRANCH_W), BF16),
        compiler_params=_cparams(3),
        name="mem_attn_prompt",
    )(z, mkv, mkv, z)


def _head_match(rows, cols):
    r = lax.broadcasted_iota(jnp.int32, (rows, cols), 0)
    c = lax.broadcasted_iota(jnp.int32, (rows, cols), 1)
    return r, c, (r & (H_C - 1)) == (c & (H_C - 1))


def _mem_attn_sample_kernel(q_ref, g_ref, k_ref, v_ref, o_ref):
    s = _dot_nt(q_ref[0].astype(BF16), k_ref[0].astype(BF16)) * ATT_SCALE
    _, _, same = _head_match(s.shape[0], s.shape[1])
    s = jnp.where(same, s, NEG)
    p = jnp.exp(s - jnp.max(s, axis=1, keepdims=True))
    o = _dot(p.astype(BF16), v_ref[0].astype(BF16)) / jnp.sum(p, axis=1, keepdims=True)
    o_ref[0] = o * _silu(g_ref[0])


def _mem_attn_sample(q16, g16, mk, mv):
    nb, nr, _ = q16.shape
    nm = mk.shape[1]
    row = pl.BlockSpec((1, nr, HD), lambda b: (b, 0, 0))
    mem = pl.BlockSpec((1, nm, HD), lambda b: (b, 0, 0))
    return pl.pallas_call(
        _mem_attn_sample_kernel,
        grid=(nb,),
        in_specs=[row, row, mem, mem],
        out_specs=row,
        out_shape=jax.ShapeDtypeStruct((nb, nr, HD), F32),
        compiler_params=_cparams(1),
        name="mem_attn_sample",
    )(q16, g16, mk, mv)


def _logf_pages_kernel(x_ref, mc_ref, mt_ref, o_ref):
    x = x_ref[...]
    o_ref[:, 0:PAGE_ROWS] = _dot_exact01(x, mc_ref[...])
    o_ref[:, PAGE_ROWS:2 * PAGE_ROWS] = _dot_exact01(x, mt_ref[...])


def _logf_pages(lf_flat, mc, mt, *, tm):
    n = lf_flat.shape[0]
    mat = pl.BlockSpec((PAGE_ROWS, PAGE_ROWS), lambda i: (0, 0))
    return pl.pallas_call(
        _logf_pages_kernel,
        grid=(n // tm,),
        in_specs=[pl.BlockSpec((tm, PAGE_ROWS), lambda i: (i, 0)), mat, mat],
        out_specs=pl.BlockSpec((tm, 2 * PAGE_ROWS), lambda i: (i, 0)),
        out_shape=jax.ShapeDtypeStruct((n, 2 * PAGE_ROWS), F32),
        compiler_params=_cparams(1),
        name="logf_pages",
    )(lf_flat, mc, mt)


def _fox_sample_kernel(pt_ref, q_ref, g_ref, kn_ref, vn_ref, lfn_ref, mn_ref, *rest, n_pages, n_new):
    del pt_ref
    np_ = PAGES_PER_STEP
    k_refs, v_refs, wt_refs = rest[:np_], rest[np_:2 * np_], rest[2 * np_:3 * np_]
    o_ref, m_ref, l_ref, acc_ref, carry_ref = rest[3 * np_:]
    step = pl.program_id(1)
    nr = q_ref.shape[1]

    @pl.when(step == 0)
    def _():
        m_ref[...] = jnp.full_like(m_ref, NEG)
        l_ref[...] = jnp.zeros_like(l_ref)
        acc_ref[...] = jnp.zeros_like(acc_ref)
        carry_ref[...] = jnp.zeros_like(carry_ref)

    q = q_ref[0].astype(BF16)

    def update(s_list, v_list):
        m_prev = m_ref[...]
        m_new = m_prev
        for s in s_list:
            m_new = jnp.maximum(m_new, jnp.max(s, axis=1, keepdims=True))
        alpha = jnp.exp(m_prev - m_new)
        l_new = alpha * l_ref[...]
        acc = alpha * acc_ref[...]
        for s, v in zip(s_list, v_list):
            p = jnp.exp(s - m_new)
            l_new = l_new + jnp.sum(p, axis=1, keepdims=True)
            acc = acc + _dot(p.astype(BF16), v)
        m_ref[...] = m_new
        l_ref[...] = l_new
        acc_ref[...] = acc

    _, _, same = _head_match(nr, PAGE_ROWS)
    carry = carry_ref[...]
    s_list, v_list = [], []
    for i in range(np_):
        wt = wt_refs[i][0]
        ck = carry + wt[:, 0:PAGE_ROWS]
        carry = carry + wt[:, PAGE_ROWS:2 * PAGE_ROWS]
        s = _dot_nt(q, k_refs[i][0].astype(BF16)) * ATT_SCALE - ck
        s_list.append(jnp.where(same, s, NEG))
        v_list.append(v_refs[i][0].astype(BF16))
    carry_ref[...] = carry
    update(s_list, v_list)

    @pl.when(step == n_pages // np_ - 1)
    def _():
        r, c, same_n = _head_match(nr, LANE)
        cn = carry[:, 0:LANE] + _dot_exact01(lfn_ref[0], mn_ref[...])[0:1, :]
        s = _dot_nt(q, kn_ref[0].astype(BF16)) * ATT_SCALE - cn
        ok = same_n & (c < n_new * H_C) & ((c >> 2) <= (r >> 2))
        update([jnp.where(ok, s, NEG)], [vn_ref[0].astype(BF16)])
        o_ref[0] = acc_ref[...] / l_ref[...] * _silu(g_ref[0])


def _fox_sample(pt, q16, g16, kn, vn, lfn, mn, kflat, vflat, wt3, *, n_new):
    nb, nr, _ = q16.shape
    n_pages = pt.shape[1]
    np_ = PAGES_PER_STEP
    row = pl.BlockSpec((1, nr, HD), lambda b, s, pt: (b, 0, 0))
    new = pl.BlockSpec((1, LANE, HD), lambda b, s, pt: (b, 0, 0))
    page = lambda i: (lambda b, s, pt: (pt[b, s * np_ + i], 0, 0))
    in_specs = [row, row, new, new,
                pl.BlockSpec((1, 8, LANE), lambda b, s, pt: (b, 0, 0)),
                pl.BlockSpec((LANE, LANE), lambda b, s, pt: (0, 0))]
    in_specs += [pl.BlockSpec((1, PAGE_ROWS, HD), page(i)) for i in range(np_)]
    in_specs += [pl.BlockSpec((1, PAGE_ROWS, HD), page(i)) for i in range(np_)]
    in_specs += [pl.BlockSpec((1, 1, 2 * PAGE_ROWS), page(i)) for i in range(np_)]
    grid_spec = pltpu.PrefetchScalarGridSpec(
        num_scalar_prefetch=1,
        grid=(nb, n_pages // np_),
        in_specs=in_specs,
        out_specs=pl.BlockSpec((1, nr, HD), lambda b, s, pt: (b, 0, 0)),
        scratch_shapes=[pltpu.VMEM((nr, 1), F32), pltpu.VMEM((nr, 1), F32), pltpu.VMEM((nr, HD), F32),
                        pltpu.VMEM((1, PAGE_ROWS), F32)],
    )
    return pl.pallas_call(
        functools.partial(_fox_sample_kernel, n_pages=n_pages, n_new=n_new),
        grid_spec=grid_spec,
        out_shape=jax.ShapeDtypeStruct((nb, nr, HD), F32),
        compiler_params=_cparams(2),
        name="fox_sample",
    )(pt, q16, g16, kn, vn, lfn, mn, *([kflat] * np_), *([vflat] * np_), *([wt3] * np_))


def _merge_kernel(*refs):
    o_refs, wb_ref, gate_refs, h_ref = refs[0:4], refs[4], refs[5:9], refs[9]
    acc = None
    for br in range(N_BRANCH):
        term = jax.nn.sigmoid(gate_refs[br][...]) * _dot(o_refs[br][...].astype(BF16), wb_ref[br])
        acc = term if acc is None else acc + term
    h_ref[...] = acc.astype(h_ref.dtype)


def _merge(outs, z, wb, *, tm):
    m = z.shape[0]
    tn = BRANCH_W
    nj = D_MODEL // tn
    o_spec = pl.BlockSpec((tm, BRANCH_W), lambda i, j: (i, 0))
    gate = lambda br: pl.BlockSpec((tm, tn), lambda i, j, br=br: (i, GATE_BLK + br * nj + j))
    return pl.pallas_call(
        _merge_kernel,
        grid=(m // tm, nj),
        in_specs=[o_spec] * 4 + [pl.BlockSpec((N_BRANCH, BRANCH_W, tn), lambda i, j: (0, 0, j))]
                 + [gate(br) for br in range(N_BRANCH)],
        out_specs=pl.BlockSpec((tm, tn), lambda i, j: (i, j)),
        out_shape=jax.ShapeDtypeStruct((m, D_MODEL), BF16),
        compiler_params=_cparams(2),
        name="merge",
    )(*outs, wb, z, z, z, z)


def _out_ln_kernel(h_ref, w_ref, x_ref, g_ref, b_ref, y_ref):
    y = ALPHA * x_ref[...] + _dot(h_ref[...], w_ref[...])
    y_ref[...] = _ln(y, g_ref[...], b_ref[...])


def _out_ln(hm, wo, x, lng, lnb, *, tm):
    m = x.shape[0]
    rows = pl.BlockSpec((tm, D_MODEL), lambda i: (i, 0))
    vec = pl.BlockSpec((1, D_MODEL), lambda i: (0, 0))
    return pl.pallas_call(
        _out_ln_kernel,
        grid=(m // tm,),
        in_specs=[rows, pl.BlockSpec((D_MODEL, D_MODEL), lambda i: (0, 0)), rows, vec, vec],
        out_specs=rows,
        out_shape=jax.ShapeDtypeStruct((m, D_MODEL), F32),
        compiler_params=_cparams(1),
        name="out_ln",
    )(hm, wo, x, lng, lnb)


def _tok_head_matrices():
    i = jnp.arange(PAGE_ROWS)
    same = (i[:, None] % H_C) == (i[None, :] % H_C)
    mc = (same & (i[:, None] // H_C <= i[None, :] // H_C)).astype(BF16)
    mt = same.astype(BF16)
    return mc, mt


def kernel(x_prompt, x_sample, mem_prompt, cache_k, cache_v, cache_logf, cache_mem_k, cache_mem_v, state_conv,
           page_table, w_in, w_mem_k, w_mem_v, ln_v_g, ln_v_b, w_s, b_s, w_dw, b_dw, ln_c_g, ln_c_b, w_pw, b_pw,
           b_f, w_branch, w_out, ln_g, ln_b):
    bp, seq, _ = x_prompt.shape
    db, t_new, _ = x_sample.shape
    n_pool = cache_k.shape[1]
    n_pages = page_table.shape[1]
    f0 = 9 * BRANCH_W

    w_main = jnp.concatenate([w_in[:, :, :f0], w_in[:, :, f0 + H_C:]], axis=2).astype(BF16)
    wft = jnp.pad(jnp.swapaxes(w_in[:, :, f0:f0 + H_C], 1, 2), ((0, 0), (0, 8 - H_C), (0, 0))).astype(BF16)
    bfb = jnp.broadcast_to(jnp.pad(b_f, ((0, 0), (0, 8 - H_C)))[:, :, None], (DEPTH, 8, LANE))
    wb = w_branch.astype(BF16)
    wo = w_out.astype(BF16)
    wpw = w_pw.astype(BF16)
    wmkv = jnp.concatenate([w_mem_k, w_mem_v], axis=2).astype(BF16)
    wdw = jnp.pad(w_dw, ((0, 0), (0, CONV_PAD - CONV_W), (0, 0)))
    vec = lambda a: a[:, None, :]
    ln_v_g, ln_v_b, b_dw, ln_c_g, ln_c_b, b_pw, ln_g, ln_b = map(
        vec, (ln_v_g, ln_v_b, b_dw, ln_c_g, ln_c_b, b_pw, ln_g, ln_b))

    idx = jnp.arange(CHUNK)
    mask_p = (idx[None, :] <= idx[:, None]).astype(F32)
    bsb_p = jnp.broadcast_to(b_s[:, :, :, None], (DEPTH, A_GROUPS, CHUNK, CHUNK))
    reps = CHUNK // t_new
    mask_s = ((idx[:, None] // t_new == idx[None, :] // t_new) & (idx[None, :] <= idx[:, None])).astype(F32)
    ws_s = jnp.tile(w_s[:, :, :t_new, :t_new], (1, 1, reps, reps))
    bsb_s = jnp.broadcast_to(jnp.tile(b_s[:, :, :t_new], (1, 1, reps))[:, :, :, None],
                             (DEPTH, A_GROUPS, CHUNK, CHUNK))

    kflat = cache_k.reshape(DEPTH * n_pool, PAGE_ROWS, HD)
    vflat = cache_v.reshape(DEPTH * n_pool, PAGE_ROWS, HD)
    mc, mt = _tok_head_matrices()
    wt3 = _logf_pages(cache_logf.reshape(DEPTH * n_pool, PAGE_ROWS), mc, mt, tm=512)
    wt3 = wt3.reshape(DEPTH * n_pool, 1, 2 * PAGE_ROWS)
    mn = jnp.pad(mc[:t_new * H_C, :t_new * H_C], ((0, LANE - t_new * H_C),) * 2)
    memk = cache_mem_k.reshape(DEPTH, db, N_MEM * H_C, HD)
    memv = cache_mem_v.reshape(DEPTH, db, N_MEM * H_C, HD)

    xp = x_prompt.reshape(bp * seq, D_MODEL)
    xs = x_sample.reshape(db * t_new, D_MODEL)
    mem2d = mem_prompt.reshape(bp * N_MEM, D_MODEL)
    outs = [[] for _ in range(11)]
    for l in range(DEPTH):
        z, lf_t, c_t = _in_proj(xp, w_main[l], wft[l], bfb[l], tm=1024, tiles_per_seq=seq // 1024)
        mkv = _mm(mem2d, wmkv[l], tm=bp * N_MEM, tn=2 * BRANCH_W)
        (oa,) = _branch_a(z, ln_v_g[l], ln_v_b[l], w_s[l], bsb_p[l], mask_p, tm=512, emit_v=False)
        ob, nconv = _branch_b(z, wdw[l], b_dw[l], ln_c_g[l], ln_c_b[l], wpw[l], b_pw[l], batch=bp, tm=512)
        oc = _fox_prompt(z, c_t, batch=bp, tq=512)
        om = _mem_attn_prompt(z, mkv, batch=bp, tq=512)
        hm = _merge((oa, ob, oc, om), z, wb[l], tm=512)
        xp = _out_ln(hm, wo[l], xp, ln_g[l], ln_b[l], tm=256)
        outs[0].append(z[:, KC * LANE:KC * LANE + BRANCH_W].reshape(bp, seq, H_C, HD))
        outs[1].append(z[:, VC * LANE:VC * LANE + BRANCH_W].reshape(bp, seq, H_C, HD))
        outs[2].append(lf_t[:H_C].T.reshape(bp, seq, H_C))
        outs[3].append(nconv)
        outs[4].append(mkv[:, :BRANCH_W].reshape(bp, N_MEM, H_C, HD))
        outs[5].append(mkv[:, BRANCH_W:].reshape(bp, N_MEM, H_C, HD))

        m_s = db * t_new
        zs, lfs_t, _ = _in_proj(xs, w_main[l], wft[l], bfb[l], tm=m_s, tiles_per_seq=1)
        oa_s, v_rows = _branch_a(zs, ln_v_g[l], ln_v_b[l], ws_s[l], bsb_s[l], mask_s, tm=m_s, emit_v=True)
        ob_s, h_glu = _branch_b_sample(zs.reshape(db, t_new, Z_COLS), state_conv[l], wdw[l], b_dw[l], ln_c_g[l],
                                       ln_c_b[l], wpw[l], b_pw[l])
        heads = lambda c: zs[:, c * LANE:c * LANE + BRANCH_W].reshape(db, t_new * H_C, HD)
        pad_new = lambda a: jnp.pad(a, ((0, 0), (0, LANE - t_new * H_C), (0, 0)))
        lfn = lfs_t[:H_C].T.reshape(db, 1, t_new * H_C)
        lfn = jnp.pad(lfn, ((0, 0), (0, 7), (0, LANE - t_new * H_C)))
        oc_s = _fox_sample(page_table + l * n_pool, heads(QC), heads(GC), pad_new(heads(KC)), pad_new(heads(VC)),
                           lfn, mn, kflat, vflat, wt3, n_new=t_new)
        om_s = _mem_attn_sample(heads(QM), heads(GM), memk[l], memv[l])
        flat = lambda a: a.reshape(m_s, BRANCH_W)
        hm_s = _merge((oa_s, flat(ob_s), flat(oc_s), flat(om_s)), zs, wb[l], tm=m_s)
        xs = _out_ln(hm_s, wo[l], xs, ln_g[l], ln_b[l], tm=m_s)
        outs[6].append(zs[:, KC * LANE:KC * LANE + BRANCH_W].reshape(db, t_new, H_C, HD))
        outs[7].append(zs[:, VC * LANE:VC * LANE + BRANCH_W].reshape(db, t_new, H_C, HD))
        outs[8].append(lfs_t[:H_C].T.reshape(db, t_new, H_C))
        outs[9].append(jnp.concatenate([state_conv[l][:, t_new:], h_glu], axis=1))
        outs[10].append(v_rows.reshape(db, t_new, BRANCH_W))

    return (xp.reshape(bp, seq, D_MODEL), xs.reshape(db, t_new, D_MODEL)) + tuple(jnp.stack(o) for o in outs)
```

```python
import functools
import math

import jax
import jax.numpy as jnp
from jax import lax
from jax.experimental import pallas as pl
from jax.experimental.pallas import tpu as pltpu

F32 = jnp.float32
BF16 = jnp.bfloat16

D_MODEL = 2048
DEPTH = 2
BRANCH_W = 512
N_BRANCH = 4
CHUNK = 128
A_GROUPS = 4
CONV_W = 31
H_C = 4
HD = 128
N_MEM = 256
PAGE_SIZE = 128
LN_EPS = 1e-5
ALPHA = (2 * DEPTH) ** 0.25
ATT_SCALE = HD ** -0.5
LOG2E = math.log2(math.e)
NEG = -1e30

LANE = 128
PAGE_ROWS = PAGE_SIZE * H_C
F_COL = 9 * BRANCH_W
Z_COLS = 28 * BRANCH_W
QC, KC, VC, GC, QM, GM = 24, 28, 32, 36, 40, 44
GATE_BLK = 12
PAGES_PER_STEP = 16
VMEM_LIMIT = 48 * 1024 * 1024


def _cparams(n_axes, vmem=VMEM_LIMIT):
    return pltpu.CompilerParams(dimension_semantics=("arbitrary",) * n_axes, vmem_limit_bytes=vmem)


def _ln(x, g, b):
    mu = jnp.mean(x, axis=-1, keepdims=True)
    xc = x - mu
    var = jnp.mean(xc * xc, axis=-1, keepdims=True)
    return xc * lax.rsqrt(var + LN_EPS) * g + b


def _silu(x):
    return x * jax.nn.sigmoid(x)


def _log_sigmoid(x):
    return jnp.minimum(x, 0.0) - jnp.log1p(jnp.exp(-jnp.abs(x)))


def _dot(a, b):
    return jnp.dot(a, b, preferred_element_type=F32)


def _dot_nt(a, b):
    return lax.dot_general(a, b, (((1,), (1,)), ((), ())), preferred_element_type=F32)


def _split3(x):
    hi = x.astype(BF16)
    r = x - hi.astype(F32)
    mid = r.astype(BF16)
    lo = (r - mid.astype(F32)).astype(BF16)
    return hi, mid, lo


def _dot_exact01(x, m01):
    hi, mid, lo = _split3(x)
    return _dot(hi, m01) + _dot(mid, m01) + _dot(lo, m01)


def _dot_exact01_left(m01, x):
    hi, mid, lo = _split3(x)
    return _dot(m01, hi) + _dot(m01, mid) + _dot(m01, lo)


W_SHIFT_ROWS = 256


def _prep_w_kernel(a_ref, b_ref, o_ref):
    j = pl.program_id(1)

    @pl.when(j < F_COL // BRANCH_W)
    def _():
        o_ref[0] = a_ref[0].astype(BF16)

    @pl.when(j >= F_COL // BRANCH_W)
    def _():
        width = BRANCH_W + LANE
        for r in range(D_MODEL // W_SHIFT_ROWS):
            rs = slice(r * W_SHIFT_ROWS, (r + 1) * W_SHIFT_ROWS)
            x = jnp.concatenate([a_ref[0, rs, :], b_ref[0, rs, :]], axis=1)
            o_ref[0, rs, :] = pltpu.roll(x, width - H_C, axis=1)[:, :BRANCH_W].astype(BF16)


def _prep_w(w_in):
    depth = w_in.shape[0]
    lanes_per_blk = BRANCH_W // LANE
    return pl.pallas_call(
        _prep_w_kernel,
        grid=(depth, Z_COLS // BRANCH_W),
        in_specs=[pl.BlockSpec((1, D_MODEL, BRANCH_W), lambda l, j: (l, 0, j)),
                  pl.BlockSpec((1, D_MODEL, LANE), lambda l, j: (l, 0, (j + 1) * lanes_per_blk))],
        out_specs=pl.BlockSpec((1, D_MODEL, BRANCH_W), lambda l, j: (l, 0, j)),
        out_shape=jax.ShapeDtypeStruct((depth, D_MODEL, Z_COLS), BF16),
        compiler_params=_cparams(2),
        name="prep_w",
    )(w_in, w_in)


K_TILE = KC * LANE // BRANCH_W
V_TILE = VC * LANE // BRANCH_W


def _in_proj_kernel(x_ref, w_ref, wf_ref, bf_ref, z_ref, lf_ref, k_ref, v_ref, *rest, tm, tiles_per_seq, emit_c):
    if emit_c:
        c_ref, xb_ref, carry_ref = rest
    else:
        (xb_ref,) = rest
    i = pl.program_id(0)
    j = pl.program_id(1)

    @pl.when(j == 0)
    def _():
        xb = x_ref[...].astype(BF16)
        xb_ref[...] = xb
        lf = _log_sigmoid(_dot(xb, wf_ref[...]) + bf_ref[...])
        lf_ref[...] = lf
        if emit_c:
            @pl.when(lax.rem(i, tiles_per_seq) == 0)
            def _():
                carry_ref[...] = jnp.zeros_like(carry_ref)

            row = lax.broadcasted_iota(jnp.int32, (LANE, LANE), 0)
            col = lax.broadcasted_iota(jnp.int32, (LANE, LANE), 1)
            lower = jnp.where(col <= row, 1.0, 0.0).astype(BF16)
            carry = carry_ref[...]
            for r in range(tm // LANE):
                rs = slice(r * LANE, (r + 1) * LANE)
                cblk = _dot_exact01_left(lower, lf[rs, :]) + carry
                carry = cblk[LANE - 1:LANE, :]
                c2 = cblk * LOG2E
                for h in range(H_C):
                    c_ref[h, rs, :] = jnp.broadcast_to(c2[:, h:h + 1], (LANE, LANE))
            carry_ref[...] = carry

    acc = _dot(xb_ref[...], w_ref[...])
    z_ref[...] = acc

    def heads_out(o_ref):
        for h in range(H_C):
            o_ref[:, h, :] = acc[:, h * HD:(h + 1) * HD]

    @pl.when(j == K_TILE)
    def _():
        heads_out(k_ref)

    @pl.when(j == V_TILE)
    def _():
        heads_out(v_ref)


def _in_proj(x, w_main, wf, bfb, *, tm, tiles_per_seq, emit_c):
    m = x.shape[0]
    tn = BRANCH_W
    kern = functools.partial(_in_proj_kernel, tm=tm, tiles_per_seq=tiles_per_seq, emit_c=emit_c)
    heads = pl.BlockSpec((tm, H_C, HD), lambda i, j: (i, 0, 0))
    out_specs = [pl.BlockSpec((tm, tn), lambda i, j: (i, j)),
                 pl.BlockSpec((tm, LANE), lambda i, j: (i, 0)), heads, heads]
    out_shape = [jax.ShapeDtypeStruct((m, Z_COLS), F32), jax.ShapeDtypeStruct((m, LANE), F32),
                 jax.ShapeDtypeStruct((m, H_C, HD), F32), jax.ShapeDtypeStruct((m, H_C, HD), F32)]
    scratch = [pltpu.VMEM((tm, D_MODEL), BF16)]
    if emit_c:
        out_specs.append(pl.BlockSpec((H_C, tm, LANE), lambda i, j: (0, i, 0)))
        out_shape.append(jax.ShapeDtypeStruct((H_C, m, LANE), F32))
        scratch.append(pltpu.VMEM((1, LANE), F32))
    return pl.pallas_call(
        kern,
        grid=(m // tm, Z_COLS // tn),
        in_specs=[
            pl.BlockSpec((tm, D_MODEL), lambda i, j: (i, 0)),
            pl.BlockSpec((D_MODEL, tn), lambda i, j: (0, j)),
            pl.BlockSpec((D_MODEL, LANE), lambda i, j: (0, 0)),
            pl.BlockSpec((1, LANE), lambda i, j: (0, 0)),
        ],
        out_specs=out_specs,
        out_shape=out_shape,
        scratch_shapes=scratch,
        compiler_params=_cparams(2),
        name="in_proj",
    )(x, w_main, wf, bfb)


def _mm_kernel(x_ref, w_ref, o_ref):
    o_ref[...] = _dot(x_ref[...].astype(BF16), w_ref[...])


def _mm(x, w, *, tm, tn):
    m, k = x.shape
    n = w.shape[1]
    return pl.pallas_call(
        _mm_kernel,
        grid=(m // tm, n // tn),
        in_specs=[pl.BlockSpec((tm, k), lambda i, j: (i, 0)), pl.BlockSpec((k, tn), lambda i, j: (0, j))],
        out_specs=pl.BlockSpec((tm, tn), lambda i, j: (i, j)),
        out_shape=jax.ShapeDtypeStruct((m, n), F32),
        compiler_params=_cparams(2),
        name="mem_kv_proj",
    )(x, w)


def _branch_a_kernel(u_ref, v_ref, g_ref, lng_ref, lnb_ref, ws_ref, bsb_ref, mask_ref, o_ref, *rest,
                     tm, emit_v):
    u = jax.nn.gelu(u_ref[...])
    v = _ln(jax.nn.gelu(v_ref[...]), lng_ref[...], lnb_ref[...])
    if emit_v:
        rest[0][...] = v
    gate = _silu(g_ref[...])
    keep = mask_ref[...] > 0.0
    for g in range(A_GROUPS):
        wg = jnp.where(keep, ws_ref[g], 0.0).astype(BF16)
        cs = slice(g * LANE, (g + 1) * LANE)
        for c in range(tm // CHUNK):
            rs = slice(c * CHUNK, (c + 1) * CHUNK)
            s = _dot(wg, v[rs, cs].astype(BF16)) + bsb_ref[g]
            o_ref[rs, cs] = (u[rs, cs] * s * gate[rs, cs]).astype(o_ref.dtype)


def _branch_a(z, lng, lnb, ws, bsb, mask, *, tm, emit_v):
    m = z.shape[0]
    blk = lambda c: pl.BlockSpec((tm, BRANCH_W), lambda i, c=c: (i, c))
    vec = pl.BlockSpec((1, BRANCH_W), lambda i: (0, 0))
    cube = pl.BlockSpec((A_GROUPS, CHUNK, CHUNK), lambda i: (0, 0, 0))
    out_specs = [pl.BlockSpec((tm, BRANCH_W), lambda i: (i, 0))]
    out_shape = [jax.ShapeDtypeStruct((m, BRANCH_W), BF16)]
    if emit_v:
        out_specs.append(pl.BlockSpec((tm, BRANCH_W), lambda i: (i, 0)))
        out_shape.append(jax.ShapeDtypeStruct((m, BRANCH_W), F32))
    return pl.pallas_call(
        functools.partial(_branch_a_kernel, tm=tm, emit_v=emit_v),
        grid=(m // tm,),
        in_specs=[blk(0), blk(1), blk(2), vec, vec, cube, cube,
                  pl.BlockSpec((CHUNK, CHUNK), lambda i: (0, 0))],
        out_specs=out_specs,
        out_shape=out_shape,
        compiler_params=_cparams(1),
        name="branch_a",
    )(z, z, z, lng, lnb, ws, bsb, mask)


CONV_PAD = 32


def _conv_tail(y, bdw_ref, lng_ref, lnb_ref, wpw_ref, bpw_ref, gate):
    y = _silu(_ln(y + bdw_ref[...], lng_ref[...], lnb_ref[...]))
    return (_dot(y.astype(BF16), wpw_ref[...]) + bpw_ref[...]) * _silu(gate)


def _branch_b_kernel(a_ref, b_ref, g_ref, wdw_ref, bdw_ref, lng_ref, lnb_ref, wpw_ref, bpw_ref,
                     o_ref, nc_ref, hp_ref, *, tm):
    t = pl.program_id(1)

    @pl.when(t == 0)
    def _():
        hp_ref[0:CONV_PAD, :] = jnp.zeros((CONV_PAD, BRANCH_W), F32)

    hp_ref[CONV_PAD:CONV_PAD + tm, :] = a_ref[...] * jax.nn.sigmoid(b_ref[...])
    off = CONV_PAD - (CONV_W - 1)
    acc = hp_ref[pl.ds(off, tm), :] * wdw_ref[0:1, :]
    for j in range(1, CONV_W):
        acc = acc + hp_ref[pl.ds(off + j, tm), :] * wdw_ref[j:j + 1, :]
    o_ref[...] = _conv_tail(acc, bdw_ref, lng_ref, lnb_ref, wpw_ref, bpw_ref, g_ref[...]).astype(o_ref.dtype)
    nc_ref[0] = hp_ref[pl.ds(CONV_PAD + tm - (CONV_W - 1), CONV_W - 1), :]
    hp_ref[0:CONV_PAD, :] = hp_ref[tm:tm + CONV_PAD, :]


def _branch_b(z, wdw, bdw, lng, lnb, wpw, bpw, *, batch, tm):
    m = z.shape[0]
    nt = m // batch // tm
    blk = lambda c: pl.BlockSpec((tm, BRANCH_W), lambda b, t, c=c: (b * nt + t, c))
    vec = pl.BlockSpec((1, BRANCH_W), lambda b, t: (0, 0))
    return pl.pallas_call(
        functools.partial(_branch_b_kernel, tm=tm),
        grid=(batch, nt),
        in_specs=[blk(3), blk(4), blk(5),
                  pl.BlockSpec((CONV_PAD, BRANCH_W), lambda b, t: (0, 0)), vec, vec, vec,
                  pl.BlockSpec((BRANCH_W, BRANCH_W), lambda b, t: (0, 0)), vec],
        out_specs=[pl.BlockSpec((tm, BRANCH_W), lambda b, t: (b * nt + t, 0)),
                   pl.BlockSpec((1, CONV_W - 1, BRANCH_W), lambda b, t: (b, 0, 0))],
        out_shape=[jax.ShapeDtypeStruct((m, BRANCH_W), BF16),
                   jax.ShapeDtypeStruct((batch, CONV_W - 1, BRANCH_W), F32)],
        scratch_shapes=[pltpu.VMEM((tm + CONV_PAD, BRANCH_W), F32)],
        compiler_params=_cparams(2),
        name="branch_b",
    )(z, z, z, wdw, bdw, lng, lnb, wpw, bpw)


def _branch_b_sample_kernel(a_ref, b_ref, g_ref, st_ref, wdw_ref, bdw_ref, lng_ref, lnb_ref, wpw_ref,
                            bpw_ref, o_ref, h_ref, hp_ref, *, t_new):
    n_hist = CONV_W - 1
    h = a_ref[0] * jax.nn.sigmoid(b_ref[0])
    h_ref[0] = h
    hp_ref[CONV_PAD:CONV_PAD + 8, :] = jnp.zeros((8, BRANCH_W), F32)
    hp_ref[0:n_hist, :] = st_ref[0]
    hp_ref[n_hist:n_hist + t_new, :] = h
    acc = hp_ref[pl.ds(0, 8), :] * wdw_ref[0:1, :]
    for j in range(1, CONV_W):
        acc = acc + hp_ref[pl.ds(j, 8), :] * wdw_ref[j:j + 1, :]
    y = _silu(_ln(acc + bdw_ref[...], lng_ref[...], lnb_ref[...]))
    out = _dot(y.astype(BF16), wpw_ref[...]) + bpw_ref[...]
    o_ref[0] = out[0:t_new, :] * _silu(g_ref[0])


def _branch_b_sample(z3, state, layer, wdw, bdw, lng, lnb, wpw, bpw):
    nb, t_new, _ = z3.shape
    blk = lambda c: pl.BlockSpec((1, t_new, BRANCH_W), lambda b, c=c: (b, 0, c))
    vec = pl.BlockSpec((1, BRANCH_W), lambda b: (0, 0))
    row = pl.BlockSpec((1, t_new, BRANCH_W), lambda b: (b, 0, 0))
    return pl.pallas_call(
        functools.partial(_branch_b_sample_kernel, t_new=t_new),
        grid=(nb,),
        in_specs=[blk(3), blk(4), blk(5),
                  pl.BlockSpec((1, CONV_W - 1, BRANCH_W), lambda b: (layer * nb + b, 0, 0)),
                  pl.BlockSpec((CONV_PAD, BRANCH_W), lambda b: (0, 0)), vec, vec, vec,
                  pl.BlockSpec((BRANCH_W, BRANCH_W), lambda b: (0, 0)), vec],
        out_specs=[row, row],
        out_shape=[jax.ShapeDtypeStruct((nb, t_new, BRANCH_W), F32)] * 2,
        scratch_shapes=[pltpu.VMEM((CONV_PAD + 8, BRANCH_W), F32)],
        compiler_params=_cparams(1),
        name="branch_b_sample",
    )(z3, z3, z3, state, wdw, bdw, lng, lnb, wpw, bpw)


FOX_STRIPS = 1


def _fox_kernel(q_ref, k_ref, v_ref, c_ref, g_ref, o_ref, kb_ref, vt_ref, m_ref, l_ref, acc_ref, *, tq, nk):
    qi = pl.program_id(2)

    @pl.when(qi == 0)
    def _():
        for j in range(nk):
            rs = slice(j * tq, (j + 1) * tq)
            kb_ref[j] = k_ref[rs, :].astype(BF16)
            vt_ref[j] = v_ref[rs, :].T.astype(BF16)

    m_ref[...] = jnp.full_like(m_ref, NEG)
    l_ref[...] = jnp.zeros_like(l_ref)
    acc_ref[...] = jnp.zeros_like(acc_ref)
    ws = tq // FOX_STRIPS
    qs = [(q_ref[s * ws:(s + 1) * ws, :] * (ATT_SCALE * LOG2E)).astype(BF16) for s in range(FOX_STRIPS)]

    def block(kj, diagonal):
        c = c_ref[0, pl.ds(pl.multiple_of(kj * tq, tq), tq), :]
        cw = jnp.concatenate([c] * (ws // LANE), axis=1)
        k = kb_ref[kj]
        vt = vt_ref[kj]
        for s in range(FOX_STRIPS):
            t = _dot_nt(k, qs[s]) - cw
            if diagonal:
                key = lax.broadcasted_iota(jnp.int32, (tq, ws), 0)
                qry = lax.broadcasted_iota(jnp.int32, (tq, ws), 1) + s * ws
                t = jnp.where(key <= qry, t, NEG)
            m_prev = m_ref[s]
            m_new = jnp.maximum(m_prev, jnp.max(t, axis=0, keepdims=True))
            alpha = jnp.exp2(m_prev - m_new)
            p = jnp.exp2(t - m_new)
            l_ref[s] = alpha * l_ref[s] + jnp.sum(p, axis=0, keepdims=True)
            acc_ref[s] = alpha * acc_ref[s] + _dot(vt, p.astype(BF16))
            m_ref[s] = m_new

    def body(kj, carry):
        block(kj, False)
        return carry

    lax.fori_loop(0, qi, body, 0)
    block(qi, True)
    for s in range(FOX_STRIPS):
        rs = slice(s * ws, (s + 1) * ws)
        o_ref[rs, :] = ((acc_ref[s] / l_ref[s]).T * _silu(g_ref[rs, :])).astype(o_ref.dtype)


def _fox_prompt(z, c_rep, *, batch, tq):
    m = z.shape[0]
    seq = m // batch
    nq = seq // tq
    ws = tq // FOX_STRIPS
    qmap = lambda c: (lambda b, h, qi: (b * nq + qi, c + h))
    kvmap = lambda c: (lambda b, h, qi: (b, c + h))
    return pl.pallas_call(
        functools.partial(_fox_kernel, tq=tq, nk=nq),
        grid=(batch, H_C, nq),
        in_specs=[pl.BlockSpec((tq, HD), qmap(QC)),
                  pl.BlockSpec((seq, HD), kvmap(KC)),
                  pl.BlockSpec((seq, HD), kvmap(VC)),
                  pl.BlockSpec((1, seq, LANE), lambda b, h, qi: (h, b, 0)),
                  pl.BlockSpec((tq, HD), qmap(GC))],
        out_specs=pl.BlockSpec((tq, HD), lambda b, h, qi: (b * nq + qi, h)),
        out_shape=jax.ShapeDtypeStruct((m, BRANCH_W), BF16),
        scratch_shapes=[pltpu.VMEM((nq, tq, HD), BF16), pltpu.VMEM((nq, HD, tq), BF16),
                        pltpu.VMEM((FOX_STRIPS, 1, ws), F32), pltpu.VMEM((FOX_STRIPS, 1, ws), F32),
                        pltpu.VMEM((FOX_STRIPS, HD, ws), F32)],
        compiler_params=_cparams(3),
        name="fox_prompt",
    )(z, z, z, c_rep, z)


def _mem_attn_kernel(q_ref, k_ref, v_ref, g_ref, o_ref):
    s = _dot_nt(q_ref[...].astype(BF16), k_ref[...].astype(BF16)) * ATT_SCALE
    p = jnp.exp(s - jnp.max(s, axis=1, keepdims=True))
    o = _dot(p.astype(BF16), v_ref[...].astype(BF16)) / jnp.sum(p, axis=1, keepdims=True)
    o_ref[...] = (o * _silu(g_ref[...])).astype(o_ref.dtype)


def _mem_attn_prompt(z, mkv, *, batch, tq):
    m = z.shape[0]
    nq = m // batch // tq
    return pl.pallas_call(
        _mem_attn_kernel,
        grid=(batch, H_C, nq),
        in_specs=[pl.BlockSpec((tq, HD), lambda b, h, qi: (b * nq + qi, QM + h)),
                  pl.BlockSpec((N_MEM, HD), lambda b, h, qi: (b, h)),
                  pl.BlockSpec((N_MEM, HD), lambda b, h, qi: (b, H_C + h)),
                  pl.BlockSpec((tq, HD), lambda b, h, qi: (b * nq + qi, GM + h))],
        out_specs=pl.BlockSpec((tq, HD), lambda b, h, qi: (b * nq + qi, h)),
        out_shape=jax.ShapeDtypeStruct((m, BRANCH_W), BF16),
        compiler_params=_cparams(3),
        name="mem_attn_prompt",
    )(z, mkv, mkv, z)


def _head_match(rows, cols):
    r = lax.broadcasted_iota(jnp.int32, (rows, cols), 0)
    c = lax.broadcasted_iota(jnp.int32, (rows, cols), 1)
    return r, c, (r & (H_C - 1)) == (c & (H_C - 1))


def _mem_attn_sample_kernel(q_ref, g_ref, k_ref, v_ref, o_ref):
    s = _dot_nt(q_ref[0].astype(BF16), k_ref[0].astype(BF16)) * ATT_SCALE
    _, _, same = _head_match(s.shape[0], s.shape[1])
    s = jnp.where(same, s, NEG)
    p = jnp.exp(s - jnp.max(s, axis=1, keepdims=True))
    o = _dot(p.astype(BF16), v_ref[0].astype(BF16)) / jnp.sum(p, axis=1, keepdims=True)
    o_ref[0] = o * _silu(g_ref[0])


def _mem_attn_sample(q16, g16, mk, mv, layer):
    nb, nr, _ = q16.shape
    nm = mk.shape[1]
    row = pl.BlockSpec((1, nr, HD), lambda b: (b, 0, 0))
    mem = pl.BlockSpec((1, nm, HD), lambda b: (layer * nb + b, 0, 0))
    return pl.pallas_call(
        _mem_attn_sample_kernel,
        grid=(nb,),
        in_specs=[row, row, mem, mem],
        out_specs=row,
        out_shape=jax.ShapeDtypeStruct((nb, nr, HD), F32),
        compiler_params=_cparams(1),
        name="mem_attn_sample",
    )(q16, g16, mk, mv)


def _logf_pages_kernel(x_ref, mc_ref, mt_ref, o_ref):
    x = x_ref[...]
    o_ref[:, 0:PAGE_ROWS] = _dot_exact01(x, mc_ref[...])
    o_ref[:, PAGE_ROWS:2 * PAGE_ROWS] = _dot_exact01(x, mt_ref[...])


def _logf_pages(lf_flat, mc, mt, *, tm):
    n = lf_flat.shape[0]
    mat = pl.BlockSpec((PAGE_ROWS, PAGE_ROWS), lambda i: (0, 0))
    return pl.pallas_call(
        _logf_pages_kernel,
        grid=(n // tm,),
        in_specs=[pl.BlockSpec((tm, PAGE_ROWS), lambda i: (i, 0)), mat, mat],
        out_specs=pl.BlockSpec((tm, 2 * PAGE_ROWS), lambda i: (i, 0)),
        out_shape=jax.ShapeDtypeStruct((n, 2 * PAGE_ROWS), F32),
        compiler_params=_cparams(1),
        name="logf_pages",
    )(lf_flat, mc, mt)


def _fox_sample_kernel(pt_ref, q_ref, g_ref, kn_ref, vn_ref, lfn_ref, mn_ref, *rest, n_pages, n_new):
    del pt_ref
    np_ = PAGES_PER_STEP
    k_refs, v_refs, wt_refs = rest[:np_], rest[np_:2 * np_], rest[2 * np_:3 * np_]
    o_ref, m_ref, l_ref, acc_ref, carry_ref = rest[3 * np_:]
    step = pl.program_id(1)
    nr = q_ref.shape[1]

    @pl.when(step == 0)
    def _():
        m_ref[...] = jnp.full_like(m_ref, NEG)
        l_ref[...] = jnp.zeros_like(l_ref)
        acc_ref[...] = jnp.zeros_like(acc_ref)
        carry_ref[...] = jnp.zeros_like(carry_ref)

    q = q_ref[0].astype(BF16)

    def update(s_list, v_list):
        m_prev = m_ref[...]
        m_new = m_prev
        for s in s_list:
            m_new = jnp.maximum(m_new, jnp.max(s, axis=1, keepdims=True))
        alpha = jnp.exp(m_prev - m_new)
        l_new = alpha * l_ref[...]
        acc = alpha * acc_ref[...]
        for s, v in zip(s_list, v_list):
            p = jnp.exp(s - m_new)
            l_new = l_new + jnp.sum(p, axis=1, keepdims=True)
            acc = acc + _dot(p.astype(BF16), v)
        m_ref[...] = m_new
        l_ref[...] = l_new
        acc_ref[...] = acc

    _, _, same = _head_match(nr, PAGE_ROWS)
    carry = carry_ref[...]
    s_list, v_list = [], []
    for i in range(np_):
        wt = wt_refs[i][0]
        ck = carry + wt[:, 0:PAGE_ROWS]
        carry = carry + wt[:, PAGE_ROWS:2 * PAGE_ROWS]
        s = _dot_nt(q, k_refs[i][0].astype(BF16)) * ATT_SCALE - ck
        s_list.append(jnp.where(same, s, NEG))
        v_list.append(v_refs[i][0].astype(BF16))
    carry_ref[...] = carry
    update(s_list, v_list)

    @pl.when(step == n_pages // np_ - 1)
    def _():
        r, c, same_n = _head_match(nr, LANE)
        cn = carry[:, 0:LANE] + _dot_exact01(lfn_ref[0], mn_ref[...])[0:1, :]
        s = _dot_nt(q, kn_ref[0].astype(BF16)) * ATT_SCALE - cn
        ok = same_n & (c < n_new * H_C) & ((c >> 2) <= (r >> 2))
        update([jnp.where(ok, s, NEG)], [vn_ref[0].astype(BF16)])
        o_ref[0] = acc_ref[...] / l_ref[...] * _silu(g_ref[0])


def _fox_sample(pt, q16, g16, kn, vn, lfn, mn, kflat, vflat, wt3, *, n_new):
    nb, nr, _ = q16.shape
    n_pages = pt.shape[1]
    np_ = PAGES_PER_STEP
    row = pl.BlockSpec((1, nr, HD), lambda b, s, pt: (b, 0, 0))
    new = pl.BlockSpec((1, LANE, HD), lambda b, s, pt: (b, 0, 0))
    page = lambda i: (lambda b, s, pt: (pt[b, s * np_ + i], 0, 0))
    in_specs = [row, row, new, new,
                pl.BlockSpec((1, 8, LANE), lambda b, s, pt: (b, 0, 0)),
                pl.BlockSpec((LANE, LANE), lambda b, s, pt: (0, 0))]
    in_specs += [pl.BlockSpec((1, PAGE_ROWS, HD), page(i)) for i in range(np_)]
    in_specs += [pl.BlockSpec((1, PAGE_ROWS, HD), page(i)) for i in range(np_)]
    in_specs += [pl.BlockSpec((1, 1, 2 * PAGE_ROWS), page(i)) for i in range(np_)]
    grid_spec = pltpu.PrefetchScalarGridSpec(
        num_scalar_prefetch=1,
        grid=(nb, n_pages // np_),
        in_specs=in_specs,
        out_specs=pl.BlockSpec((1, nr, HD), lambda b, s, pt: (b, 0, 0)),
        scratch_shapes=[pltpu.VMEM((nr, 1), F32), pltpu.VMEM((nr, 1), F32), pltpu.VMEM((nr, HD), F32),
                        pltpu.VMEM((1, PAGE_ROWS), F32)],
    )
    return pl.pallas_call(
        functools.partial(_fox_sample_kernel, n_pages=n_pages, n_new=n_new),
        grid_spec=grid_spec,
        out_shape=jax.ShapeDtypeStruct((nb, nr, HD), F32),
        compiler_params=_cparams(2),
        name="fox_sample",
    )(pt, q16, g16, kn, vn, lfn, mn, *([kflat] * np_), *([vflat] * np_), *([wt3] * np_))


def _merge_kernel(*refs):
    o_refs, wb_ref, gate_refs, h_ref = refs[0:4], refs[4], refs[5:9], refs[9]
    acc = None
    for br in range(N_BRANCH):
        term = jax.nn.sigmoid(gate_refs[br][...]) * _dot(o_refs[br][...].astype(BF16), wb_ref[br])
        acc = term if acc is None else acc + term
    h_ref[...] = acc.astype(h_ref.dtype)


def _merge(outs, z, wb, *, tm):
    m = z.shape[0]
    tn = BRANCH_W
    nj = D_MODEL // tn
    o_spec = pl.BlockSpec((tm, BRANCH_W), lambda i, j: (i, 0))
    gate = lambda br: pl.BlockSpec((tm, tn), lambda i, j, br=br: (i, GATE_BLK + br * nj + j))
    return pl.pallas_call(
        _merge_kernel,
        grid=(m // tm, nj),
        in_specs=[o_spec] * 4 + [pl.BlockSpec((N_BRANCH, BRANCH_W, tn), lambda i, j: (0, 0, j))]
                 + [gate(br) for br in range(N_BRANCH)],
        out_specs=pl.BlockSpec((tm, tn), lambda i, j: (i, j)),
        out_shape=jax.ShapeDtypeStruct((m, D_MODEL), BF16),
        compiler_params=_cparams(2),
        name="merge",
    )(*outs, wb, z, z, z, z)


def _out_ln_kernel(h_ref, w_ref, x_ref, g_ref, b_ref, y_ref):
    y = ALPHA * x_ref[...] + _dot(h_ref[...], w_ref[...])
    y_ref[...] = _ln(y, g_ref[...], b_ref[...])


def _out_ln(hm, wo, x, lng, lnb, *, tm):
    m = x.shape[0]
    rows = pl.BlockSpec((tm, D_MODEL), lambda i: (i, 0))
    vec = pl.BlockSpec((1, D_MODEL), lambda i: (0, 0))
    return pl.pallas_call(
        _out_ln_kernel,
        grid=(m // tm,),
        in_specs=[rows, pl.BlockSpec((D_MODEL, D_MODEL), lambda i: (0, 0)), rows, vec, vec],
        out_specs=rows,
        out_shape=jax.ShapeDtypeStruct((m, D_MODEL), F32),
        compiler_params=_cparams(1),
        name="out_ln",
    )(hm, wo, x, lng, lnb)


def _tok_head_matrices():
    i = jnp.arange(PAGE_ROWS)
    same = (i[:, None] % H_C) == (i[None, :] % H_C)
    mc = (same & (i[:, None] // H_C <= i[None, :] // H_C)).astype(BF16)
    mt = same.astype(BF16)
    return mc, mt


def kernel(x_prompt, x_sample, mem_prompt, cache_k, cache_v, cache_logf, cache_mem_k, cache_mem_v, state_conv,
           page_table, w_in, w_mem_k, w_mem_v, ln_v_g, ln_v_b, w_s, b_s, w_dw, b_dw, ln_c_g, ln_c_b, w_pw, b_pw,
           b_f, w_branch, w_out, ln_g, ln_b):
    bp, seq, _ = x_prompt.shape
    db, t_new, _ = x_sample.shape
    n_pool = cache_k.shape[1]

    w_main = _prep_w(w_in)
    wf = jnp.pad(w_in[:, :, F_COL:F_COL + H_C], ((0, 0), (0, 0), (0, LANE - H_C))).astype(BF16)
    bfb = jnp.pad(b_f, ((0, 0), (0, LANE - H_C)))[:, None, :]
    wb = w_branch.astype(BF16)
    wo = w_out.astype(BF16)
    wpw = w_pw.astype(BF16)
    wmkv = jnp.concatenate([w_mem_k, w_mem_v], axis=2).astype(BF16)
    wdw = jnp.pad(w_dw, ((0, 0), (0, CONV_PAD - CONV_W), (0, 0)))
    vec = lambda a: a[:, None, :]
    ln_v_g, ln_v_b, b_dw, ln_c_g, ln_c_b, b_pw, ln_g, ln_b = map(
        vec, (ln_v_g, ln_v_b, b_dw, ln_c_g, ln_c_b, b_pw, ln_g, ln_b))

    idx = jnp.arange(CHUNK)
    mask_p = (idx[None, :] <= idx[:, None]).astype(F32)
    bsb_p = jnp.broadcast_to(b_s[:, :, :, None], (DEPTH, A_GROUPS, CHUNK, CHUNK))
    reps = CHUNK // t_new
    mask_s = ((idx[:, None] // t_new == idx[None, :] // t_new) & (idx[None, :] <= idx[:, None])).astype(F32)
    ws_s = jnp.tile(w_s[:, :, :t_new, :t_new], (1, 1, reps, reps))
    bsb_s = jnp.broadcast_to(jnp.tile(b_s[:, :, :t_new], (1, 1, reps))[:, :, :, None],
                             (DEPTH, A_GROUPS, CHUNK, CHUNK))

    kflat = cache_k.reshape(DEPTH * n_pool, PAGE_ROWS, HD)
    vflat = cache_v.reshape(DEPTH * n_pool, PAGE_ROWS, HD)
    mc, mt = _tok_head_matrices()
    wt3 = _logf_pages(cache_logf.reshape(DEPTH * n_pool, PAGE_ROWS), mc, mt, tm=512)
    wt3 = wt3.reshape(DEPTH * n_pool, 1, 2 * PAGE_ROWS)
    mn = jnp.pad(mc[:t_new * H_C, :t_new * H_C], ((0, LANE - t_new * H_C),) * 2)
    memk = cache_mem_k.reshape(DEPTH * db, N_MEM * H_C, HD)
    memv = cache_mem_v.reshape(DEPTH * db, N_MEM * H_C, HD)
    state = state_conv.reshape(DEPTH * db, CONV_W - 1, BRANCH_W)

    xp = x_prompt.reshape(bp * seq, D_MODEL)
    xs = x_sample.reshape(db * t_new, D_MODEL)
    mem2d = mem_prompt.reshape(bp * N_MEM, D_MODEL)
    m_s = db * t_new
    outs = [[] for _ in range(11)]
    for l in range(DEPTH):
        z, lf, k_p, v_p, c_rep = _in_proj(xp, w_main[l], wf[l], bfb[l], tm=1024, tiles_per_seq=seq // 1024,
                                          emit_c=True)
        mkv = _mm(mem2d, wmkv[l], tm=bp * N_MEM, tn=2 * BRANCH_W)
        (oa,) = _branch_a(z, ln_v_g[l], ln_v_b[l], w_s[l], bsb_p[l], mask_p, tm=512, emit_v=False)
        ob, nconv = _branch_b(z, wdw[l], b_dw[l], ln_c_g[l], ln_c_b[l], wpw[l], b_pw[l], batch=bp, tm=512)
        oc = _fox_prompt(z, c_rep, batch=bp, tq=512)
        om = _mem_attn_prompt(z, mkv, batch=bp, tq=512)
        hm = _merge((oa, ob, oc, om), z, wb[l], tm=512)
        xp = _out_ln(hm, wo[l], xp, ln_g[l], ln_b[l], tm=256)
        outs[0].append(k_p.reshape(bp, seq, H_C, HD))
        outs[1].append(v_p.reshape(bp, seq, H_C, HD))
        outs[2].append(lf[:, :H_C].reshape(bp, seq, H_C))
        outs[3].append(nconv)
        outs[4].append(mkv[:, :BRANCH_W].reshape(bp, N_MEM, H_C, HD))
        outs[5].append(mkv[:, BRANCH_W:].reshape(bp, N_MEM, H_C, HD))

        zs, lfs, k_s, v_s = _in_proj(xs, w_main[l], wf[l], bfb[l], tm=m_s, tiles_per_seq=1, emit_c=False)
        oa_s, v_rows = _branch_a(zs, ln_v_g[l], ln_v_b[l], ws_s[l], bsb_s[l], mask_s, tm=m_s, emit_v=True)
        ob_s, h_glu = _branch_b_sample(zs.reshape(db, t_new, Z_COLS), state, l, wdw[l], b_dw[l], ln_c_g[l],
                                       ln_c_b[l], wpw[l], b_pw[l])
        heads = lambda c: zs[:, c * LANE:c * LANE + BRANCH_W].reshape(db, t_new * H_C, HD)
        pad_new = lambda a: jnp.pad(a.reshape(db, t_new * H_C, HD), ((0, 0), (0, LANE - t_new * H_C), (0, 0)))
        lfn = lfs[:, :H_C].reshape(db, 1, t_new * H_C)
        lfn = jnp.pad(lfn, ((0, 0), (0, 7), (0, LANE - t_new * H_C)))
        oc_s = _fox_sample(page_table + l * n_pool, heads(QC), heads(GC), pad_new(k_s), pad_new(v_s),
                           lfn, mn, kflat, vflat, wt3, n_new=t_new)
        om_s = _mem_attn_sample(heads(QM), heads(GM), memk, memv, l)
        flat = lambda a: a.reshape(m_s, BRANCH_W)
        hm_s = _merge((oa_s, flat(ob_s), flat(oc_s), flat(om_s)), zs, wb[l], tm=m_s)
        xs = _out_ln(hm_s, wo[l], xs, ln_g[l], ln_b[l], tm=m_s)
        outs[6].append(k_s.reshape(db, t_new, H_C, HD))
        outs[7].append(v_s.reshape(db, t_new, H_C, HD))
        outs[8].append(lfs[:, :H_C].reshape(db, t_new, H_C))
        outs[9].append(jnp.concatenate([state_conv[l][:, t_new:], h_glu], axis=1))
        outs[10].append(v_rows.reshape(db, t_new, BRANCH_W))

    return (xp.reshape(bp, seq, D_MODEL), xs.reshape(db, t_new, D_MODEL)) + tuple(jnp.stack(o) for o in outs)
```

```python
import functools
import math

import jax
import jax.numpy as jnp
from jax import lax
from jax.experimental import pallas as pl
from jax.experimental.pallas import tpu as pltpu

F32 = jnp.float32
BF16 = jnp.bfloat16

D_MODEL = 2048
DEPTH = 2
BRANCH_W = 512
N_BRANCH = 4
CHUNK = 128
A_GROUPS = 4
CONV_W = 31
H_C = 4
HD = 128
N_MEM = 256
PAGE_SIZE = 128
LN_EPS = 1e-5
ALPHA = (2 * DEPTH) ** 0.25
ATT_SCALE = HD ** -0.5
LOG2E = math.log2(math.e)
NEG = -1e30

LANE = 128
SUBLANES = 8
PAGE_ROWS = PAGE_SIZE * H_C
F_COL = 9 * BRANCH_W
Z_COLS = 28 * BRANCH_W
QC, KC, VC, GC, QM, GM = 24, 28, 32, 36, 40, 44
GATE_BLK = 12
PAGES_PER_STEP = 16
VMEM_LIMIT = 48 * 1024 * 1024


def _cparams(n_axes, vmem=VMEM_LIMIT):
    return pltpu.CompilerParams(dimension_semantics=("arbitrary",) * n_axes, vmem_limit_bytes=vmem)


def _ln(x, g, b):
    mu = jnp.mean(x, axis=-1, keepdims=True)
    xc = x - mu
    var = jnp.mean(xc * xc, axis=-1, keepdims=True)
    return xc * lax.rsqrt(var + LN_EPS) * g + b


def _silu(x):
    return x * jax.nn.sigmoid(x)


def _log_sigmoid(x):
    return jnp.minimum(x, 0.0) - jnp.log1p(jnp.exp(-jnp.abs(x)))


def _dot(a, b):
    return jnp.dot(a, b, preferred_element_type=F32)


def _dot_nt(a, b):
    return lax.dot_general(a, b, (((1,), (1,)), ((), ())), preferred_element_type=F32)


def _split3(x):
    hi = x.astype(BF16)
    r = x - hi.astype(F32)
    mid = r.astype(BF16)
    lo = (r - mid.astype(F32)).astype(BF16)
    return hi, mid, lo


def _dot_exact01(x, m01):
    hi, mid, lo = _split3(x)
    return _dot(hi, m01) + _dot(mid, m01) + _dot(lo, m01)


def _dot_exact01_left(m01, x):
    hi, mid, lo = _split3(x)
    return _dot(m01, hi) + _dot(m01, mid) + _dot(m01, lo)


def _prep_w_kernel(a_ref, f_ref, o_ref, wf_ref):
    j = pl.program_id(0)
    for l in range(DEPTH):
        o_ref[l] = a_ref[:, l, :].T.astype(BF16)

    @pl.when(j == 0)
    def _():
        lane = lax.broadcasted_iota(jnp.int32, (D_MODEL, LANE), 1)
        for l in range(DEPTH):
            wf_ref[l] = jnp.where(lane < H_C, f_ref[:, l, :].T, 0.0).astype(BF16)


def _prep_w(w_in):
    wt = jnp.transpose(w_in, (2, 0, 1))
    elems = lambda rows: (pl.Element(rows), pl.Element(DEPTH), pl.Element(D_MODEL))
    tc = BRANCH_W
    return pl.pallas_call(
        _prep_w_kernel,
        grid=(Z_COLS // tc,),
        in_specs=[pl.BlockSpec(elems(tc), lambda j: (j * tc + jnp.where(j >= F_COL // tc, H_C, 0), 0, 0)),
                  pl.BlockSpec(elems(LANE), lambda j: (F_COL, 0, 0))],
        out_specs=[pl.BlockSpec((DEPTH, D_MODEL, tc), lambda j: (0, 0, j)),
                   pl.BlockSpec((DEPTH, D_MODEL, LANE), lambda j: (0, 0, 0))],
        out_shape=[jax.ShapeDtypeStruct((DEPTH, D_MODEL, Z_COLS), BF16),
                   jax.ShapeDtypeStruct((DEPTH, D_MODEL, LANE), BF16)],
        compiler_params=_cparams(1),
        name="prep_w",
    )(wt, wt)


K_TILE = KC * LANE // BRANCH_W
V_TILE = VC * LANE // BRANCH_W


def _in_proj_kernel(x_ref, w_ref, wf_ref, bf_ref, z_ref, lf_ref, k_ref, v_ref, *rest, tm, tiles_per_seq, emit_c):
    if emit_c:
        c_ref, xb_ref, carry_ref = rest
    else:
        (xb_ref,) = rest
    i = pl.program_id(0)
    j = pl.program_id(1)

    @pl.when(j == 0)
    def _():
        xb = x_ref[...].astype(BF16)
        xb_ref[...] = xb
        lf = _log_sigmoid(_dot(xb, wf_ref[...]) + bf_ref[...])
        lf_ref[...] = lf
        if emit_c:
            @pl.when(lax.rem(i, tiles_per_seq) == 0)
            def _():
                carry_ref[...] = jnp.zeros_like(carry_ref)

            row = lax.broadcasted_iota(jnp.int32, (LANE, LANE), 0)
            col = lax.broadcasted_iota(jnp.int32, (LANE, LANE), 1)
            lower = jnp.where(col <= row, 1.0, 0.0).astype(BF16)
            carry = carry_ref[...]
            for r in range(tm // LANE):
                rs = slice(r * LANE, (r + 1) * LANE)
                cblk = _dot_exact01_left(lower, lf[rs, :]) + carry
                carry = cblk[LANE - 1:LANE, :]
                c2 = cblk * LOG2E
                for h in range(H_C):
                    c_ref[h, rs, :] = jnp.broadcast_to(c2[:, h:h + 1], (LANE, LANE))
            carry_ref[...] = carry

    acc = _dot(xb_ref[...], w_ref[...])
    z_ref[...] = acc

    def heads_out(o_ref):
        for h in range(H_C):
            o_ref[:, h, :] = acc[:, h * HD:(h + 1) * HD]

    @pl.when(j == K_TILE)
    def _():
        heads_out(k_ref)

    @pl.when(j == V_TILE)
    def _():
        heads_out(v_ref)


def _in_proj(x, w_main, wf, bfb, layer, *, tm, tiles_per_seq, emit_c):
    m = x.shape[0]
    tn = BRANCH_W
    kern = functools.partial(_in_proj_kernel, tm=tm, tiles_per_seq=tiles_per_seq, emit_c=emit_c)
    heads = pl.BlockSpec((tm, H_C, HD), lambda i, j: (i, 0, 0))
    out_specs = [pl.BlockSpec((tm, tn), lambda i, j: (i, j)),
                 pl.BlockSpec((tm, LANE), lambda i, j: (i, 0)), heads, heads]
    out_shape = [jax.ShapeDtypeStruct((m, Z_COLS), F32), jax.ShapeDtypeStruct((m, LANE), F32),
                 jax.ShapeDtypeStruct((m, H_C, HD), F32), jax.ShapeDtypeStruct((m, H_C, HD), F32)]
    scratch = [pltpu.VMEM((tm, D_MODEL), BF16)]
    if emit_c:
        out_specs.append(pl.BlockSpec((H_C, tm, LANE), lambda i, j: (0, i, 0)))
        out_shape.append(jax.ShapeDtypeStruct((H_C, m, LANE), F32))
        scratch.append(pltpu.VMEM((1, LANE), F32))
    return pl.pallas_call(
        kern,
        grid=(m // tm, Z_COLS // tn),
        in_specs=[
            pl.BlockSpec((tm, D_MODEL), lambda i, j: (i, 0)),
            pl.BlockSpec((None, D_MODEL, tn), lambda i, j: (layer, 0, j)),
            pl.BlockSpec((None, D_MODEL, LANE), lambda i, j: (layer, 0, 0)),
            pl.BlockSpec((1, LANE), lambda i, j: (0, 0)),
        ],
        out_specs=out_specs,
        out_shape=out_shape,
        scratch_shapes=scratch,
        compiler_params=_cparams(2),
        name="in_proj",
    )(x, w_main, wf, bfb)


def _mm_kernel(x_ref, w_ref, o_ref):
    o_ref[...] = _dot(x_ref[...].astype(BF16), w_ref[...])


def _mm(x, w, *, tm, tn):
    m, k = x.shape
    n = w.shape[1]
    return pl.pallas_call(
        _mm_kernel,
        grid=(m // tm, n // tn),
        in_specs=[pl.BlockSpec((tm, k), lambda i, j: (i, 0)), pl.BlockSpec((k, tn), lambda i, j: (0, j))],
        out_specs=pl.BlockSpec((tm, tn), lambda i, j: (i, j)),
        out_shape=jax.ShapeDtypeStruct((m, n), F32),
        compiler_params=_cparams(2),
        name="mem_kv_proj",
    )(x, w)


def _branch_a_kernel(u_ref, v_ref, g_ref, lng_ref, lnb_ref, ws_ref, bsb_ref, mask_ref, o_ref, *rest,
                     tm, emit_v):
    u = jax.nn.gelu(u_ref[...])
    v = _ln(jax.nn.gelu(v_ref[...]), lng_ref[...], lnb_ref[...])
    if emit_v:
        rest[0][...] = v
    gate = _silu(g_ref[...])
    keep = mask_ref[...] > 0.0
    for g in range(A_GROUPS):
        wg = jnp.where(keep, ws_ref[g], 0.0).astype(BF16)
        cs = slice(g * LANE, (g + 1) * LANE)
        for c in range(tm // CHUNK):
            rs = slice(c * CHUNK, (c + 1) * CHUNK)
            s = _dot(wg, v[rs, cs].astype(BF16)) + bsb_ref[g]
            o_ref[rs, cs] = (u[rs, cs] * s * gate[rs, cs]).astype(o_ref.dtype)


def _branch_a(z, lng, lnb, ws, bsb, mask, *, tm, emit_v):
    m = z.shape[0]
    blk = lambda c: pl.BlockSpec((tm, BRANCH_W), lambda i, c=c: (i, c))
    vec = pl.BlockSpec((1, BRANCH_W), lambda i: (0, 0))
    cube = pl.BlockSpec((A_GROUPS, CHUNK, CHUNK), lambda i: (0, 0, 0))
    out_specs = [pl.BlockSpec((tm, BRANCH_W), lambda i: (i, 0))]
    out_shape = [jax.ShapeDtypeStruct((m, BRANCH_W), BF16)]
    if emit_v:
        out_specs.append(pl.BlockSpec((tm, BRANCH_W), lambda i: (i, 0)))
        out_shape.append(jax.ShapeDtypeStruct((m, BRANCH_W), F32))
    return pl.pallas_call(
        functools.partial(_branch_a_kernel, tm=tm, emit_v=emit_v),
        grid=(m // tm,),
        in_specs=[blk(0), blk(1), blk(2), vec, vec, cube, cube,
                  pl.BlockSpec((CHUNK, CHUNK), lambda i: (0, 0))],
        out_specs=out_specs,
        out_shape=out_shape,
        compiler_params=_cparams(1),
        name="branch_a",
    )(z, z, z, lng, lnb, ws, bsb, mask)


CONV_PAD = 32


def _conv_tail(y, bdw_ref, lng_ref, lnb_ref, wpw_ref, bpw_ref, gate):
    y = _silu(_ln(y + bdw_ref[...], lng_ref[...], lnb_ref[...]))
    return (_dot(y.astype(BF16), wpw_ref[...]) + bpw_ref[...]) * _silu(gate)


def _branch_b_kernel(a_ref, b_ref, g_ref, wdw_ref, bdw_ref, lng_ref, lnb_ref, wpw_ref, bpw_ref,
                     o_ref, nc_ref, hp_ref, sw_ref, *, tm):
    t = pl.program_id(1)

    @pl.when(t == 0)
    def _():
        hp_ref[0:CONV_PAD, :] = jnp.zeros((CONV_PAD, BRANCH_W), F32)

    hp_ref[CONV_PAD:CONV_PAD + tm, :] = a_ref[...] * jax.nn.sigmoid(b_ref[...])
    off = CONV_PAD - (CONV_W - 1)
    acc = None
    for r in range(SUBLANES):
        taps = [j for j in range(CONV_W) if (off + j) % SUBLANES == r]
        rows = tm + taps[-1] - taps[0]
        if r:
            sw_ref[0:rows, :] = hp_ref[pl.ds(off + taps[0], rows), :]
        for j in taps:
            src = sw_ref[j - taps[0]:j - taps[0] + tm, :] if r else hp_ref[off + j:off + j + tm, :]
            term = src * wdw_ref[j:j + 1, :]
            acc = term if acc is None else acc + term
    o_ref[...] = _conv_tail(acc, bdw_ref, lng_ref, lnb_ref, wpw_ref, bpw_ref, g_ref[...]).astype(o_ref.dtype)
    nc_ref[0] = hp_ref[pl.ds(CONV_PAD + tm - (CONV_W - 1), CONV_W - 1), :]
    hp_ref[0:CONV_PAD, :] = hp_ref[tm:tm + CONV_PAD, :]


def _branch_b(z, wdw, bdw, lng, lnb, wpw, bpw, *, batch, tm):
    m = z.shape[0]
    nt = m // batch // tm
    blk = lambda c: pl.BlockSpec((tm, BRANCH_W), lambda b, t, c=c: (b * nt + t, c))
    vec = pl.BlockSpec((1, BRANCH_W), lambda b, t: (0, 0))
    return pl.pallas_call(
        functools.partial(_branch_b_kernel, tm=tm),
        grid=(batch, nt),
        in_specs=[blk(3), blk(4), blk(5),
                  pl.BlockSpec((CONV_PAD, BRANCH_W), lambda b, t: (0, 0)), vec, vec, vec,
                  pl.BlockSpec((BRANCH_W, BRANCH_W), lambda b, t: (0, 0)), vec],
        out_specs=[pl.BlockSpec((tm, BRANCH_W), lambda b, t: (b * nt + t, 0)),
                   pl.BlockSpec((1, CONV_W - 1, BRANCH_W), lambda b, t: (b, 0, 0))],
        out_shape=[jax.ShapeDtypeStruct((m, BRANCH_W), BF16),
                   jax.ShapeDtypeStruct((batch, CONV_W - 1, BRANCH_W), F32)],
        scratch_shapes=[pltpu.VMEM((tm + CONV_PAD, BRANCH_W), F32)] * 2,
        compiler_params=_cparams(2),
        name="branch_b",
    )(z, z, z, wdw, bdw, lng, lnb, wpw, bpw)


def _branch_b_sample_kernel(a_ref, b_ref, g_ref, st_ref, wdw_ref, bdw_ref, lng_ref, lnb_ref, wpw_ref,
                            bpw_ref, o_ref, h_ref, hp_ref, *, t_new):
    n_hist = CONV_W - 1
    h = a_ref[0] * jax.nn.sigmoid(b_ref[0])
    h_ref[0] = h
    hp_ref[CONV_PAD:CONV_PAD + 8, :] = jnp.zeros((8, BRANCH_W), F32)
    hp_ref[0:n_hist, :] = st_ref[0]
    hp_ref[n_hist:n_hist + t_new, :] = h
    acc = hp_ref[pl.ds(0, 8), :] * wdw_ref[0:1, :]
    for j in range(1, CONV_W):
        acc = acc + hp_ref[pl.ds(j, 8), :] * wdw_ref[j:j + 1, :]
    y = _silu(_ln(acc + bdw_ref[...], lng_ref[...], lnb_ref[...]))
    out = _dot(y.astype(BF16), wpw_ref[...]) + bpw_ref[...]
    o_ref[0] = out[0:t_new, :] * _silu(g_ref[0])


def _branch_b_sample(z3, state, layer, wdw, bdw, lng, lnb, wpw, bpw):
    nb, t_new, _ = z3.shape
    blk = lambda c: pl.BlockSpec((1, t_new, BRANCH_W), lambda b, c=c: (b, 0, c))
    vec = pl.BlockSpec((1, BRANCH_W), lambda b: (0, 0))
    row = pl.BlockSpec((1, t_new, BRANCH_W), lambda b: (b, 0, 0))
    return pl.pallas_call(
        functools.partial(_branch_b_sample_kernel, t_new=t_new),
        grid=(nb,),
        in_specs=[blk(3), blk(4), blk(5),
                  pl.BlockSpec((1, CONV_W - 1, BRANCH_W), lambda b: (layer * nb + b, 0, 0)),
                  pl.BlockSpec((CONV_PAD, BRANCH_W), lambda b: (0, 0)), vec, vec, vec,
                  pl.BlockSpec((BRANCH_W, BRANCH_W), lambda b: (0, 0)), vec],
        out_specs=[row, row],
        out_shape=[jax.ShapeDtypeStruct((nb, t_new, BRANCH_W), F32)] * 2,
        scratch_shapes=[pltpu.VMEM((CONV_PAD + 8, BRANCH_W), F32)],
        compiler_params=_cparams(1),
        name="branch_b_sample",
    )(z3, z3, z3, state, wdw, bdw, lng, lnb, wpw, bpw)


FOX_STRIPS = 1


def _fox_kernel(q_ref, k_ref, v_ref, c_ref, g_ref, o_ref, kb_ref, vt_ref, m_ref, l_ref, acc_ref, *, tq, nk):
    qi = pl.program_id(2)

    @pl.when(qi == 0)
    def _():
        for j in range(nk):
            rs = slice(j * tq, (j + 1) * tq)
            kb_ref[j] = k_ref[rs, :].astype(BF16)
            vt_ref[j] = v_ref[rs, :].T.astype(BF16)

    m_ref[...] = jnp.full_like(m_ref, NEG)
    l_ref[...] = jnp.zeros_like(l_ref)
    acc_ref[...] = jnp.zeros_like(acc_ref)
    ws = tq // FOX_STRIPS
    qs = [(q_ref[s * ws:(s + 1) * ws, :] * (ATT_SCALE * LOG2E)).astype(BF16) for s in range(FOX_STRIPS)]

    def block(kj, diagonal):
        c = c_ref[0, pl.ds(pl.multiple_of(kj * tq, tq), tq), :]
        cw = jnp.concatenate([c] * (ws // LANE), axis=1)
        k = kb_ref[kj]
        vt = vt_ref[kj]
        for s in range(FOX_STRIPS):
            t = _dot_nt(k, qs[s]) - cw
            if diagonal:
                key = lax.broadcasted_iota(jnp.int32, (tq, ws), 0)
                qry = lax.broadcasted_iota(jnp.int32, (tq, ws), 1) + s * ws
                t = jnp.where(key <= qry, t, NEG)
            m_prev = m_ref[s]
            m_new = jnp.maximum(m_prev, jnp.max(t, axis=0, keepdims=True))
            alpha = jnp.exp2(m_prev - m_new)
            p = jnp.exp2(t - m_new)
            l_ref[s] = alpha * l_ref[s] + jnp.sum(p, axis=0, keepdims=True)
            acc_ref[s] = alpha * acc_ref[s] + _dot(vt, p.astype(BF16))
            m_ref[s] = m_new

    def body(kj, carry):
        block(kj, False)
        return carry

    lax.fori_loop(0, qi, body, 0)
    block(qi, True)
    for s in range(FOX_STRIPS):
        rs = slice(s * ws, (s + 1) * ws)
        o_ref[rs, :] = ((acc_ref[s] / l_ref[s]).T * _silu(g_ref[rs, :])).astype(o_ref.dtype)


def _fox_prompt(z, c_rep, *, batch, tq):
    m = z.shape[0]
    seq = m // batch
    nq = seq // tq
    ws = tq // FOX_STRIPS
    qmap = lambda c: (lambda b, h, qi: (b * nq + qi, c + h))
    kvmap = lambda c: (lambda b, h, qi: (b, c + h))
    return pl.pallas_call(
        functools.partial(_fox_kernel, tq=tq, nk=nq),
        grid=(batch, H_C, nq),
        in_specs=[pl.BlockSpec((tq, HD), qmap(QC)),
                  pl.BlockSpec((seq, HD), kvmap(KC)),
                  pl.BlockSpec((seq, HD), kvmap(VC)),
                  pl.BlockSpec((1, seq, LANE), lambda b, h, qi: (h, b, 0)),
                  pl.BlockSpec((tq, HD), qmap(GC))],
        out_specs=pl.BlockSpec((tq, HD), lambda b, h, qi: (b * nq + qi, h)),
        out_shape=jax.ShapeDtypeStruct((m, BRANCH_W), BF16),
        scratch_shapes=[pltpu.VMEM((nq, tq, HD), BF16), pltpu.VMEM((nq, HD, tq), BF16),
                        pltpu.VMEM((FOX_STRIPS, 1, ws), F32), pltpu.VMEM((FOX_STRIPS, 1, ws), F32),
                        pltpu.VMEM((FOX_STRIPS, HD, ws), F32)],
        compiler_params=_cparams(3),
        name="fox_prompt",
    )(z, z, z, c_rep, z)


def _mem_attn_kernel(q_ref, k_ref, v_ref, g_ref, o_ref):
    s = _dot_nt(q_ref[...].astype(BF16), k_ref[...].astype(BF16)) * ATT_SCALE
    p = jnp.exp(s - jnp.max(s, axis=1, keepdims=True))
    o = _dot(p.astype(BF16), v_ref[...].astype(BF16)) / jnp.sum(p, axis=1, keepdims=True)
    o_ref[...] = (o * _silu(g_ref[...])).astype(o_ref.dtype)


def _mem_attn_prompt(z, mkv, *, batch, tq):
    m = z.shape[0]
    nq = m // batch // tq
    return pl.pallas_call(
        _mem_attn_kernel,
        grid=(batch, H_C, nq),
        in_specs=[pl.BlockSpec((tq, HD), lambda b, h, qi: (b * nq + qi, QM + h)),
                  pl.BlockSpec((N_MEM, HD), lambda b, h, qi: (b, h)),
                  pl.BlockSpec((N_MEM, HD), lambda b, h, qi: (b, H_C + h)),
                  pl.BlockSpec((tq, HD), lambda b, h, qi: (b * nq + qi, GM + h))],
        out_specs=pl.BlockSpec((tq, HD), lambda b, h, qi: (b * nq + qi, h)),
        out_shape=jax.ShapeDtypeStruct((m, BRANCH_W), BF16),
        compiler_params=_cparams(3),
        name="mem_attn_prompt",
    )(z, mkv, mkv, z)


def _head_match(rows, cols):
    r = lax.broadcasted_iota(jnp.int32, (rows, cols), 0)
    c = lax.broadcasted_iota(jnp.int32, (rows, cols), 1)
    return r, c, (r & (H_C - 1)) == (c & (H_C - 1))


def _mem_attn_sample_kernel(q_ref, g_ref, k_ref, v_ref, o_ref):
    s = _dot_nt(q_ref[0].astype(BF16), k_ref[0].astype(BF16)) * ATT_SCALE
    _, _, same = _head_match(s.shape[0], s.shape[1])
    s = jnp.where(same, s, NEG)
    p = jnp.exp(s - jnp.max(s, axis=1, keepdims=True))
    o = _dot(p.astype(BF16), v_ref[0].astype(BF16)) / jnp.sum(p, axis=1, keepdims=True)
    o_ref[0] = o * _silu(g_ref[0])


def _mem_attn_sample(q16, g16, mk, mv, layer):
    nb, nr, _ = q16.shape
    nm = mk.shape[1]
    row = pl.BlockSpec((1, nr, HD), lambda b: (b, 0, 0))
    mem = pl.BlockSpec((1, nm, HD), lambda b: (layer * nb + b, 0, 0))
    return pl.pallas_call(
        _mem_attn_sample_kernel,
        grid=(nb,),
        in_specs=[row, row, mem, mem],
        out_specs=row,
        out_shape=jax.ShapeDtypeStruct((nb, nr, HD), F32),
        compiler_params=_cparams(1),
        name="mem_attn_sample",
    )(q16, g16, mk, mv)


def _logf_pages_kernel(x_ref, mc_ref, mt_ref, o_ref):
    x = x_ref[...]
    o_ref[:, 0:PAGE_ROWS] = _dot_exact01(x, mc_ref[...])
    o_ref[:, PAGE_ROWS:2 * PAGE_ROWS] = _dot_exact01(x, mt_ref[...])


def _logf_pages(lf_flat, mc, mt, *, tm):
    n = lf_flat.shape[0]
    mat = pl.BlockSpec((PAGE_ROWS, PAGE_ROWS), lambda i: (0, 0))
    return pl.pallas_call(
        _logf_pages_kernel,
        grid=(n // tm,),
        in_specs=[pl.BlockSpec((tm, PAGE_ROWS), lambda i: (i, 0)), mat, mat],
        out_specs=pl.BlockSpec((tm, 2 * PAGE_ROWS), lambda i: (i, 0)),
        out_shape=jax.ShapeDtypeStruct((n, 2 * PAGE_ROWS), F32),
        compiler_params=_cparams(1),
        name="logf_pages",
    )(lf_flat, mc, mt)


def _fox_sample_kernel(pt_ref, q_ref, g_ref, kn_ref, vn_ref, lfn_ref, mn_ref, *rest, n_pages, n_new):
    del pt_ref
    np_ = PAGES_PER_STEP
    k_refs, v_refs, wt_refs = rest[:np_], rest[np_:2 * np_], rest[2 * np_:3 * np_]
    o_ref, m_ref, l_ref, acc_ref, carry_ref = rest[3 * np_:]
    step = pl.program_id(1)
    nr = q_ref.shape[1]

    @pl.when(step == 0)
    def _():
        m_ref[...] = jnp.full_like(m_ref, NEG)
        l_ref[...] = jnp.zeros_like(l_ref)
        acc_ref[...] = jnp.zeros_like(acc_ref)
        carry_ref[...] = jnp.zeros_like(carry_ref)

    q = q_ref[0].astype(BF16)

    def update(s_list, v_list):
        m_prev = m_ref[...]
        m_new = m_prev
        for s in s_list:
            m_new = jnp.maximum(m_new, jnp.max(s, axis=1, keepdims=True))
        alpha = jnp.exp(m_prev - m_new)
        l_new = alpha * l_ref[...]
        acc = alpha * acc_ref[...]
        for s, v in zip(s_list, v_list):
            p = jnp.exp(s - m_new)
            l_new = l_new + jnp.sum(p, axis=1, keepdims=True)
            acc = acc + _dot(p.astype(BF16), v)
        m_ref[...] = m_new
        l_ref[...] = l_new
        acc_ref[...] = acc

    _, _, same = _head_match(nr, PAGE_ROWS)
    carry = carry_ref[...]
    s_list, v_list = [], []
    for i in range(np_):
        wt = wt_refs[i][0]
        ck = carry + wt[:, 0:PAGE_ROWS]
        carry = carry + wt[:, PAGE_ROWS:2 * PAGE_ROWS]
        s = _dot_nt(q, k_refs[i][0].astype(BF16)) * ATT_SCALE - ck
        s_list.append(jnp.where(same, s, NEG))
        v_list.append(v_refs[i][0].astype(BF16))
    carry_ref[...] = carry
    update(s_list, v_list)

    @pl.when(step == n_pages // np_ - 1)
    def _():
        r, c, same_n = _head_match(nr, LANE)
        cn = carry[:, 0:LANE] + _dot_exact01(lfn_ref[0], mn_ref[...])[0:1, :]
        s = _dot_nt(q, kn_ref[0].astype(BF16)) * ATT_SCALE - cn
        ok = same_n & (c < n_new * H_C) & ((c >> 2) <= (r >> 2))
        update([jnp.where(ok, s, NEG)], [vn_ref[0].astype(BF16)])
        o_ref[0] = acc_ref[...] / l_ref[...] * _silu(g_ref[0])


def _fox_sample(pt, q16, g16, kn, vn, lfn, mn, kflat, vflat, wt3, *, n_new):
    nb, nr, _ = q16.shape
    n_pages = pt.shape[1]
    np_ = PAGES_PER_STEP
    row = pl.BlockSpec((1, nr, HD), lambda b, s, pt: (b, 0, 0))
    new = pl.BlockSpec((1, LANE, HD), lambda b, s, pt: (b, 0, 0))
    page = lambda i: (lambda b, s, pt: (pt[b, s * np_ + i], 0, 0))
    in_specs = [row, row, new, new,
                pl.BlockSpec((1, 8, LANE), lambda b, s, pt: (b, 0, 0)),
                pl.BlockSpec((LANE, LANE), lambda b, s, pt: (0, 0))]
    in_specs += [pl.BlockSpec((1, PAGE_ROWS, HD), page(i)) for i in range(np_)]
    in_specs += [pl.BlockSpec((1, PAGE_ROWS, HD), page(i)) for i in range(np_)]
    in_specs += [pl.BlockSpec((1, 1, 2 * PAGE_ROWS), page(i)) for i in range(np_)]
    grid_spec = pltpu.PrefetchScalarGridSpec(
        num_scalar_prefetch=1,
        grid=(nb, n_pages // np_),
        in_specs=in_specs,
        out_specs=pl.BlockSpec((1, nr, HD), lambda b, s, pt: (b, 0, 0)),
        scratch_shapes=[pltpu.VMEM((nr, 1), F32), pltpu.VMEM((nr, 1), F32), pltpu.VMEM((nr, HD), F32),
                        pltpu.VMEM((1, PAGE_ROWS), F32)],
    )
    return pl.pallas_call(
        functools.partial(_fox_sample_kernel, n_pages=n_pages, n_new=n_new),
        grid_spec=grid_spec,
        out_shape=jax.ShapeDtypeStruct((nb, nr, HD), F32),
        compiler_params=_cparams(2),
        name="fox_sample",
    )(pt, q16, g16, kn, vn, lfn, mn, *([kflat] * np_), *([vflat] * np_), *([wt3] * np_))


def _merge_kernel(*refs):
    o_refs, wb_ref, gate_refs, h_ref = refs[0:4], refs[4], refs[5:9], refs[9]
    acc = None
    for br in range(N_BRANCH):
        gate = 0.5 * jnp.tanh(0.5 * gate_refs[br][...]) + 0.5
        term = gate * _dot(o_refs[br][...].astype(BF16), wb_ref[br])
        acc = term if acc is None else acc + term
    h_ref[...] = acc.astype(h_ref.dtype)


def _merge(outs, z, wb, layer, *, tm):
    m = z.shape[0]
    tn = BRANCH_W
    nj = D_MODEL // tn
    o_spec = pl.BlockSpec((tm, BRANCH_W), lambda i, j: (i, 0))
    gate = lambda br: pl.BlockSpec((tm, tn), lambda i, j, br=br: (i, GATE_BLK + br * nj + j))
    return pl.pallas_call(
        _merge_kernel,
        grid=(m // tm, nj),
        in_specs=[o_spec] * 4 + [pl.BlockSpec((None, N_BRANCH, BRANCH_W, tn), lambda i, j: (layer, 0, 0, j))]
                 + [gate(br) for br in range(N_BRANCH)],
        out_specs=pl.BlockSpec((tm, tn), lambda i, j: (i, j)),
        out_shape=jax.ShapeDtypeStruct((m, D_MODEL), BF16),
        compiler_params=_cparams(2),
        name="merge",
    )(*outs, wb, z, z, z, z)


def _out_ln_kernel(h_ref, w_ref, x_ref, g_ref, b_ref, y_ref, *, tm):
    half = max(tm // 2, SUBLANES)
    for s in range(tm // half):
        rs = slice(s * half, (s + 1) * half)
        y = ALPHA * x_ref[rs, :] + _dot(h_ref[rs, :], w_ref[...])
        y_ref[rs, :] = _ln(y, g_ref[...], b_ref[...])


def _out_ln(hm, wo, x, lng, lnb, layer, *, tm):
    m = x.shape[0]
    rows = pl.BlockSpec((tm, D_MODEL), lambda i: (i, 0))
    vec = pl.BlockSpec((1, D_MODEL), lambda i: (0, 0))
    return pl.pallas_call(
        functools.partial(_out_ln_kernel, tm=tm),
        grid=(m // tm,),
        in_specs=[rows, pl.BlockSpec((None, D_MODEL, D_MODEL), lambda i: (layer, 0, 0)), rows, vec, vec],
        out_specs=rows,
        out_shape=jax.ShapeDtypeStruct((m, D_MODEL), F32),
        compiler_params=_cparams(1),
        name="out_ln",
    )(hm, wo, x, lng, lnb)


def _tok_head_matrices():
    i = jnp.arange(PAGE_ROWS)
    same = (i[:, None] % H_C) == (i[None, :] % H_C)
    mc = (same & (i[:, None] // H_C <= i[None, :] // H_C)).astype(BF16)
    mt = same.astype(BF16)
    return mc, mt


def kernel(x_prompt, x_sample, mem_prompt, cache_k, cache_v, cache_logf, cache_mem_k, cache_mem_v, state_conv,
           page_table, w_in, w_mem_k, w_mem_v, ln_v_g, ln_v_b, w_s, b_s, w_dw, b_dw, ln_c_g, ln_c_b, w_pw, b_pw,
           b_f, w_branch, w_out, ln_g, ln_b):
    bp, seq, _ = x_prompt.shape
    db, t_new, _ = x_sample.shape
    n_pool = cache_k.shape[1]

    w_main, wf = _prep_w(w_in)
    bfb = jnp.pad(b_f, ((0, 0), (0, LANE - H_C)))[:, None, :]
    wb = w_branch.astype(BF16)
    wo = w_out.astype(BF16)
    wpw = w_pw.astype(BF16)
    wmkv = jnp.concatenate([w_mem_k, w_mem_v], axis=2).astype(BF16)
    wdw = jnp.pad(w_dw, ((0, 0), (0, CONV_PAD - CONV_W), (0, 0)))
    vec = lambda a: a[:, None, :]
    ln_v_g, ln_v_b, b_dw, ln_c_g, ln_c_b, b_pw, ln_g, ln_b = map(
        vec, (ln_v_g, ln_v_b, b_dw, ln_c_g, ln_c_b, b_pw, ln_g, ln_b))

    idx = jnp.arange(CHUNK)
    mask_p = (idx[None, :] <= idx[:, None]).astype(F32)
    bsb_p = jnp.broadcast_to(b_s[:, :, :, None], (DEPTH, A_GROUPS, CHUNK, CHUNK))
    reps = CHUNK // t_new
    mask_s = ((idx[:, None] // t_new == idx[None, :] // t_new) & (idx[None, :] <= idx[:, None])).astype(F32)
    ws_s = jnp.tile(w_s[:, :, :t_new, :t_new], (1, 1, reps, reps))
    bsb_s = jnp.broadcast_to(jnp.tile(b_s[:, :, :t_new], (1, 1, reps))[:, :, :, None],
                             (DEPTH, A_GROUPS, CHUNK, CHUNK))

    kflat = cache_k.reshape(DEPTH * n_pool, PAGE_ROWS, HD)
    vflat = cache_v.reshape(DEPTH * n_pool, PAGE_ROWS, HD)
    mc, mt = _tok_head_matrices()
    wt3 = _logf_pages(cache_logf.reshape(DEPTH * n_pool, PAGE_ROWS), mc, mt, tm=512)
    wt3 = wt3.reshape(DEPTH * n_pool, 1, 2 * PAGE_ROWS)
    mn = jnp.pad(mc[:t_new * H_C, :t_new * H_C], ((0, LANE - t_new * H_C),) * 2)
    memk = cache_mem_k.reshape(DEPTH * db, N_MEM * H_C, HD)
    memv = cache_mem_v.reshape(DEPTH * db, N_MEM * H_C, HD)
    state = state_conv.reshape(DEPTH * db, CONV_W - 1, BRANCH_W)

    xp = x_prompt.reshape(bp * seq, D_MODEL)
    xs = x_sample.reshape(db * t_new, D_MODEL)
    mem2d = mem_prompt.reshape(bp * N_MEM, D_MODEL)
    m_s = db * t_new
    outs = [[] for _ in range(11)]
    for l in range(DEPTH):
        z, lf, k_p, v_p, c_rep = _in_proj(xp, w_main, wf, bfb[l], l, tm=1024, tiles_per_seq=seq // 1024,
                                          emit_c=True)
        mkv = _mm(mem2d, wmkv[l], tm=bp * N_MEM, tn=2 * BRANCH_W)
        (oa,) = _branch_a(z, ln_v_g[l], ln_v_b[l], w_s[l], bsb_p[l], mask_p, tm=512, emit_v=False)
        ob, nconv = _branch_b(z, wdw[l], b_dw[l], ln_c_g[l], ln_c_b[l], wpw[l], b_pw[l], batch=bp, tm=512)
        oc = _fox_prompt(z, c_rep, batch=bp, tq=512)
        om = _mem_attn_prompt(z, mkv, batch=bp, tq=512)
        hm = _merge((oa, ob, oc, om), z, wb, l, tm=512)
        xp = _out_ln(hm, wo, xp, ln_g[l], ln_b[l], l, tm=256)
        outs[0].append(k_p.reshape(bp, seq, H_C, HD))
        outs[1].append(v_p.reshape(bp, seq, H_C, HD))
        outs[2].append(lf[:, :H_C].reshape(bp, seq, H_C))
        outs[3].append(nconv)
        outs[4].append(mkv[:, :BRANCH_W].reshape(bp, N_MEM, H_C, HD))
        outs[5].append(mkv[:, BRANCH_W:].reshape(bp, N_MEM, H_C, HD))

        zs, lfs, k_s, v_s = _in_proj(xs, w_main, wf, bfb[l], l, tm=m_s, tiles_per_seq=1, emit_c=False)
        oa_s, v_rows = _branch_a(zs, ln_v_g[l], ln_v_b[l], ws_s[l], bsb_s[l], mask_s, tm=m_s, emit_v=True)
        ob_s, h_glu = _branch_b_sample(zs.reshape(db, t_new, Z_COLS), state, l, wdw[l], b_dw[l], ln_c_g[l],
                                       ln_c_b[l], wpw[l], b_pw[l])
        heads = lambda c: zs[:, c * LANE:c * LANE + BRANCH_W].reshape(db, t_new * H_C, HD)
        pad_new = lambda a: jnp.pad(a.reshape(db, t_new * H_C, HD), ((0, 0), (0, LANE - t_new * H_C), (0, 0)))
        lfn = lfs[:, :H_C].reshape(db, 1, t_new * H_C)
        lfn = jnp.pad(lfn, ((0, 0), (0, 7), (0, LANE - t_new * H_C)))
        oc_s = _fox_sample(page_table + l * n_pool, heads(QC), heads(GC), pad_new(k_s), pad_new(v_s),
                           lfn, mn, kflat, vflat, wt3, n_new=t_new)
        om_s = _mem_attn_sample(heads(QM), heads(GM), memk, memv, l)
        flat = lambda a: a.reshape(m_s, BRANCH_W)
        hm_s = _merge((oa_s, flat(ob_s), flat(oc_s), flat(om_s)), zs, wb, l, tm=m_s)
        xs = _out_ln(hm_s, wo, xs, ln_g[l], ln_b[l], l, tm=m_s)
        outs[6].append(k_s.reshape(db, t_new, H_C, HD))
        outs[7].append(v_s.reshape(db, t_new, H_C, HD))
        outs[8].append(lfs[:, :H_C].reshape(db, t_new, H_C))
        outs[9].append(jnp.concatenate([state_conv[l][:, t_new:], h_glu], axis=1))
        outs[10].append(v_rows.reshape(db, t_new, BRANCH_W))

    return (xp.reshape(bp, seq, D_MODEL), xs.reshape(db, t_new, D_MODEL)) + tuple(jnp.stack(o) for o in outs)
```

```python
import functools
import math

import jax
import jax.numpy as jnp
from jax import lax
from jax.experimental import pallas as pl
from jax.experimental.pallas import tpu as pltpu

F32 = jnp.float32
BF16 = jnp.bfloat16

D_MODEL = 2048
DEPTH = 2
BRANCH_W = 512
N_BRANCH = 4
CHUNK = 128
A_GROUPS = 4
CONV_W = 31
H_C = 4
HD = 128
N_MEM = 256
PAGE_SIZE = 128
LN_EPS = 1e-5
ALPHA = (2 * DEPTH) ** 0.25
ATT_SCALE = HD ** -0.5
LOG2E = math.log2(math.e)
NEG = -1e30

LANE = 128
SUBLANES = 8
PAGE_ROWS = PAGE_SIZE * H_C
F_COL = 9 * BRANCH_W
Z_COLS = 28 * BRANCH_W
QC, KC, VC, GC, QM, GM = 24, 28, 32, 36, 40, 44
GATE_BLK = 12
PAGES_PER_STEP = 16
VMEM_LIMIT = 48 * 1024 * 1024


def _cparams(n_axes, vmem=VMEM_LIMIT):
    return pltpu.CompilerParams(dimension_semantics=("arbitrary",) * n_axes, vmem_limit_bytes=vmem)


def _ln(x, g, b):
    mu = jnp.mean(x, axis=-1, keepdims=True)
    xc = x - mu
    var = jnp.mean(xc * xc, axis=-1, keepdims=True)
    return xc * lax.rsqrt(var + LN_EPS) * g + b


def _silu(x):
    return x * jax.nn.sigmoid(x)


def _log_sigmoid(x):
    return jnp.minimum(x, 0.0) - jnp.log1p(jnp.exp(-jnp.abs(x)))


def _dot(a, b):
    return jnp.dot(a, b, preferred_element_type=F32)


def _dot_nt(a, b):
    return lax.dot_general(a, b, (((1,), (1,)), ((), ())), preferred_element_type=F32)


def _split3(x):
    hi = x.astype(BF16)
    r = x - hi.astype(F32)
    mid = r.astype(BF16)
    lo = (r - mid.astype(F32)).astype(BF16)
    return hi, mid, lo


def _dot_exact01(x, m01):
    hi, mid, lo = _split3(x)
    return _dot(hi, m01) + _dot(mid, m01) + _dot(lo, m01)


def _dot_exact01_left(m01, x):
    hi, mid, lo = _split3(x)
    return _dot(m01, hi) + _dot(m01, mid) + _dot(m01, lo)


def _prep_w_kernel(a_ref, f_ref, o_ref, wf_ref):
    j = pl.program_id(0)
    for l in range(DEPTH):
        o_ref[l] = a_ref[:, l, :].T.astype(BF16)

    @pl.when(j == 0)
    def _():
        lane = lax.broadcasted_iota(jnp.int32, (D_MODEL, LANE), 1)
        for l in range(DEPTH):
            wf_ref[l] = jnp.where(lane < H_C, f_ref[:, l, :].T, 0.0).astype(BF16)


def _prep_w(w_in):
    wt = jnp.transpose(w_in, (2, 0, 1))
    elems = lambda rows: (pl.Element(rows), pl.Element(DEPTH), pl.Element(D_MODEL))
    tc = BRANCH_W
    return pl.pallas_call(
        _prep_w_kernel,
        grid=(Z_COLS // tc,),
        in_specs=[pl.BlockSpec(elems(tc), lambda j: (j * tc + jnp.where(j >= F_COL // tc, H_C, 0), 0, 0)),
                  pl.BlockSpec(elems(LANE), lambda j: (F_COL, 0, 0))],
        out_specs=[pl.BlockSpec((DEPTH, D_MODEL, tc), lambda j: (0, 0, j)),
                   pl.BlockSpec((DEPTH, D_MODEL, LANE), lambda j: (0, 0, 0))],
        out_shape=[jax.ShapeDtypeStruct((DEPTH, D_MODEL, Z_COLS), BF16),
                   jax.ShapeDtypeStruct((DEPTH, D_MODEL, LANE), BF16)],
        compiler_params=_cparams(1),
        name="prep_w",
    )(wt, wt)


K_TILE = KC * LANE // BRANCH_W
V_TILE = VC * LANE // BRANCH_W


def _in_proj_kernel(x_ref, w_ref, wf_ref, bf_ref, z_ref, lf_ref, k_ref, v_ref, *rest, tm, tiles_per_seq, emit_c):
    if emit_c:
        c_ref, xb_ref, carry_ref = rest
    else:
        (xb_ref,) = rest
    i = pl.program_id(0)
    j = pl.program_id(1)

    @pl.when(j == 0)
    def _():
        xb = x_ref[...].astype(BF16)
        xb_ref[...] = xb
        lf = _log_sigmoid(_dot(xb, wf_ref[...]) + bf_ref[...])
        lf_ref[...] = lf
        if emit_c:
            @pl.when(lax.rem(i, tiles_per_seq) == 0)
            def _():
                carry_ref[...] = jnp.zeros_like(carry_ref)

            row = lax.broadcasted_iota(jnp.int32, (LANE, LANE), 0)
            col = lax.broadcasted_iota(jnp.int32, (LANE, LANE), 1)
            lower = jnp.where(col <= row, 1.0, 0.0).astype(BF16)
            carry = carry_ref[...]
            for r in range(tm // LANE):
                rs = slice(r * LANE, (r + 1) * LANE)
                cblk = _dot_exact01_left(lower, lf[rs, :]) + carry
                carry = cblk[LANE - 1:LANE, :]
                c2 = cblk * LOG2E
                for h in range(H_C):
                    c_ref[h, rs, :] = jnp.broadcast_to(c2[:, h:h + 1], (LANE, LANE))
            carry_ref[...] = carry

    acc = _dot(xb_ref[...], w_ref[...])
    z_ref[...] = acc.astype(z_ref.dtype)

    def heads_out(o_ref):
        for h in range(H_C):
            o_ref[:, h, :] = acc[:, h * HD:(h + 1) * HD]

    @pl.when(j == K_TILE)
    def _():
        heads_out(k_ref)

    @pl.when(j == V_TILE)
    def _():
        heads_out(v_ref)


def _in_proj(x, w_main, wf, bfb, layer, *, tm, tiles_per_seq, emit_c, z_dtype):
    m = x.shape[0]
    tn = BRANCH_W
    kern = functools.partial(_in_proj_kernel, tm=tm, tiles_per_seq=tiles_per_seq, emit_c=emit_c)
    heads = pl.BlockSpec((tm, H_C, HD), lambda i, j: (i, 0, 0))
    out_specs = [pl.BlockSpec((tm, tn), lambda i, j: (i, j)),
                 pl.BlockSpec((tm, LANE), lambda i, j: (i, 0)), heads, heads]
    out_shape = [jax.ShapeDtypeStruct((m, Z_COLS), z_dtype), jax.ShapeDtypeStruct((m, LANE), F32),
                 jax.ShapeDtypeStruct((m, H_C, HD), F32), jax.ShapeDtypeStruct((m, H_C, HD), F32)]
    scratch = [pltpu.VMEM((tm, D_MODEL), BF16)]
    if emit_c:
        out_specs.append(pl.BlockSpec((H_C, tm, LANE), lambda i, j: (0, i, 0)))
        out_shape.append(jax.ShapeDtypeStruct((H_C, m, LANE), F32))
        scratch.append(pltpu.VMEM((1, LANE), F32))
    return pl.pallas_call(
        kern,
        grid=(m // tm, Z_COLS // tn),
        in_specs=[
            pl.BlockSpec((tm, D_MODEL), lambda i, j: (i, 0)),
            pl.BlockSpec((None, D_MODEL, tn), lambda i, j: (layer, 0, j)),
            pl.BlockSpec((None, D_MODEL, LANE), lambda i, j: (layer, 0, 0)),
            pl.BlockSpec((1, LANE), lambda i, j: (0, 0)),
        ],
        out_specs=out_specs,
        out_shape=out_shape,
        scratch_shapes=scratch,
        compiler_params=_cparams(2),
        name="in_proj",
    )(x, w_main, wf, bfb)


def _mm_kernel(x_ref, w_ref, o_ref):
    o_ref[...] = _dot(x_ref[...].astype(BF16), w_ref[...])


def _mm(x, w, *, tm, tn):
    m, k = x.shape
    n = w.shape[1]
    return pl.pallas_call(
        _mm_kernel,
        grid=(m // tm, n // tn),
        in_specs=[pl.BlockSpec((tm, k), lambda i, j: (i, 0)), pl.BlockSpec((k, tn), lambda i, j: (0, j))],
        out_specs=pl.BlockSpec((tm, tn), lambda i, j: (i, j)),
        out_shape=jax.ShapeDtypeStruct((m, n), F32),
        compiler_params=_cparams(2),
        name="mem_kv_proj",
    )(x, w)


def _branch_a_kernel(u_ref, v_ref, g_ref, lng_ref, lnb_ref, ws_ref, bsb_ref, mask_ref, o_ref, *rest,
                     tm, emit_v):
    u = jax.nn.gelu(u_ref[...].astype(F32))
    v = _ln(jax.nn.gelu(v_ref[...].astype(F32)), lng_ref[...], lnb_ref[...])
    if emit_v:
        rest[0][...] = v
    gate = _silu(g_ref[...].astype(F32))
    keep = mask_ref[...] > 0.0
    for g in range(A_GROUPS):
        wg = jnp.where(keep, ws_ref[g], 0.0).astype(BF16)
        cs = slice(g * LANE, (g + 1) * LANE)
        for c in range(tm // CHUNK):
            rs = slice(c * CHUNK, (c + 1) * CHUNK)
            s = _dot(wg, v[rs, cs].astype(BF16)) + bsb_ref[g]
            o_ref[rs, cs] = (u[rs, cs] * s * gate[rs, cs]).astype(o_ref.dtype)


def _branch_a(z, lng, lnb, ws, bsb, mask, *, tm, emit_v):
    m = z.shape[0]
    blk = lambda c: pl.BlockSpec((tm, BRANCH_W), lambda i, c=c: (i, c))
    vec = pl.BlockSpec((1, BRANCH_W), lambda i: (0, 0))
    cube = pl.BlockSpec((A_GROUPS, CHUNK, CHUNK), lambda i: (0, 0, 0))
    out_specs = [pl.BlockSpec((tm, BRANCH_W), lambda i: (i, 0))]
    out_shape = [jax.ShapeDtypeStruct((m, BRANCH_W), BF16)]
    if emit_v:
        out_specs.append(pl.BlockSpec((tm, BRANCH_W), lambda i: (i, 0)))
        out_shape.append(jax.ShapeDtypeStruct((m, BRANCH_W), F32))
    return pl.pallas_call(
        functools.partial(_branch_a_kernel, tm=tm, emit_v=emit_v),
        grid=(m // tm,),
        in_specs=[blk(0), blk(1), blk(2), vec, vec, cube, cube,
                  pl.BlockSpec((CHUNK, CHUNK), lambda i: (0, 0))],
        out_specs=out_specs,
        out_shape=out_shape,
        compiler_params=_cparams(1),
        name="branch_a",
    )(z, z, z, lng, lnb, ws, bsb, mask)


CONV_PAD = 32


def _conv_tail(y, bdw_ref, lng_ref, lnb_ref, wpw_ref, bpw_ref, gate):
    y = _silu(_ln(y + bdw_ref[...], lng_ref[...], lnb_ref[...]))
    return (_dot(y.astype(BF16), wpw_ref[...]) + bpw_ref[...]) * _silu(gate)


def _branch_b_kernel(a_ref, b_ref, g_ref, wdw_ref, bdw_ref, lng_ref, lnb_ref, wpw_ref, bpw_ref,
                     o_ref, nc_ref, hp_ref, sw_ref, *, tm):
    t = pl.program_id(1)

    @pl.when(t == 0)
    def _():
        hp_ref[0:CONV_PAD, :] = jnp.zeros((CONV_PAD, BRANCH_W), F32)

    hp_ref[CONV_PAD:CONV_PAD + tm, :] = a_ref[...].astype(F32) * jax.nn.sigmoid(b_ref[...].astype(F32))
    off = CONV_PAD - (CONV_W - 1)
    acc = None
    for r in range(SUBLANES):
        taps = [j for j in range(CONV_W) if (off + j) % SUBLANES == r]
        rows = tm + taps[-1] - taps[0]
        if r:
            sw_ref[0:rows, :] = hp_ref[pl.ds(off + taps[0], rows), :]
        for j in taps:
            src = sw_ref[j - taps[0]:j - taps[0] + tm, :] if r else hp_ref[off + j:off + j + tm, :]
            term = src * wdw_ref[j:j + 1, :]
            acc = term if acc is None else acc + term
    gate = g_ref[...].astype(F32)
    o_ref[...] = _conv_tail(acc, bdw_ref, lng_ref, lnb_ref, wpw_ref, bpw_ref, gate).astype(o_ref.dtype)
    nc_ref[0] = hp_ref[pl.ds(CONV_PAD + tm - (CONV_W - 1), CONV_W - 1), :]
    hp_ref[0:CONV_PAD, :] = hp_ref[tm:tm + CONV_PAD, :]


def _branch_b(z, wdw, bdw, lng, lnb, wpw, bpw, *, batch, tm):
    m = z.shape[0]
    nt = m // batch // tm
    blk = lambda c: pl.BlockSpec((tm, BRANCH_W), lambda b, t, c=c: (b * nt + t, c))
    vec = pl.BlockSpec((1, BRANCH_W), lambda b, t: (0, 0))
    return pl.pallas_call(
        functools.partial(_branch_b_kernel, tm=tm),
        grid=(batch, nt),
        in_specs=[blk(3), blk(4), blk(5),
                  pl.BlockSpec((CONV_PAD, BRANCH_W), lambda b, t: (0, 0)), vec, vec, vec,
                  pl.BlockSpec((BRANCH_W, BRANCH_W), lambda b, t: (0, 0)), vec],
        out_specs=[pl.BlockSpec((tm, BRANCH_W), lambda b, t: (b * nt + t, 0)),
                   pl.BlockSpec((1, CONV_W - 1, BRANCH_W), lambda b, t: (b, 0, 0))],
        out_shape=[jax.ShapeDtypeStruct((m, BRANCH_W), BF16),
                   jax.ShapeDtypeStruct((batch, CONV_W - 1, BRANCH_W), F32)],
        scratch_shapes=[pltpu.VMEM((tm + CONV_PAD, BRANCH_W), F32)] * 2,
        compiler_params=_cparams(2),
        name="branch_b",
    )(z, z, z, wdw, bdw, lng, lnb, wpw, bpw)


def _branch_b_sample_kernel(a_ref, b_ref, g_ref, st_ref, wdw_ref, bdw_ref, lng_ref, lnb_ref, wpw_ref,
                            bpw_ref, o_ref, h_ref, hp_ref, *, t_new):
    n_hist = CONV_W - 1
    h = a_ref[0] * jax.nn.sigmoid(b_ref[0])
    h_ref[0] = h
    hp_ref[CONV_PAD:CONV_PAD + 8, :] = jnp.zeros((8, BRANCH_W), F32)
    hp_ref[0:n_hist, :] = st_ref[0]
    hp_ref[n_hist:n_hist + t_new, :] = h
    acc = hp_ref[pl.ds(0, 8), :] * wdw_ref[0:1, :]
    for j in range(1, CONV_W):
        acc = acc + hp_ref[pl.ds(j, 8), :] * wdw_ref[j:j + 1, :]
    y = _silu(_ln(acc + bdw_ref[...], lng_ref[...], lnb_ref[...]))
    out = _dot(y.astype(BF16), wpw_ref[...]) + bpw_ref[...]
    o_ref[0] = out[0:t_new, :] * _silu(g_ref[0])


def _branch_b_sample(z3, state, layer, wdw, bdw, lng, lnb, wpw, bpw):
    nb, t_new, _ = z3.shape
    blk = lambda c: pl.BlockSpec((1, t_new, BRANCH_W), lambda b, c=c: (b, 0, c))
    vec = pl.BlockSpec((1, BRANCH_W), lambda b: (0, 0))
    row = pl.BlockSpec((1, t_new, BRANCH_W), lambda b: (b, 0, 0))
    return pl.pallas_call(
        functools.partial(_branch_b_sample_kernel, t_new=t_new),
        grid=(nb,),
        in_specs=[blk(3), blk(4), blk(5),
                  pl.BlockSpec((1, CONV_W - 1, BRANCH_W), lambda b: (layer * nb + b, 0, 0)),
                  pl.BlockSpec((CONV_PAD, BRANCH_W), lambda b: (0, 0)), vec, vec, vec,
                  pl.BlockSpec((BRANCH_W, BRANCH_W), lambda b: (0, 0)), vec],
        out_specs=[row, row],
        out_shape=[jax.ShapeDtypeStruct((nb, t_new, BRANCH_W), F32)] * 2,
        scratch_shapes=[pltpu.VMEM((CONV_PAD + 8, BRANCH_W), F32)],
        compiler_params=_cparams(1),
        name="branch_b_sample",
    )(z3, z3, z3, state, wdw, bdw, lng, lnb, wpw, bpw)


def _fox_kernel(q_ref, k_ref, v_ref, c_ref, g_ref, o_ref, vt_ref, s0_ref, s1_ref, m_ref, l_ref, acc_ref, *,
                tq, nk):
    qi = pl.program_id(2)

    @pl.when(qi == 0)
    def _():
        for j in range(nk):
            vt_ref[j] = v_ref[j * tq:(j + 1) * tq, :].astype(F32).T.astype(BF16)

    m_ref[...] = jnp.full_like(m_ref, NEG)
    l_ref[...] = jnp.zeros_like(l_ref)
    acc_ref[...] = jnp.zeros_like(acc_ref)
    qs = (q_ref[...].astype(F32) * (ATT_SCALE * LOG2E)).astype(BF16)

    def rows(kj):
        return pl.ds(pl.multiple_of(kj * tq, tq), tq)

    def scores(kj, s_ref):
        s_ref[...] = _dot_nt(k_ref[rows(kj), :].astype(BF16), qs)

    def softmax_pv(kj, s_ref, diagonal):
        c = c_ref[0, rows(kj), :]
        t = s_ref[...] - jnp.concatenate([c] * (tq // LANE), axis=1)
        if diagonal:
            key = lax.broadcasted_iota(jnp.int32, (tq, tq), 0)
            qry = lax.broadcasted_iota(jnp.int32, (tq, tq), 1)
            t = jnp.where(key <= qry, t, NEG)
        m_prev = m_ref[...]
        m_new = jnp.maximum(m_prev, jnp.max(t, axis=0, keepdims=True))
        alpha = jnp.exp2(m_prev - m_new)
        p = jnp.exp2(t - m_new)
        l_ref[...] = alpha * l_ref[...] + jnp.sum(p, axis=0, keepdims=True)
        acc_ref[...] = alpha * acc_ref[...] + _dot(vt_ref[kj], p.astype(BF16))
        m_ref[...] = m_new

    scores(0, s0_ref)

    def pair(p, carry):
        kj = 2 * p
        scores(kj + 1, s1_ref)
        softmax_pv(kj, s0_ref, False)
        scores(kj + 2, s0_ref)
        softmax_pv(kj + 1, s1_ref, False)
        return carry

    lax.fori_loop(0, lax.shift_right_logical(qi, 1), pair, 0)

    @pl.when((qi & 1) == 0)
    def _():
        softmax_pv(qi, s0_ref, True)

    @pl.when((qi & 1) == 1)
    def _():
        scores(qi, s1_ref)
        softmax_pv(qi - 1, s0_ref, False)
        softmax_pv(qi, s1_ref, True)

    o_ref[...] = ((acc_ref[...] / l_ref[...]).T * _silu(g_ref[...].astype(F32))).astype(o_ref.dtype)


def _fox_prompt(z, c_rep, *, batch, tq):
    m = z.shape[0]
    seq = m // batch
    nq = seq // tq
    qmap = lambda c: (lambda b, h, qi: (b * nq + qi, c + h))
    kvmap = lambda c: (lambda b, h, qi: (b, c + h))
    return pl.pallas_call(
        functools.partial(_fox_kernel, tq=tq, nk=nq),
        grid=(batch, H_C, nq),
        in_specs=[pl.BlockSpec((tq, HD), qmap(QC)),
                  pl.BlockSpec((seq, HD), kvmap(KC)),
                  pl.BlockSpec((seq, HD), kvmap(VC)),
                  pl.BlockSpec((1, seq, LANE), lambda b, h, qi: (h, b, 0)),
                  pl.BlockSpec((tq, HD), qmap(GC))],
        out_specs=pl.BlockSpec((tq, HD), lambda b, h, qi: (b * nq + qi, h)),
        out_shape=jax.ShapeDtypeStruct((m, BRANCH_W), BF16),
        scratch_shapes=[pltpu.VMEM((nq, HD, tq), BF16), pltpu.VMEM((tq, tq), F32), pltpu.VMEM((tq, tq), F32),
                        pltpu.VMEM((1, tq), F32), pltpu.VMEM((1, tq), F32), pltpu.VMEM((HD, tq), F32)],
        compiler_params=_cparams(3),
        name="fox_prompt",
    )(z, z, z, c_rep, z)


def _mem_attn_kernel(q_ref, k_ref, v_ref, g_ref, o_ref):
    s = _dot_nt(q_ref[...].astype(BF16), k_ref[...].astype(BF16)) * ATT_SCALE
    p = jnp.exp(s - jnp.max(s, axis=1, keepdims=True))
    o = _dot(p.astype(BF16), v_ref[...].astype(BF16)) / jnp.sum(p, axis=1, keepdims=True)
    o_ref[...] = (o * _silu(g_ref[...].astype(F32))).astype(o_ref.dtype)


def _mem_attn_prompt(z, mkv, *, batch, tq):
    m = z.shape[0]
    nq = m // batch // tq
    return pl.pallas_call(
        _mem_attn_kernel,
        grid=(batch, H_C, nq),
        in_specs=[pl.BlockSpec((tq, HD), lambda b, h, qi: (b * nq + qi, QM + h)),
                  pl.BlockSpec((N_MEM, HD), lambda b, h, qi: (b, h)),
                  pl.BlockSpec((N_MEM, HD), lambda b, h, qi: (b, H_C + h)),
                  pl.BlockSpec((tq, HD), lambda b, h, qi: (b * nq + qi, GM + h))],
        out_specs=pl.BlockSpec((tq, HD), lambda b, h, qi: (b * nq + qi, h)),
        out_shape=jax.ShapeDtypeStruct((m, BRANCH_W), BF16),
        compiler_params=_cparams(3),
        name="mem_attn_prompt",
    )(z, mkv, mkv, z)


def _head_match(rows, cols):
    r = lax.broadcasted_iota(jnp.int32, (rows, cols), 0)
    c = lax.broadcasted_iota(jnp.int32, (rows, cols), 1)
    return r, c, (r & (H_C - 1)) == (c & (H_C - 1))


def _mem_attn_sample_kernel(q_ref, g_ref, k_ref, v_ref, o_ref):
    s = _dot_nt(q_ref[0].astype(BF16), k_ref[0].astype(BF16)) * ATT_SCALE
    _, _, same = _head_match(s.shape[0], s.shape[1])
    s = jnp.where(same, s, NEG)
    p = jnp.exp(s - jnp.max(s, axis=1, keepdims=True))
    o = _dot(p.astype(BF16), v_ref[0].astype(BF16)) / jnp.sum(p, axis=1, keepdims=True)
    o_ref[0] = o * _silu(g_ref[0])


def _mem_attn_sample(q16, g16, mk, mv, layer):
    nb, nr, _ = q16.shape
    nm = mk.shape[1]
    row = pl.BlockSpec((1, nr, HD), lambda b: (b, 0, 0))
    mem = pl.BlockSpec((1, nm, HD), lambda b: (layer * nb + b, 0, 0))
    return pl.pallas_call(
        _mem_attn_sample_kernel,
        grid=(nb,),
        in_specs=[row, row, mem, mem],
        out_specs=row,
        out_shape=jax.ShapeDtypeStruct((nb, nr, HD), F32),
        compiler_params=_cparams(1),
        name="mem_attn_sample",
    )(q16, g16, mk, mv)


def _logf_pages_kernel(x_ref, mc_ref, mt_ref, o_ref):
    x = x_ref[...]
    o_ref[:, 0:PAGE_ROWS] = _dot_exact01(x, mc_ref[...])
    o_ref[:, PAGE_ROWS:2 * PAGE_ROWS] = _dot_exact01(x, mt_ref[...])


def _logf_pages(lf_flat, mc, mt, *, tm):
    n = lf_flat.shape[0]
    mat = pl.BlockSpec((PAGE_ROWS, PAGE_ROWS), lambda i: (0, 0))
    return pl.pallas_call(
        _logf_pages_kernel,
        grid=(n // tm,),
        in_specs=[pl.BlockSpec((tm, PAGE_ROWS), lambda i: (i, 0)), mat, mat],
        out_specs=pl.BlockSpec((tm, 2 * PAGE_ROWS), lambda i: (i, 0)),
        out_shape=jax.ShapeDtypeStruct((n, 2 * PAGE_ROWS), F32),
        compiler_params=_cparams(1),
        name="logf_pages",
    )(lf_flat, mc, mt)


def _fox_sample_kernel(pt_ref, q_ref, g_ref, kn_ref, vn_ref, lfn_ref, mn_ref, *rest, n_pages, n_new):
    del pt_ref
    np_ = PAGES_PER_STEP
    k_refs, v_refs, wt_refs = rest[:np_], rest[np_:2 * np_], rest[2 * np_:3 * np_]
    o_ref, m_ref, l_ref, acc_ref, carry_ref = rest[3 * np_:]
    step = pl.program_id(1)
    nr = q_ref.shape[1]

    @pl.when(step == 0)
    def _():
        m_ref[...] = jnp.full_like(m_ref, NEG)
        l_ref[...] = jnp.zeros_like(l_ref)
        acc_ref[...] = jnp.zeros_like(acc_ref)
        carry_ref[...] = jnp.zeros_like(carry_ref)

    q = q_ref[0].astype(BF16)

    def update(s_list, v_list):
        m_prev = m_ref[...]
        m_new = m_prev
        for s in s_list:
            m_new = jnp.maximum(m_new, jnp.max(s, axis=1, keepdims=True))
        alpha = jnp.exp(m_prev - m_new)
        l_new = alpha * l_ref[...]
        acc = alpha * acc_ref[...]
        for s, v in zip(s_list, v_list):
            p = jnp.exp(s - m_new)
            l_new = l_new + jnp.sum(p, axis=1, keepdims=True)
            acc = acc + _dot(p.astype(BF16), v)
        m_ref[...] = m_new
        l_ref[...] = l_new
        acc_ref[...] = acc

    _, _, same = _head_match(nr, PAGE_ROWS)
    carry = carry_ref[...]
    s_list, v_list = [], []
    for i in range(np_):
        wt = wt_refs[i][0]
        ck = carry + wt[:, 0:PAGE_ROWS]
        carry = carry + wt[:, PAGE_ROWS:2 * PAGE_ROWS]
        s = _dot_nt(q, k_refs[i][0].astype(BF16)) * ATT_SCALE - ck
        s_list.append(jnp.where(same, s, NEG))
        v_list.append(v_refs[i][0].astype(BF16))
    carry_ref[...] = carry
    update(s_list, v_list)

    @pl.when(step == n_pages // np_ - 1)
    def _():
        r, c, same_n = _head_match(nr, LANE)
        cn = carry[:, 0:LANE] + _dot_exact01(lfn_ref[0], mn_ref[...])[0:1, :]
        s = _dot_nt(q, kn_ref[0].astype(BF16)) * ATT_SCALE - cn
        ok = same_n & (c < n_new * H_C) & ((c >> 2) <= (r >> 2))
        update([jnp.where(ok, s, NEG)], [vn_ref[0].astype(BF16)])
        o_ref[0] = acc_ref[...] / l_ref[...] * _silu(g_ref[0])


def _fox_sample(pt, q16, g16, kn, vn, lfn, mn, kflat, vflat, wt3, *, n_new):
    nb, nr, _ = q16.shape
    n_pages = pt.shape[1]
    np_ = PAGES_PER_STEP
    row = pl.BlockSpec((1, nr, HD), lambda b, s, pt: (b, 0, 0))
    new = pl.BlockSpec((1, LANE, HD), lambda b, s, pt: (b, 0, 0))
    page = lambda i: (lambda b, s, pt: (pt[b, s * np_ + i], 0, 0))
    in_specs = [row, row, new, new,
                pl.BlockSpec((1, 8, LANE), lambda b, s, pt: (b, 0, 0)),
                pl.BlockSpec((LANE, LANE), lambda b, s, pt: (0, 0))]
    in_specs += [pl.BlockSpec((1, PAGE_ROWS, HD), page(i)) for i in range(np_)]
    in_specs += [pl.BlockSpec((1, PAGE_ROWS, HD), page(i)) for i in range(np_)]
    in_specs += [pl.BlockSpec((1, 1, 2 * PAGE_ROWS), page(i)) for i in range(np_)]
    grid_spec = pltpu.PrefetchScalarGridSpec(
        num_scalar_prefetch=1,
        grid=(nb, n_pages // np_),
        in_specs=in_specs,
        out_specs=pl.BlockSpec((1, nr, HD), lambda b, s, pt: (b, 0, 0)),
        scratch_shapes=[pltpu.VMEM((nr, 1), F32), pltpu.VMEM((nr, 1), F32), pltpu.VMEM((nr, HD), F32),
                        pltpu.VMEM((1, PAGE_ROWS), F32)],
    )
    return pl.pallas_call(
        functools.partial(_fox_sample_kernel, n_pages=n_pages, n_new=n_new),
        grid_spec=grid_spec,
        out_shape=jax.ShapeDtypeStruct((nb, nr, HD), F32),
        compiler_params=_cparams(2),
        name="fox_sample",
    )(pt, q16, g16, kn, vn, lfn, mn, *([kflat] * np_), *([vflat] * np_), *([wt3] * np_))


def _merge_kernel(*refs):
    o_refs, wb_ref, gate_refs, h_ref = refs[0:4], refs[4], refs[5:9], refs[9]
    outs = [o_refs[br][...].astype(BF16) for br in range(N_BRANCH)]
    for jc in range(D_MODEL // BRANCH_W):
        cs = slice(jc * BRANCH_W, (jc + 1) * BRANCH_W)
        acc = None
        for br in range(N_BRANCH):
            gate = 0.5 * jnp.tanh(0.5 * gate_refs[br][:, cs].astype(F32)) + 0.5
            term = gate * _dot(outs[br], wb_ref[br, :, cs])
            acc = term if acc is None else acc + term
        h_ref[:, cs] = acc.astype(h_ref.dtype)


def _merge(outs, z, wb, layer, *, tm):
    m = z.shape[0]
    o_spec = pl.BlockSpec((tm, BRANCH_W), lambda i: (i, 0))
    gate0 = GATE_BLK * BRANCH_W // D_MODEL
    gate = lambda br: pl.BlockSpec((tm, D_MODEL), lambda i, br=br: (i, gate0 + br))
    return pl.pallas_call(
        _merge_kernel,
        grid=(m // tm,),
        in_specs=[o_spec] * 4 + [pl.BlockSpec((None, N_BRANCH, BRANCH_W, D_MODEL), lambda i: (layer, 0, 0, 0))]
                 + [gate(br) for br in range(N_BRANCH)],
        out_specs=pl.BlockSpec((tm, D_MODEL), lambda i: (i, 0)),
        out_shape=jax.ShapeDtypeStruct((m, D_MODEL), BF16),
        compiler_params=_cparams(1),
        name="merge",
    )(*outs, wb, z, z, z, z)


def _out_ln_kernel(h_ref, w_ref, x_ref, g_ref, b_ref, y_ref, *, tm):
    half = max(tm // 2, SUBLANES)
    for s in range(tm // half):
        rs = slice(s * half, (s + 1) * half)
        y = ALPHA * x_ref[rs, :] + _dot(h_ref[rs, :], w_ref[...])
        y_ref[rs, :] = _ln(y, g_ref[...], b_ref[...])


def _out_ln(hm, wo, x, lng, lnb, layer, *, tm):
    m = x.shape[0]
    rows = pl.BlockSpec((tm, D_MODEL), lambda i: (i, 0))
    vec = pl.BlockSpec((1, D_MODEL), lambda i: (0, 0))
    return pl.pallas_call(
        functools.partial(_out_ln_kernel, tm=tm),
        grid=(m // tm,),
        in_specs=[rows, pl.BlockSpec((None, D_MODEL, D_MODEL), lambda i: (layer, 0, 0)), rows, vec, vec],
        out_specs=rows,
        out_shape=jax.ShapeDtypeStruct((m, D_MODEL), F32),
        compiler_params=_cparams(1),
        name="out_ln",
    )(hm, wo, x, lng, lnb)


def _tok_head_matrices():
    i = jnp.arange(PAGE_ROWS)
    same = (i[:, None] % H_C) == (i[None, :] % H_C)
    mc = (same & (i[:, None] // H_C <= i[None, :] // H_C)).astype(BF16)
    mt = same.astype(BF16)
    return mc, mt


def kernel(x_prompt, x_sample, mem_prompt, cache_k, cache_v, cache_logf, cache_mem_k, cache_mem_v, state_conv,
           page_table, w_in, w_mem_k, w_mem_v, ln_v_g, ln_v_b, w_s, b_s, w_dw, b_dw, ln_c_g, ln_c_b, w_pw, b_pw,
           b_f, w_branch, w_out, ln_g, ln_b):
    bp, seq, _ = x_prompt.shape
    db, t_new, _ = x_sample.shape
    n_pool = cache_k.shape[1]

    w_main, wf = _prep_w(w_in)
    bfb = jnp.pad(b_f, ((0, 0), (0, LANE - H_C)))[:, None, :]
    wb = w_branch.astype(BF16)
    wo = w_out.astype(BF16)
    wpw = w_pw.astype(BF16)
    wmkv = jnp.concatenate([w_mem_k, w_mem_v], axis=2).astype(BF16)
    wdw = jnp.pad(w_dw, ((0, 0), (0, CONV_PAD - CONV_W), (0, 0)))
    vec = lambda a: a[:, None, :]
    ln_v_g, ln_v_b, b_dw, ln_c_g, ln_c_b, b_pw, ln_g, ln_b = map(
        vec, (ln_v_g, ln_v_b, b_dw, ln_c_g, ln_c_b, b_pw, ln_g, ln_b))

    idx = jnp.arange(CHUNK)
    mask_p = (idx[None, :] <= idx[:, None]).astype(F32)
    bsb_p = jnp.broadcast_to(b_s[:, :, :, None], (DEPTH, A_GROUPS, CHUNK, CHUNK))
    reps = CHUNK // t_new
    mask_s = ((idx[:, None] // t_new == idx[None, :] // t_new) & (idx[None, :] <= idx[:, None])).astype(F32)
    ws_s = jnp.tile(w_s[:, :, :t_new, :t_new], (1, 1, reps, reps))
    bsb_s = jnp.broadcast_to(jnp.tile(b_s[:, :, :t_new], (1, 1, reps))[:, :, :, None],
                             (DEPTH, A_GROUPS, CHUNK, CHUNK))

    kflat = cache_k.reshape(DEPTH * n_pool, PAGE_ROWS, HD)
    vflat = cache_v.reshape(DEPTH * n_pool, PAGE_ROWS, HD)
    mc, mt = _tok_head_matrices()
    wt3 = _logf_pages(cache_logf.reshape(DEPTH * n_pool, PAGE_ROWS), mc, mt, tm=512)
    wt3 = wt3.reshape(DEPTH * n_pool, 1, 2 * PAGE_ROWS)
    mn = jnp.pad(mc[:t_new * H_C, :t_new * H_C], ((0, LANE - t_new * H_C),) * 2)
    memk = cache_mem_k.reshape(DEPTH * db, N_MEM * H_C, HD)
    memv = cache_mem_v.reshape(DEPTH * db, N_MEM * H_C, HD)
    state = state_conv.reshape(DEPTH * db, CONV_W - 1, BRANCH_W)

    xp = x_prompt.reshape(bp * seq, D_MODEL)
    xs = x_sample.reshape(db * t_new, D_MODEL)
    mem2d = mem_prompt.reshape(bp * N_MEM, D_MODEL)
    m_s = db * t_new
    outs = [[] for _ in range(11)]
    for l in range(DEPTH):
        z, lf, k_p, v_p, c_rep = _in_proj(xp, w_main, wf, bfb[l], l, tm=1024, tiles_per_seq=seq // 1024,
                                          emit_c=True, z_dtype=BF16)
        mkv = _mm(mem2d, wmkv[l], tm=bp * N_MEM, tn=2 * BRANCH_W)
        (oa,) = _branch_a(z, ln_v_g[l], ln_v_b[l], w_s[l], bsb_p[l], mask_p, tm=512, emit_v=False)
        ob, nconv = _branch_b(z, wdw[l], b_dw[l], ln_c_g[l], ln_c_b[l], wpw[l], b_pw[l], batch=bp, tm=512)
        oc = _fox_prompt(z, c_rep, batch=bp, tq=512)
        om = _mem_attn_prompt(z, mkv, batch=bp, tq=512)
        hm = _merge((oa, ob, oc, om), z, wb, l, tm=256)
        xp = _out_ln(hm, wo, xp, ln_g[l], ln_b[l], l, tm=256)
        outs[0].append(k_p.reshape(bp, seq, H_C, HD))
        outs[1].append(v_p.reshape(bp, seq, H_C, HD))
        outs[2].append(lf[:, :H_C].reshape(bp, seq, H_C))
        outs[3].append(nconv)
        outs[4].append(mkv[:, :BRANCH_W].reshape(bp, N_MEM, H_C, HD))
        outs[5].append(mkv[:, BRANCH_W:].reshape(bp, N_MEM, H_C, HD))

        zs, lfs, k_s, v_s = _in_proj(xs, w_main, wf, bfb[l], l, tm=m_s, tiles_per_seq=1, emit_c=False,
                                     z_dtype=F32)
        oa_s, v_rows = _branch_a(zs, ln_v_g[l], ln_v_b[l], ws_s[l], bsb_s[l], mask_s, tm=m_s, emit_v=True)
        ob_s, h_glu = _branch_b_sample(zs.reshape(db, t_new, Z_COLS), state, l, wdw[l], b_dw[l], ln_c_g[l],
                                       ln_c_b[l], wpw[l], b_pw[l])
        heads = lambda c: zs[:, c * LANE:c * LANE + BRANCH_W].reshape(db, t_new * H_C, HD)
        pad_new = lambda a: jnp.pad(a.reshape(db, t_new * H_C, HD), ((0, 0), (0, LANE - t_new * H_C), (0, 0)))
        lfn = lfs[:, :H_C].reshape(db, 1, t_new * H_C)
        lfn = jnp.pad(lfn, ((0, 0), (0, 7), (0, LANE - t_new * H_C)))
        oc_s = _fox_sample(page_table + l * n_pool, heads(QC), heads(GC), pad_new(k_s), pad_new(v_s),
                           lfn, mn, kflat, vflat, wt3, n_new=t_new)
        om_s = _mem_attn_sample(heads(QM), heads(GM), memk, memv, l)
        flat = lambda a: a.reshape(m_s, BRANCH_W)
        hm_s = _merge((oa_s, flat(ob_s), flat(oc_s), flat(om_s)), zs, wb, l, tm=m_s)
        xs = _out_ln(hm_s, wo, xs, ln_g[l], ln_b[l], l, tm=m_s)
        outs[6].append(k_s.reshape(db, t_new, H_C, HD))
        outs[7].append(v_s.reshape(db, t_new, H_C, HD))
        outs[8].append(lfs[:, :H_C].reshape(db, t_new, H_C))
        outs[9].append(jnp.concatenate([state_conv[l][:, t_new:], h_glu], axis=1))
        outs[10].append(v_rows.reshape(db, t_new, BRANCH_W))

    return (xp.reshape(bp, seq, D_MODEL), xs.reshape(db, t_new, D_MODEL)) + tuple(jnp.stack(o) for o in outs)
```

```python
import functools
import math

import jax
import jax.numpy as jnp
from jax import lax
from jax.experimental import pallas as pl
from jax.experimental.pallas import tpu as pltpu

F32 = jnp.float32
BF16 = jnp.bfloat16

D_MODEL = 2048
DEPTH = 2
BRANCH_W = 512
N_BRANCH = 4
CHUNK = 128
A_GROUPS = 4
CONV_W = 31
H_C = 4
HD = 128
N_MEM = 256
PAGE_SIZE = 128
LN_EPS = 1e-5
ALPHA = (2 * DEPTH) ** 0.25
ATT_SCALE = HD ** -0.5
LOG2E = math.log2(math.e)
NEG = -1e30

LANE = 128
SUBLANES = 8
PAGE_ROWS = PAGE_SIZE * H_C
F_COL = 9 * BRANCH_W
Z_COLS = 28 * BRANCH_W
QC, KC, VC, GC, QM, GM = 24, 28, 32, 36, 40, 44
GATE_BLK = 12
PAGES_PER_STEP = 16
VMEM_LIMIT = 48 * 1024 * 1024


def _cparams(n_axes, vmem=VMEM_LIMIT):
    return pltpu.CompilerParams(dimension_semantics=("arbitrary",) * n_axes, vmem_limit_bytes=vmem)


def _ln(x, g, b):
    mu = jnp.mean(x, axis=-1, keepdims=True)
    xc = x - mu
    var = jnp.mean(xc * xc, axis=-1, keepdims=True)
    return xc * lax.rsqrt(var + LN_EPS) * g + b


def _silu(x):
    return x * jax.nn.sigmoid(x)


def _log_sigmoid(x):
    return jnp.minimum(x, 0.0) - jnp.log1p(jnp.exp(-jnp.abs(x)))


def _dot(a, b):
    return jnp.dot(a, b, preferred_element_type=F32)


def _dot_nt(a, b):
    return lax.dot_general(a, b, (((1,), (1,)), ((), ())), preferred_element_type=F32)


def _split3(x):
    hi = x.astype(BF16)
    r = x - hi.astype(F32)
    mid = r.astype(BF16)
    lo = (r - mid.astype(F32)).astype(BF16)
    return hi, mid, lo


def _dot_exact01(x, m01):
    hi, mid, lo = _split3(x)
    return _dot(hi, m01) + _dot(mid, m01) + _dot(lo, m01)


def _dot_exact01_left(m01, x):
    hi, mid, lo = _split3(x)
    return _dot(m01, hi) + _dot(m01, mid) + _dot(m01, lo)


def _prep_w_kernel(a_ref, f_ref, o_ref, wf_ref):
    j = pl.program_id(0)
    for l in range(DEPTH):
        o_ref[l] = a_ref[:, l, :].T.astype(BF16)

    @pl.when(j == 0)
    def _():
        lane = lax.broadcasted_iota(jnp.int32, (D_MODEL, LANE), 1)
        for l in range(DEPTH):
            wf_ref[l] = jnp.where(lane < H_C, f_ref[:, l, :].T, 0.0).astype(BF16)


def _prep_w(w_in):
    wt = jnp.transpose(w_in, (2, 0, 1))
    elems = lambda rows: (pl.Element(rows), pl.Element(DEPTH), pl.Element(D_MODEL))
    tc = BRANCH_W
    return pl.pallas_call(
        _prep_w_kernel,
        grid=(Z_COLS // tc,),
        in_specs=[pl.BlockSpec(elems(tc), lambda j: (j * tc + jnp.where(j >= F_COL // tc, H_C, 0), 0, 0)),
                  pl.BlockSpec(elems(LANE), lambda j: (F_COL, 0, 0))],
        out_specs=[pl.BlockSpec((DEPTH, D_MODEL, tc), lambda j: (0, 0, j)),
                   pl.BlockSpec((DEPTH, D_MODEL, LANE), lambda j: (0, 0, 0))],
        out_shape=[jax.ShapeDtypeStruct((DEPTH, D_MODEL, Z_COLS), BF16),
                   jax.ShapeDtypeStruct((DEPTH, D_MODEL, LANE), BF16)],
        compiler_params=_cparams(1),
        name="prep_w",
    )(wt, wt)


def _in_proj_kernel(x_ref, w_ref, wf_ref, bf_ref, z_ref, lf_ref, k_ref, v_ref, *rest, tm, tn, tiles_per_seq, emit_c):
    if emit_c:
        c_ref, xb_ref, carry_ref = rest
    else:
        (xb_ref,) = rest
    i = pl.program_id(0)
    j = pl.program_id(1)

    @pl.when(j == 0)
    def _():
        xb = x_ref[...].astype(BF16)
        xb_ref[...] = xb
        lf = _log_sigmoid(_dot(xb, wf_ref[...]) + bf_ref[...])
        lf_ref[...] = lf
        if emit_c:
            @pl.when(lax.rem(i, tiles_per_seq) == 0)
            def _():
                carry_ref[...] = jnp.zeros_like(carry_ref)

            row = lax.broadcasted_iota(jnp.int32, (LANE, LANE), 0)
            col = lax.broadcasted_iota(jnp.int32, (LANE, LANE), 1)
            lower = jnp.where(col <= row, 1.0, 0.0).astype(BF16)
            carry = carry_ref[...]
            for r in range(tm // LANE):
                rs = slice(r * LANE, (r + 1) * LANE)
                cblk = _dot_exact01_left(lower, lf[rs, :]) + carry
                carry = cblk[LANE - 1:LANE, :]
                c2 = cblk * LOG2E
                for h in range(H_C):
                    c_ref[h, rs, :] = jnp.broadcast_to(c2[:, h:h + 1], (LANE, LANE))
            carry_ref[...] = carry

    acc = _dot(xb_ref[...], w_ref[...])
    z_ref[...] = acc.astype(z_ref.dtype)

    def heads_out(o_ref, col0):
        for h in range(H_C):
            o_ref[:, h, :] = acc[:, col0 + h * HD:col0 + (h + 1) * HD]

    @pl.when(j == KC * LANE // tn)
    def _():
        heads_out(k_ref, KC * LANE % tn)

    @pl.when(j == VC * LANE // tn)
    def _():
        heads_out(v_ref, VC * LANE % tn)


def _in_proj(x, w_main, wf, bfb, layer, *, tm, tiles_per_seq, emit_c, z_dtype):
    m = x.shape[0]
    tn = 2 * BRANCH_W
    kern = functools.partial(_in_proj_kernel, tm=tm, tn=tn, tiles_per_seq=tiles_per_seq, emit_c=emit_c)
    heads = pl.BlockSpec((tm, H_C, HD), lambda i, j: (i, 0, 0))
    out_specs = [pl.BlockSpec((tm, tn), lambda i, j: (i, j)),
                 pl.BlockSpec((tm, LANE), lambda i, j: (i, 0)), heads, heads]
    out_shape = [jax.ShapeDtypeStruct((m, Z_COLS), z_dtype), jax.ShapeDtypeStruct((m, LANE), F32),
                 jax.ShapeDtypeStruct((m, H_C, HD), F32), jax.ShapeDtypeStruct((m, H_C, HD), F32)]
    scratch = [pltpu.VMEM((tm, D_MODEL), BF16)]
    if emit_c:
        out_specs.append(pl.BlockSpec((H_C, tm, LANE), lambda i, j: (0, i, 0)))
        out_shape.append(jax.ShapeDtypeStruct((H_C, m, LANE), F32))
        scratch.append(pltpu.VMEM((1, LANE), F32))
    return pl.pallas_call(
        kern,
        grid=(m // tm, Z_COLS // tn),
        in_specs=[
            pl.BlockSpec((tm, D_MODEL), lambda i, j: (i, 0), pipeline_mode=pl.Buffered(1)),
            pl.BlockSpec((None, D_MODEL, tn), lambda i, j: (layer, 0, j)),
            pl.BlockSpec((None, D_MODEL, LANE), lambda i, j: (layer, 0, 0)),
            pl.BlockSpec((1, LANE), lambda i, j: (0, 0)),
        ],
        out_specs=out_specs,
        out_shape=out_shape,
        scratch_shapes=scratch,
        compiler_params=_cparams(2),
        name="in_proj",
    )(x, w_main, wf, bfb)


def _mm_kernel(x_ref, w_ref, o_ref):
    o_ref[...] = _dot(x_ref[...].astype(BF16), w_ref[...])


def _mm(x, w, *, tm, tn):
    m, k = x.shape
    n = w.shape[1]
    return pl.pallas_call(
        _mm_kernel,
        grid=(m // tm, n // tn),
        in_specs=[pl.BlockSpec((tm, k), lambda i, j: (i, 0)), pl.BlockSpec((k, tn), lambda i, j: (0, j))],
        out_specs=pl.BlockSpec((tm, tn), lambda i, j: (i, j)),
        out_shape=jax.ShapeDtypeStruct((m, n), F32),
        compiler_params=_cparams(2),
        name="mem_kv_proj",
    )(x, w)


def _branch_a_kernel(u_ref, v_ref, g_ref, lng_ref, lnb_ref, ws_ref, bsb_ref, mask_ref, o_ref, *rest,
                     tm, emit_v):
    u = jax.nn.gelu(u_ref[...].astype(F32))
    v = _ln(jax.nn.gelu(v_ref[...].astype(F32)), lng_ref[...], lnb_ref[...])
    if emit_v:
        rest[0][...] = v
    gate = _silu(g_ref[...].astype(F32))
    keep = mask_ref[...] > 0.0
    for g in range(A_GROUPS):
        wg = jnp.where(keep, ws_ref[g], 0.0).astype(BF16)
        cs = slice(g * LANE, (g + 1) * LANE)
        for c in range(tm // CHUNK):
            rs = slice(c * CHUNK, (c + 1) * CHUNK)
            s = _dot(wg, v[rs, cs].astype(BF16)) + bsb_ref[g]
            o_ref[rs, cs] = (u[rs, cs] * s * gate[rs, cs]).astype(o_ref.dtype)


def _branch_a(z, lng, lnb, ws, bsb, mask, *, tm, emit_v):
    m = z.shape[0]
    blk = lambda c: pl.BlockSpec((tm, BRANCH_W), lambda i, c=c: (i, c))
    vec = pl.BlockSpec((1, BRANCH_W), lambda i: (0, 0))
    cube = pl.BlockSpec((A_GROUPS, CHUNK, CHUNK), lambda i: (0, 0, 0))
    out_specs = [pl.BlockSpec((tm, BRANCH_W), lambda i: (i, 0))]
    out_shape = [jax.ShapeDtypeStruct((m, BRANCH_W), BF16)]
    if emit_v:
        out_specs.append(pl.BlockSpec((tm, BRANCH_W), lambda i: (i, 0)))
        out_shape.append(jax.ShapeDtypeStruct((m, BRANCH_W), F32))
    return pl.pallas_call(
        functools.partial(_branch_a_kernel, tm=tm, emit_v=emit_v),
        grid=(m // tm,),
        in_specs=[blk(0), blk(1), blk(2), vec, vec, cube, cube,
                  pl.BlockSpec((CHUNK, CHUNK), lambda i: (0, 0))],
        out_specs=out_specs,
        out_shape=out_shape,
        compiler_params=_cparams(1),
        name="branch_a",
    )(z, z, z, lng, lnb, ws, bsb, mask)


CONV_PAD = 32


def _conv_tail(y, bdw_ref, lng_ref, lnb_ref, wpw_ref, bpw_ref, gate):
    y = _silu(_ln(y + bdw_ref[...], lng_ref[...], lnb_ref[...]))
    return (_dot(y.astype(BF16), wpw_ref[...]) + bpw_ref[...]) * _silu(gate)


def _branch_b_kernel(a_ref, b_ref, g_ref, wdw_ref, bdw_ref, lng_ref, lnb_ref, wpw_ref, bpw_ref,
                     o_ref, nc_ref, hp_ref, sw_ref, *, tm):
    t = pl.program_id(1)

    @pl.when(t == 0)
    def _():
        hp_ref[0:CONV_PAD, :] = jnp.zeros((CONV_PAD, BRANCH_W), F32)

    hp_ref[CONV_PAD:CONV_PAD + tm, :] = a_ref[...].astype(F32) * jax.nn.sigmoid(b_ref[...].astype(F32))
    off = CONV_PAD - (CONV_W - 1)
    acc = None
    for r in range(SUBLANES):
        taps = [j for j in range(CONV_W) if (off + j) % SUBLANES == r]
        rows = tm + taps[-1] - taps[0]
        if r:
            sw_ref[0:rows, :] = hp_ref[pl.ds(off + taps[0], rows), :]
        for j in taps:
            src = sw_ref[j - taps[0]:j - taps[0] + tm, :] if r else hp_ref[off + j:off + j + tm, :]
            term = src * wdw_ref[j:j + 1, :]
            acc = term if acc is None else acc + term
    gate = g_ref[...].astype(F32)
    o_ref[...] = _conv_tail(acc, bdw_ref, lng_ref, lnb_ref, wpw_ref, bpw_ref, gate).astype(o_ref.dtype)
    nc_ref[0] = hp_ref[pl.ds(CONV_PAD + tm - (CONV_W - 1), CONV_W - 1), :]
    hp_ref[0:CONV_PAD, :] = hp_ref[tm:tm + CONV_PAD, :]


def _branch_b(z, wdw, bdw, lng, lnb, wpw, bpw, *, batch, tm):
    m = z.shape[0]
    nt = m // batch // tm
    blk = lambda c: pl.BlockSpec((tm, BRANCH_W), lambda b, t, c=c: (b * nt + t, c))
    vec = pl.BlockSpec((1, BRANCH_W), lambda b, t: (0, 0))
    return pl.pallas_call(
        functools.partial(_branch_b_kernel, tm=tm),
        grid=(batch, nt),
        in_specs=[blk(3), blk(4), blk(5),
                  pl.BlockSpec((CONV_PAD, BRANCH_W), lambda b, t: (0, 0)), vec, vec, vec,
                  pl.BlockSpec((BRANCH_W, BRANCH_W), lambda b, t: (0, 0)), vec],
        out_specs=[pl.BlockSpec((tm, BRANCH_W), lambda b, t: (b * nt + t, 0)),
                   pl.BlockSpec((1, CONV_W - 1, BRANCH_W), lambda b, t: (b, 0, 0))],
        out_shape=[jax.ShapeDtypeStruct((m, BRANCH_W), BF16),
                   jax.ShapeDtypeStruct((batch, CONV_W - 1, BRANCH_W), F32)],
        scratch_shapes=[pltpu.VMEM((tm + CONV_PAD, BRANCH_W), F32)] * 2,
        compiler_params=_cparams(2),
        name="branch_b",
    )(z, z, z, wdw, bdw, lng, lnb, wpw, bpw)


def _branch_b_sample_kernel(a_ref, b_ref, g_ref, st_ref, wdw_ref, bdw_ref, lng_ref, lnb_ref, wpw_ref,
                            bpw_ref, o_ref, h_ref, hp_ref, *, t_new):
    n_hist = CONV_W - 1
    h = a_ref[0] * jax.nn.sigmoid(b_ref[0])
    h_ref[0] = h
    hp_ref[CONV_PAD:CONV_PAD + 8, :] = jnp.zeros((8, BRANCH_W), F32)
    hp_ref[0:n_hist, :] = st_ref[0]
    hp_ref[n_hist:n_hist + t_new, :] = h
    acc = hp_ref[pl.ds(0, 8), :] * wdw_ref[0:1, :]
    for j in range(1, CONV_W):
        acc = acc + hp_ref[pl.ds(j, 8), :] * wdw_ref[j:j + 1, :]
    y = _silu(_ln(acc + bdw_ref[...], lng_ref[...], lnb_ref[...]))
    out = _dot(y.astype(BF16), wpw_ref[...]) + bpw_ref[...]
    o_ref[0] = out[0:t_new, :] * _silu(g_ref[0])


def _branch_b_sample(z3, state, layer, wdw, bdw, lng, lnb, wpw, bpw):
    nb, t_new, _ = z3.shape
    blk = lambda c: pl.BlockSpec((1, t_new, BRANCH_W), lambda b, c=c: (b, 0, c))
    vec = pl.BlockSpec((1, BRANCH_W), lambda b: (0, 0))
    row = pl.BlockSpec((1, t_new, BRANCH_W), lambda b: (b, 0, 0))
    return pl.pallas_call(
        functools.partial(_branch_b_sample_kernel, t_new=t_new),
        grid=(nb,),
        in_specs=[blk(3), blk(4), blk(5),
                  pl.BlockSpec((1, CONV_W - 1, BRANCH_W), lambda b: (layer * nb + b, 0, 0)),
                  pl.BlockSpec((CONV_PAD, BRANCH_W), lambda b: (0, 0)), vec, vec, vec,
                  pl.BlockSpec((BRANCH_W, BRANCH_W), lambda b: (0, 0)), vec],
        out_specs=[row, row],
        out_shape=[jax.ShapeDtypeStruct((nb, t_new, BRANCH_W), F32)] * 2,
        scratch_shapes=[pltpu.VMEM((CONV_PAD + 8, BRANCH_W), F32)],
        compiler_params=_cparams(1),
        name="branch_b_sample",
    )(z3, z3, z3, state, wdw, bdw, lng, lnb, wpw, bpw)


def _fox_kernel(q_ref, k_ref, v_ref, c_ref, g_ref, o_ref, vt_ref, s0_ref, s1_ref, m_ref, l_ref, acc_ref, *,
                tq, nk):
    qi = pl.program_id(2)

    @pl.when(qi == 0)
    def _():
        for j in range(nk):
            vt_ref[j] = v_ref[j * tq:(j + 1) * tq, :].astype(F32).T.astype(BF16)

    m_ref[...] = jnp.full_like(m_ref, NEG)
    l_ref[...] = jnp.zeros_like(l_ref)
    acc_ref[...] = jnp.zeros_like(acc_ref)
    qs = (q_ref[...].astype(F32) * (ATT_SCALE * LOG2E)).astype(BF16)

    def rows(kj):
        return pl.ds(pl.multiple_of(kj * tq, tq), tq)

    def scores(kj, s_ref):
        s_ref[...] = _dot_nt(k_ref[rows(kj), :].astype(BF16), qs)

    def softmax_pv(kj, s_ref, diagonal):
        c = c_ref[0, rows(kj), :]
        t = s_ref[...] - jnp.concatenate([c] * (tq // LANE), axis=1)
        if diagonal:
            key = lax.broadcasted_iota(jnp.int32, (tq, tq), 0)
            qry = lax.broadcasted_iota(jnp.int32, (tq, tq), 1)
            t = jnp.where(key <= qry, t, NEG)
        m_prev = m_ref[...]
        m_new = jnp.maximum(m_prev, jnp.max(t, axis=0, keepdims=True))
        alpha = jnp.exp2(m_prev - m_new)
        p = jnp.exp2(t - m_new)
        l_ref[...] = alpha * l_ref[...] + jnp.sum(p, axis=0, keepdims=True)
        acc_ref[...] = alpha * acc_ref[...] + _dot(vt_ref[kj], p.astype(BF16))
        m_ref[...] = m_new

    scores(0, s0_ref)

    def pair(p, carry):
        kj = 2 * p
        scores(kj + 1, s1_ref)
        softmax_pv(kj, s0_ref, False)
        scores(kj + 2, s0_ref)
        softmax_pv(kj + 1, s1_ref, False)
        return carry

    lax.fori_loop(0, lax.shift_right_logical(qi, 1), pair, 0)

    @pl.when((qi & 1) == 0)
    def _():
        softmax_pv(qi, s0_ref, True)

    @pl.when((qi & 1) == 1)
    def _():
        scores(qi, s1_ref)
        softmax_pv(qi - 1, s0_ref, False)
        softmax_pv(qi, s1_ref, True)

    o_ref[...] = ((acc_ref[...] / l_ref[...]).T * _silu(g_ref[...].astype(F32))).astype(o_ref.dtype)


def _fox_prompt(z, c_rep, *, batch, tq):
    m = z.shape[0]
    seq = m // batch
    nq = seq // tq
    qmap = lambda c: (lambda b, h, qi: (b * nq + qi, c + h))
    kvmap = lambda c: (lambda b, h, qi: (b, c + h))
    return pl.pallas_call(
        functools.partial(_fox_kernel, tq=tq, nk=nq),
        grid=(batch, H_C, nq),
        in_specs=[pl.BlockSpec((tq, HD), qmap(QC)),
                  pl.BlockSpec((seq, HD), kvmap(KC)),
                  pl.BlockSpec((seq, HD), kvmap(VC)),
                  pl.BlockSpec((1, seq, LANE), lambda b, h, qi: (h, b, 0)),
                  pl.BlockSpec((tq, HD), qmap(GC))],
        out_specs=pl.BlockSpec((tq, HD), lambda b, h, qi: (b * nq + qi, h)),
        out_shape=jax.ShapeDtypeStruct((m, BRANCH_W), BF16),
        scratch_shapes=[pltpu.VMEM((nq, HD, tq), BF16), pltpu.VMEM((tq, tq), F32), pltpu.VMEM((tq, tq), F32),
                        pltpu.VMEM((1, tq), F32), pltpu.VMEM((1, tq), F32), pltpu.VMEM((HD, tq), F32)],
        compiler_params=_cparams(3),
        name="fox_prompt",
    )(z, z, z, c_rep, z)


def _mem_attn_kernel(q_ref, k_ref, v_ref, g_ref, o_ref):
    for h in range(H_C):
        cs = slice(h * HD, (h + 1) * HD)
        s = _dot_nt(q_ref[:, cs].astype(BF16), k_ref[:, cs].astype(BF16)) * ATT_SCALE
        p = jnp.exp(s - jnp.max(s, axis=1, keepdims=True))
        o = _dot(p.astype(BF16), v_ref[:, cs].astype(BF16)) / jnp.sum(p, axis=1, keepdims=True)
        o_ref[:, cs] = (o * _silu(g_ref[:, cs].astype(F32))).astype(o_ref.dtype)


def _mem_attn_prompt(z, mkv, *, batch, tq):
    m = z.shape[0]
    nq = m // batch // tq
    wide = lambda c: pl.BlockSpec((tq, BRANCH_W), lambda b, qi, c=c: (b * nq + qi, c * LANE // BRANCH_W))
    return pl.pallas_call(
        _mem_attn_kernel,
        grid=(batch, nq),
        in_specs=[wide(QM),
                  pl.BlockSpec((N_MEM, BRANCH_W), lambda b, qi: (b, 0)),
                  pl.BlockSpec((N_MEM, BRANCH_W), lambda b, qi: (b, 1)),
                  wide(GM)],
        out_specs=pl.BlockSpec((tq, BRANCH_W), lambda b, qi: (b * nq + qi, 0)),
        out_shape=jax.ShapeDtypeStruct((m, BRANCH_W), BF16),
        compiler_params=_cparams(2),
        name="mem_attn_prompt",
    )(z, mkv, mkv, z)


def _head_match(rows, cols):
    r = lax.broadcasted_iota(jnp.int32, (rows, cols), 0)
    c = lax.broadcasted_iota(jnp.int32, (rows, cols), 1)
    return r, c, (r & (H_C - 1)) == (c & (H_C - 1))


def _mem_attn_sample_kernel(q_ref, g_ref, k_ref, v_ref, o_ref):
    s = _dot_nt(q_ref[0].astype(BF16), k_ref[0].astype(BF16)) * ATT_SCALE
    _, _, same = _head_match(s.shape[0], s.shape[1])
    s = jnp.where(same, s, NEG)
    p = jnp.exp(s - jnp.max(s, axis=1, keepdims=True))
    o = _dot(p.astype(BF16), v_ref[0].astype(BF16)) / jnp.sum(p, axis=1, keepdims=True)
    o_ref[0] = o * _silu(g_ref[0])


def _mem_attn_sample(q16, g16, mk, mv, layer):
    nb, nr, _ = q16.shape
    nm = mk.shape[1]
    row = pl.BlockSpec((1, nr, HD), lambda b: (b, 0, 0))
    mem = pl.BlockSpec((1, nm, HD), lambda b: (layer * nb + b, 0, 0))
    return pl.pallas_call(
        _mem_attn_sample_kernel,
        grid=(nb,),
        in_specs=[row, row, mem, mem],
        out_specs=row,
        out_shape=jax.ShapeDtypeStruct((nb, nr, HD), F32),
        compiler_params=_cparams(1),
        name="mem_attn_sample",
    )(q16, g16, mk, mv)


def _logf_pages_kernel(x_ref, mc_ref, mt_ref, o_ref):
    x = x_ref[...]
    o_ref[:, 0, 0:PAGE_ROWS] = _dot_exact01(x, mc_ref[...])
    o_ref[:, 0, PAGE_ROWS:2 * PAGE_ROWS] = _dot_exact01(x, mt_ref[...])


def _logf_pages(lf_flat, mc, mt, *, tm):
    n = lf_flat.shape[0]
    mat = pl.BlockSpec((PAGE_ROWS, PAGE_ROWS), lambda i: (0, 0))
    return pl.pallas_call(
        _logf_pages_kernel,
        grid=(n // tm,),
        in_specs=[pl.BlockSpec((tm, PAGE_ROWS), lambda i: (i, 0)), mat, mat],
        out_specs=pl.BlockSpec((tm, 1, 2 * PAGE_ROWS), lambda i: (i, 0, 0)),
        out_shape=jax.ShapeDtypeStruct((n, 1, 2 * PAGE_ROWS), F32),
        compiler_params=_cparams(1),
        name="logf_pages",
    )(lf_flat, mc, mt)


def _softmax_update(state, s_list, v_list):
    m_prev, l_prev, acc_prev = state
    m_new = m_prev
    for s in s_list:
        m_new = jnp.maximum(m_new, jnp.max(s, axis=1, keepdims=True))
    alpha = jnp.exp(m_prev - m_new)
    l_new = alpha * l_prev
    acc = alpha * acc_prev
    for s, v in zip(s_list, v_list):
        p = jnp.exp(s - m_new)
        l_new = l_new + jnp.sum(p, axis=1, keepdims=True)
        acc = acc + _dot(p.astype(BF16), v)
    return m_new, l_new, acc


def _fox_sample_kernel(pt_ref, q_ref, g_ref, kn_ref, vn_ref, lfn_ref, mn_ref, k_hbm, v_hbm, wt_hbm, o_ref,
                       kbuf, vbuf, wtbuf, sem, m_ref, l_ref, acc_ref, carry_ref, *, steps_per_seq, n_new):
    np_ = PAGES_PER_STEP
    seq = pl.program_id(0)
    sub = pl.program_id(1)
    step = seq * steps_per_seq + sub
    n_steps = pl.num_programs(0) * steps_per_seq
    nr = q_ref.shape[1]

    def slot_copies(slot, page_of):
        copies = []
        for p in range(np_):
            page = page_of(p)
            copies.append(pltpu.make_async_copy(k_hbm.at[page], kbuf.at[slot, p], sem.at[slot]))
            copies.append(pltpu.make_async_copy(v_hbm.at[page], vbuf.at[slot, p], sem.at[slot]))
            copies.append(pltpu.make_async_copy(wt_hbm.at[page], wtbuf.at[slot, p], sem.at[slot]))
        return copies

    def start_step(for_step, slot):
        b = lax.shift_right_logical(for_step, steps_per_seq.bit_length() - 1)
        first_page = (for_step & (steps_per_seq - 1)) * np_
        for c in slot_copies(slot, lambda p: pt_ref[b, first_page + p]):
            c.start()

    slot = step & 1

    @pl.when(step == 0)
    def _():
        start_step(step, slot)

    @pl.when(step + 1 < n_steps)
    def _():
        start_step(step + 1, 1 - slot)

    for c in slot_copies(slot, lambda p: 0):
        c.wait()

    first = sub == 0
    q = q_ref[0].astype(BF16)
    state = (jnp.where(first, NEG, m_ref[...]), jnp.where(first, 0.0, l_ref[...]),
             jnp.where(first, 0.0, acc_ref[...]))
    carry = jnp.where(first, 0.0, carry_ref[...])
    _, _, same = _head_match(nr, PAGE_ROWS)
    s_list, v_list = [], []
    for p in range(np_):
        wt = wtbuf[slot, p]
        ck = carry + wt[:, 0:PAGE_ROWS]
        carry = carry + wt[:, PAGE_ROWS:2 * PAGE_ROWS]
        s = _dot_nt(q, kbuf[slot, p].astype(BF16)) * ATT_SCALE - ck
        s_list.append(jnp.where(same, s, NEG))
        v_list.append(vbuf[slot, p].astype(BF16))
    state = _softmax_update(state, s_list, v_list)
    m_ref[...], l_ref[...], acc_ref[...] = state
    carry_ref[...] = carry

    @pl.when(sub == steps_per_seq - 1)
    def _():
        r, c, same_n = _head_match(nr, LANE)
        cn = carry[:, 0:LANE] + _dot_exact01(lfn_ref[0], mn_ref[...])[0:1, :]
        s = _dot_nt(q, kn_ref[0].astype(BF16)) * ATT_SCALE - cn
        ok = same_n & (c < n_new * H_C) & ((c >> 2) <= (r >> 2))
        _, l_fin, acc_fin = _softmax_update(state, [jnp.where(ok, s, NEG)], [vn_ref[0].astype(BF16)])
        o_ref[0] = acc_fin / l_fin * _silu(g_ref[0])


def _fox_sample(pt, q16, g16, kn, vn, lfn, mn, kflat, vflat, wt3, *, n_new):
    nb, nr, _ = q16.shape
    n_pages = pt.shape[1]
    np_ = PAGES_PER_STEP
    steps_per_seq = n_pages // np_
    assert steps_per_seq * np_ == n_pages and steps_per_seq & (steps_per_seq - 1) == 0
    row = pl.BlockSpec((1, nr, HD), lambda b, s, pt: (b, 0, 0))
    new = pl.BlockSpec((1, LANE, HD), lambda b, s, pt: (b, 0, 0))
    pool = pl.BlockSpec(memory_space=pl.ANY)
    grid_spec = pltpu.PrefetchScalarGridSpec(
        num_scalar_prefetch=1,
        grid=(nb, steps_per_seq),
        in_specs=[row, row, new, new,
                  pl.BlockSpec((1, 8, LANE), lambda b, s, pt: (b, 0, 0)),
                  pl.BlockSpec((LANE, LANE), lambda b, s, pt: (0, 0)),
                  pool, pool, pool],
        out_specs=pl.BlockSpec((1, nr, HD), lambda b, s, pt: (b, 0, 0)),
        scratch_shapes=[pltpu.VMEM((2, np_, PAGE_ROWS, HD), F32), pltpu.VMEM((2, np_, PAGE_ROWS, HD), F32),
                        pltpu.VMEM((2, np_, 1, 2 * PAGE_ROWS), F32), pltpu.SemaphoreType.DMA((2,)),
                        pltpu.VMEM((nr, 1), F32), pltpu.VMEM((nr, 1), F32), pltpu.VMEM((nr, HD), F32),
                        pltpu.VMEM((1, PAGE_ROWS), F32)],
    )
    return pl.pallas_call(
        functools.partial(_fox_sample_kernel, steps_per_seq=steps_per_seq, n_new=n_new),
        grid_spec=grid_spec,
        out_shape=jax.ShapeDtypeStruct((nb, nr, HD), F32),
        compiler_params=_cparams(2),
        name="fox_sample",
    )(pt, q16, g16, kn, vn, lfn, mn, kflat, vflat, wt3)


def _merge_kernel(*refs):
    o_refs, wb_ref, gate_refs, h_ref = refs[0:4], refs[4], refs[5:9], refs[9]
    outs = [o_refs[br][...].astype(BF16) for br in range(N_BRANCH)]
    for jc in range(D_MODEL // BRANCH_W):
        cs = slice(jc * BRANCH_W, (jc + 1) * BRANCH_W)
        acc = None
        for br in range(N_BRANCH):
            gate = 0.5 * jnp.tanh(0.5 * gate_refs[br][:, cs].astype(F32)) + 0.5
            term = gate * _dot(outs[br], wb_ref[br, :, cs])
            acc = term if acc is None else acc + term
        h_ref[:, cs] = acc.astype(h_ref.dtype)


def _merge(outs, z, wb, layer, *, tm):
    m = z.shape[0]
    o_spec = pl.BlockSpec((tm, BRANCH_W), lambda i: (i, 0))
    gate0 = GATE_BLK * BRANCH_W // D_MODEL
    gate = lambda br: pl.BlockSpec((tm, D_MODEL), lambda i, br=br: (i, gate0 + br))
    return pl.pallas_call(
        _merge_kernel,
        grid=(m // tm,),
        in_specs=[o_spec] * 4 + [pl.BlockSpec((None, N_BRANCH, BRANCH_W, D_MODEL), lambda i: (layer, 0, 0, 0))]
                 + [gate(br) for br in range(N_BRANCH)],
        out_specs=pl.BlockSpec((tm, D_MODEL), lambda i: (i, 0)),
        out_shape=jax.ShapeDtypeStruct((m, D_MODEL), BF16),
        compiler_params=_cparams(1),
        name="merge",
    )(*outs, wb, z, z, z, z)


def _out_ln_kernel(h_ref, w_ref, x_ref, g_ref, b_ref, y_ref, *, tm):
    half = max(tm // 2, SUBLANES)
    for s in range(tm // half):
        rs = slice(s * half, (s + 1) * half)
        y = ALPHA * x_ref[rs, :] + _dot(h_ref[rs, :], w_ref[...])
        y_ref[rs, :] = _ln(y, g_ref[...], b_ref[...])


def _out_ln(hm, wo, x, lng, lnb, layer, *, tm):
    m = x.shape[0]
    rows = pl.BlockSpec((tm, D_MODEL), lambda i: (i, 0))
    vec = pl.BlockSpec((1, D_MODEL), lambda i: (0, 0))
    return pl.pallas_call(
        functools.partial(_out_ln_kernel, tm=tm),
        grid=(m // tm,),
        in_specs=[rows, pl.BlockSpec((None, D_MODEL, D_MODEL), lambda i: (layer, 0, 0)), rows, vec, vec],
        out_specs=rows,
        out_shape=jax.ShapeDtypeStruct((m, D_MODEL), F32),
        compiler_params=_cparams(1),
        name="out_ln",
    )(hm, wo, x, lng, lnb)


def _tok_head_matrices():
    i = jnp.arange(PAGE_ROWS)
    same = (i[:, None] % H_C) == (i[None, :] % H_C)
    mc = (same & (i[:, None] // H_C <= i[None, :] // H_C)).astype(BF16)
    mt = same.astype(BF16)
    return mc, mt


def kernel(x_prompt, x_sample, mem_prompt, cache_k, cache_v, cache_logf, cache_mem_k, cache_mem_v, state_conv,
           page_table, w_in, w_mem_k, w_mem_v, ln_v_g, ln_v_b, w_s, b_s, w_dw, b_dw, ln_c_g, ln_c_b, w_pw, b_pw,
           b_f, w_branch, w_out, ln_g, ln_b):
    bp, seq, _ = x_prompt.shape
    db, t_new, _ = x_sample.shape
    n_pool = cache_k.shape[1]

    w_main, wf = _prep_w(w_in)
    bfb = jnp.pad(b_f, ((0, 0), (0, LANE - H_C)))[:, None, :]
    wb = w_branch.astype(BF16)
    wo = w_out.astype(BF16)
    wpw = w_pw.astype(BF16)
    wmkv = jnp.concatenate([w_mem_k, w_mem_v], axis=2).astype(BF16)
    wdw = jnp.pad(w_dw, ((0, 0), (0, CONV_PAD - CONV_W), (0, 0)))
    vec = lambda a: a[:, None, :]
    ln_v_g, ln_v_b, b_dw, ln_c_g, ln_c_b, b_pw, ln_g, ln_b = map(
        vec, (ln_v_g, ln_v_b, b_dw, ln_c_g, ln_c_b, b_pw, ln_g, ln_b))

    idx = jnp.arange(CHUNK)
    mask_p = (idx[None, :] <= idx[:, None]).astype(F32)
    bsb_p = jnp.broadcast_to(b_s[:, :, :, None], (DEPTH, A_GROUPS, CHUNK, CHUNK))
    reps = CHUNK // t_new
    mask_s = ((idx[:, None] // t_new == idx[None, :] // t_new) & (idx[None, :] <= idx[:, None])).astype(F32)
    ws_s = jnp.tile(w_s[:, :, :t_new, :t_new], (1, 1, reps, reps))
    bsb_s = jnp.broadcast_to(jnp.tile(b_s[:, :, :t_new], (1, 1, reps))[:, :, :, None],
                             (DEPTH, A_GROUPS, CHUNK, CHUNK))

    kflat = cache_k.reshape(DEPTH * n_pool, PAGE_ROWS, HD)
    vflat = cache_v.reshape(DEPTH * n_pool, PAGE_ROWS, HD)
    mc, mt = _tok_head_matrices()
    wt3 = _logf_pages(cache_logf.reshape(DEPTH * n_pool, PAGE_ROWS), mc, mt, tm=512)
    mn = jnp.pad(mc[:t_new * H_C, :t_new * H_C], ((0, LANE - t_new * H_C),) * 2)
    memk = cache_mem_k.reshape(DEPTH * db, N_MEM * H_C, HD)
    memv = cache_mem_v.reshape(DEPTH * db, N_MEM * H_C, HD)
    state = state_conv.reshape(DEPTH * db, CONV_W - 1, BRANCH_W)

    xp = x_prompt.reshape(bp * seq, D_MODEL)
    xs = x_sample.reshape(db * t_new, D_MODEL)
    mem2d = mem_prompt.reshape(bp * N_MEM, D_MODEL)
    m_s = db * t_new
    outs = [[] for _ in range(11)]
    for l in range(DEPTH):
        z, lf, k_p, v_p, c_rep = _in_proj(xp, w_main, wf, bfb[l], l, tm=1024, tiles_per_seq=seq // 1024,
                                          emit_c=True, z_dtype=BF16)
        mkv = _mm(mem2d, wmkv[l], tm=bp * N_MEM, tn=2 * BRANCH_W)
        (oa,) = _branch_a(z, ln_v_g[l], ln_v_b[l], w_s[l], bsb_p[l], mask_p, tm=512, emit_v=False)
        ob, nconv = _branch_b(z, wdw[l], b_dw[l], ln_c_g[l], ln_c_b[l], wpw[l], b_pw[l], batch=bp, tm=512)
        oc = _fox_prompt(z, c_rep, batch=bp, tq=512)
        om = _mem_attn_prompt(z, mkv, batch=bp, tq=512)
        hm = _merge((oa, ob, oc, om), z, wb, l, tm=256)
        xp = _out_ln(hm, wo, xp, ln_g[l], ln_b[l], l, tm=256)
        outs[0].append(k_p.reshape(bp, seq, H_C, HD))
        outs[1].append(v_p.reshape(bp, seq, H_C, HD))
        outs[2].append(lf[:, :H_C].reshape(bp, seq, H_C))
        outs[3].append(nconv)
        outs[4].append(mkv[:, :BRANCH_W].reshape(bp, N_MEM, H_C, HD))
        outs[5].append(mkv[:, BRANCH_W:].reshape(bp, N_MEM, H_C, HD))

        zs, lfs, k_s, v_s = _in_proj(xs, w_main, wf, bfb[l], l, tm=m_s, tiles_per_seq=1, emit_c=False,
                                     z_dtype=F32)
        oa_s, v_rows = _branch_a(zs, ln_v_g[l], ln_v_b[l], ws_s[l], bsb_s[l], mask_s, tm=m_s, emit_v=True)
        ob_s, h_glu = _branch_b_sample(zs.reshape(db, t_new, Z_COLS), state, l, wdw[l], b_dw[l], ln_c_g[l],
                                       ln_c_b[l], wpw[l], b_pw[l])
        heads = lambda c: zs[:, c * LANE:c * LANE + BRANCH_W].reshape(db, t_new * H_C, HD)
        pad_new = lambda a: jnp.pad(a.reshape(db, t_new * H_C, HD), ((0, 0), (0, LANE - t_new * H_C), (0, 0)))
        lfn = lfs[:, :H_C].reshape(db, 1, t_new * H_C)
        lfn = jnp.pad(lfn, ((0, 0), (0, 7), (0, LANE - t_new * H_C)))
        oc_s = _fox_sample(page_table + l * n_pool, heads(QC), heads(GC), pad_new(k_s), pad_new(v_s),
                           lfn, mn, kflat, vflat, wt3, n_new=t_new)
        om_s = _mem_attn_sample(heads(QM), heads(GM), memk, memv, l)
        flat = lambda a: a.reshape(m_s, BRANCH_W)
        hm_s = _merge((oa_s, flat(ob_s), flat(oc_s), flat(om_s)), zs, wb, l, tm=m_s)
        xs = _out_ln(hm_s, wo, xs, ln_g[l], ln_b[l], l, tm=m_s)
        outs[6].append(k_s.reshape(db, t_new, H_C, HD))
        outs[7].append(v_s.reshape(db, t_new, H_C, HD))
        outs[8].append(lfs[:, :H_C].reshape(db, t_new, H_C))
        outs[9].append(jnp.concatenate([state_conv[l][:, t_new:], h_glu], axis=1))
        outs[10].append(v_rows.reshape(db, t_new, BRANCH_W))

    return (xp.reshape(bp, seq, D_MODEL), xs.reshape(db, t_new, D_MODEL)) + tuple(jnp.stack(o) for o in outs)
```

```python
import functools
import math

import jax
import jax.numpy as jnp
from jax import lax
from jax.experimental import pallas as pl
from jax.experimental.pallas import tpu as pltpu

F32 = jnp.float32
BF16 = jnp.bfloat16

D_MODEL = 2048
DEPTH = 2
BRANCH_W = 512
N_BRANCH = 4
CHUNK = 128
A_GROUPS = 4
CONV_W = 31
H_C = 4
HD = 128
N_MEM = 256
PAGE_SIZE = 128
LN_EPS = 1e-5
ALPHA = (2 * DEPTH) ** 0.25
ATT_SCALE = HD ** -0.5
LOG2E = math.log2(math.e)
NEG = -1e30

LANE = 128
SUBLANES = 8
PAGE_ROWS = PAGE_SIZE * H_C
F_COL = 9 * BRANCH_W
Z_COLS = 28 * BRANCH_W
QC, KC, VC, GC, QM, GM = 24, 28, 32, 36, 40, 44
GATE_BLK = 12
PAGES_PER_STEP = 16
VMEM_LIMIT = 48 * 1024 * 1024


def _cparams(n_axes, vmem=VMEM_LIMIT):
    return pltpu.CompilerParams(dimension_semantics=("arbitrary",) * n_axes, vmem_limit_bytes=vmem)


def _ln(x, g, b):
    mu = jnp.mean(x, axis=-1, keepdims=True)
    xc = x - mu
    var = jnp.mean(xc * xc, axis=-1, keepdims=True)
    return xc * lax.rsqrt(var + LN_EPS) * g + b


def _silu(x):
    return x * jax.nn.sigmoid(x)


def _log_sigmoid(x):
    return jnp.minimum(x, 0.0) - jnp.log1p(jnp.exp(-jnp.abs(x)))


def _dot(a, b):
    return jnp.dot(a, b, preferred_element_type=F32)


def _dot_nt(a, b):
    return lax.dot_general(a, b, (((1,), (1,)), ((), ())), preferred_element_type=F32)


def _split3(x):
    hi = x.astype(BF16)
    r = x - hi.astype(F32)
    mid = r.astype(BF16)
    lo = (r - mid.astype(F32)).astype(BF16)
    return hi, mid, lo


def _dot_exact01(x, m01):
    hi, mid, lo = _split3(x)
    return _dot(hi, m01) + _dot(mid, m01) + _dot(lo, m01)


def _dot_exact01_left(m01, x):
    hi, mid, lo = _split3(x)
    return _dot(m01, hi) + _dot(m01, mid) + _dot(m01, lo)


def _prep_w_kernel(a_ref, f_ref, o_ref, wf_ref):
    j = pl.program_id(0)
    for l in range(DEPTH):
        o_ref[l] = a_ref[:, l, :].T.astype(BF16)

    @pl.when(j == 0)
    def _():
        lane = lax.broadcasted_iota(jnp.int32, (D_MODEL, LANE), 1)
        for l in range(DEPTH):
            wf_ref[l] = jnp.where(lane < H_C, f_ref[:, l, :].T, 0.0).astype(BF16)


def _prep_w(w_in):
    wt = jnp.transpose(w_in, (2, 0, 1))
    elems = lambda rows: (pl.Element(rows), pl.Element(DEPTH), pl.Element(D_MODEL))
    tc = BRANCH_W
    return pl.pallas_call(
        _prep_w_kernel,
        grid=(Z_COLS // tc,),
        in_specs=[pl.BlockSpec(elems(tc), lambda j: (j * tc + jnp.where(j >= F_COL // tc, H_C, 0), 0, 0)),
                  pl.BlockSpec(elems(LANE), lambda j: (F_COL, 0, 0))],
        out_specs=[pl.BlockSpec((DEPTH, D_MODEL, tc), lambda j: (0, 0, j)),
                   pl.BlockSpec((DEPTH, D_MODEL, LANE), lambda j: (0, 0, 0))],
        out_shape=[jax.ShapeDtypeStruct((DEPTH, D_MODEL, Z_COLS), BF16),
                   jax.ShapeDtypeStruct((DEPTH, D_MODEL, LANE), BF16)],
        compiler_params=_cparams(1),
        name="prep_w",
    )(wt, wt)


def _in_proj_kernel(x_ref, w_ref, wf_ref, bf_ref, z_ref, lf_ref, k_ref, v_ref, *rest, tm, tn, tiles_per_seq, emit_c):
    if emit_c:
        c_ref, xb_ref, carry_ref = rest
    else:
        (xb_ref,) = rest
    i = pl.program_id(0)
    j = pl.program_id(1)

    @pl.when(j == 0)
    def _():
        xb = x_ref[...].astype(BF16)
        xb_ref[...] = xb
        lf = _log_sigmoid(_dot(xb, wf_ref[...]) + bf_ref[...])
        lf_ref[...] = lf
        if emit_c:
            @pl.when(lax.rem(i, tiles_per_seq) == 0)
            def _():
                carry_ref[...] = jnp.zeros_like(carry_ref)

            row = lax.broadcasted_iota(jnp.int32, (LANE, LANE), 0)
            col = lax.broadcasted_iota(jnp.int32, (LANE, LANE), 1)
            lower = jnp.where(col <= row, 1.0, 0.0).astype(BF16)
            carry = carry_ref[...]
            for r in range(tm // LANE):
                rs = slice(r * LANE, (r + 1) * LANE)
                cblk = _dot_exact01_left(lower, lf[rs, :]) + carry
                carry = cblk[LANE - 1:LANE, :]
                c2 = cblk * LOG2E
                for h in range(H_C):
                    c_ref[h, rs, :] = jnp.broadcast_to(c2[:, h:h + 1], (LANE, LANE))
            carry_ref[...] = carry

    acc = _dot(xb_ref[...], w_ref[...])
    z_ref[...] = acc.astype(z_ref.dtype)

    def heads_out(o_ref, col0):
        for h in range(H_C):
            o_ref[:, h, :] = acc[:, col0 + h * HD:col0 + (h + 1) * HD]

    @pl.when(j == KC * LANE // tn)
    def _():
        heads_out(k_ref, KC * LANE % tn)

    @pl.when(j == VC * LANE // tn)
    def _():
        heads_out(v_ref, VC * LANE % tn)


def _in_proj(x, w_main, wf, bfb, layer, *, tm, tiles_per_seq, emit_c, z_dtype):
    m = x.shape[0]
    tn = 2 * BRANCH_W
    kern = functools.partial(_in_proj_kernel, tm=tm, tn=tn, tiles_per_seq=tiles_per_seq, emit_c=emit_c)
    heads = pl.BlockSpec((tm, H_C, HD), lambda i, j: (i, 0, 0))
    out_specs = [pl.BlockSpec((tm, tn), lambda i, j: (i, j)),
                 pl.BlockSpec((tm, LANE), lambda i, j: (i, 0)), heads, heads]
    out_shape = [jax.ShapeDtypeStruct((m, Z_COLS), z_dtype), jax.ShapeDtypeStruct((m, LANE), F32),
                 jax.ShapeDtypeStruct((m, H_C, HD), F32), jax.ShapeDtypeStruct((m, H_C, HD), F32)]
    scratch = [pltpu.VMEM((tm, D_MODEL), BF16)]
    if emit_c:
        out_specs.append(pl.BlockSpec((H_C, tm, LANE), lambda i, j: (0, i, 0)))
        out_shape.append(jax.ShapeDtypeStruct((H_C, m, LANE), F32))
        scratch.append(pltpu.VMEM((1, LANE), F32))
    return pl.pallas_call(
        kern,
        grid=(m // tm, Z_COLS // tn),
        in_specs=[
            pl.BlockSpec((tm, D_MODEL), lambda i, j: (i, 0), pipeline_mode=pl.Buffered(1)),
            pl.BlockSpec((None, D_MODEL, tn), lambda i, j: (layer, 0, j)),
            pl.BlockSpec((None, D_MODEL, LANE), lambda i, j: (layer, 0, 0)),
            pl.BlockSpec((1, LANE), lambda i, j: (0, 0)),
        ],
        out_specs=out_specs,
        out_shape=out_shape,
        scratch_shapes=scratch,
        compiler_params=_cparams(2),
        name="in_proj",
    )(x, w_main, wf, bfb)


def _mm_kernel(x_ref, w_ref, o_ref):
    o_ref[...] = _dot(x_ref[...].astype(BF16), w_ref[...])


def _mm(x, w, *, tm, tn):
    m, k = x.shape
    n = w.shape[1]
    return pl.pallas_call(
        _mm_kernel,
        grid=(m // tm, n // tn),
        in_specs=[pl.BlockSpec((tm, k), lambda i, j: (i, 0)), pl.BlockSpec((k, tn), lambda i, j: (0, j))],
        out_specs=pl.BlockSpec((tm, tn), lambda i, j: (i, j)),
        out_shape=jax.ShapeDtypeStruct((m, n), F32),
        compiler_params=_cparams(2),
        name="mem_kv_proj",
    )(x, w)


def _branch_a_kernel(u_ref, v_ref, g_ref, lng_ref, lnb_ref, ws_ref, bsb_ref, mask_ref, o_ref, *rest,
                     tm, emit_v):
    u = jax.nn.gelu(u_ref[...].astype(F32))
    v = _ln(jax.nn.gelu(v_ref[...].astype(F32)), lng_ref[...], lnb_ref[...])
    if emit_v:
        rest[0][...] = v
    gate = _silu(g_ref[...].astype(F32))
    keep = mask_ref[...] > 0.0
    for g in range(A_GROUPS):
        wg = jnp.where(keep, ws_ref[g], 0.0).astype(BF16)
        cs = slice(g * LANE, (g + 1) * LANE)
        for c in range(tm // CHUNK):
            rs = slice(c * CHUNK, (c + 1) * CHUNK)
            s = _dot(wg, v[rs, cs].astype(BF16)) + bsb_ref[g]
            o_ref[rs, cs] = (u[rs, cs] * s * gate[rs, cs]).astype(o_ref.dtype)


def _branch_a(z, lng, lnb, ws, bsb, mask, *, tm, emit_v):
    m = z.shape[0]
    blk = lambda c: pl.BlockSpec((tm, BRANCH_W), lambda i, c=c: (i, c))
    vec = pl.BlockSpec((1, BRANCH_W), lambda i: (0, 0))
    cube = pl.BlockSpec((A_GROUPS, CHUNK, CHUNK), lambda i: (0, 0, 0))
    out_specs = [pl.BlockSpec((tm, BRANCH_W), lambda i: (i, 0))]
    out_shape = [jax.ShapeDtypeStruct((m, BRANCH_W), BF16)]
    if emit_v:
        out_specs.append(pl.BlockSpec((tm, BRANCH_W), lambda i: (i, 0)))
        out_shape.append(jax.ShapeDtypeStruct((m, BRANCH_W), F32))
    return pl.pallas_call(
        functools.partial(_branch_a_kernel, tm=tm, emit_v=emit_v),
        grid=(m // tm,),
        in_specs=[blk(0), blk(1), blk(2), vec, vec, cube, cube,
                  pl.BlockSpec((CHUNK, CHUNK), lambda i: (0, 0))],
        out_specs=out_specs,
        out_shape=out_shape,
        compiler_params=_cparams(1),
        name="branch_a",
    )(z, z, z, lng, lnb, ws, bsb, mask)


CONV_PAD = 32


def _conv_tail(y, bdw_ref, lng_ref, lnb_ref, wpw_ref, bpw_ref, gate):
    y = _silu(_ln(y + bdw_ref[...], lng_ref[...], lnb_ref[...]))
    return (_dot(y.astype(BF16), wpw_ref[...]) + bpw_ref[...]) * _silu(gate)


def _branch_b_kernel(a_ref, b_ref, g_ref, wdw_ref, bdw_ref, lng_ref, lnb_ref, wpw_ref, bpw_ref,
                     o_ref, nc_ref, hp_ref, sw_ref, *, tm):
    t = pl.program_id(1)

    @pl.when(t == 0)
    def _():
        hp_ref[0:CONV_PAD, :] = jnp.zeros((CONV_PAD, BRANCH_W), F32)

    hp_ref[CONV_PAD:CONV_PAD + tm, :] = a_ref[...].astype(F32) * jax.nn.sigmoid(b_ref[...].astype(F32))
    off = CONV_PAD - (CONV_W - 1)
    acc = None
    for r in range(SUBLANES):
        taps = [j for j in range(CONV_W) if (off + j) % SUBLANES == r]
        rows = tm + taps[-1] - taps[0]
        if r:
            sw_ref[0:rows, :] = hp_ref[pl.ds(off + taps[0], rows), :]
        for j in taps:
            src = sw_ref[j - taps[0]:j - taps[0] + tm, :] if r else hp_ref[off + j:off + j + tm, :]
            term = src * wdw_ref[j:j + 1, :]
            acc = term if acc is None else acc + term
    gate = g_ref[...].astype(F32)
    o_ref[...] = _conv_tail(acc, bdw_ref, lng_ref, lnb_ref, wpw_ref, bpw_ref, gate).astype(o_ref.dtype)
    nc_ref[0] = hp_ref[pl.ds(CONV_PAD + tm - (CONV_W - 1), CONV_W - 1), :]
    hp_ref[0:CONV_PAD, :] = hp_ref[tm:tm + CONV_PAD, :]


def _branch_b(z, wdw, bdw, lng, lnb, wpw, bpw, *, batch, tm):
    m = z.shape[0]
    nt = m // batch // tm
    blk = lambda c: pl.BlockSpec((tm, BRANCH_W), lambda b, t, c=c: (b * nt + t, c))
    vec = pl.BlockSpec((1, BRANCH_W), lambda b, t: (0, 0))
    return pl.pallas_call(
        functools.partial(_branch_b_kernel, tm=tm),
        grid=(batch, nt),
        in_specs=[blk(3), blk(4), blk(5),
                  pl.BlockSpec((CONV_PAD, BRANCH_W), lambda b, t: (0, 0)), vec, vec, vec,
                  pl.BlockSpec((BRANCH_W, BRANCH_W), lambda b, t: (0, 0)), vec],
        out_specs=[pl.BlockSpec((tm, BRANCH_W), lambda b, t: (b * nt + t, 0)),
                   pl.BlockSpec((1, CONV_W - 1, BRANCH_W), lambda b, t: (b, 0, 0))],
        out_shape=[jax.ShapeDtypeStruct((m, BRANCH_W), BF16),
                   jax.ShapeDtypeStruct((batch, CONV_W - 1, BRANCH_W), F32)],
        scratch_shapes=[pltpu.VMEM((tm + CONV_PAD, BRANCH_W), F32)] * 2,
        compiler_params=_cparams(2),
        name="branch_b",
    )(z, z, z, wdw, bdw, lng, lnb, wpw, bpw)


SEQS_PER_STEP = 8


def _branch_b_sample_kernel(a_ref, b_ref, g_ref, st_ref, wdw_ref, bdw_ref, lng_ref, lnb_ref, wpw_ref,
                            bpw_ref, o_ref, h_ref, hp_ref, y_ref, *, t_new):
    n_hist = CONV_W - 1
    for s in range(SEQS_PER_STEP):
        h = a_ref[s] * jax.nn.sigmoid(b_ref[s])
        h_ref[s] = h
        hp_ref[s, CONV_PAD:CONV_PAD + SUBLANES, :] = jnp.zeros((SUBLANES, BRANCH_W), F32)
        hp_ref[s, 0:n_hist, :] = st_ref[s]
        hp_ref[s, n_hist:n_hist + t_new, :] = h
        acc = hp_ref[s, pl.ds(0, SUBLANES), :] * wdw_ref[0:1, :]
        for j in range(1, CONV_W):
            acc = acc + hp_ref[s, pl.ds(j, SUBLANES), :] * wdw_ref[j:j + 1, :]
        y_ref[s * SUBLANES:(s + 1) * SUBLANES, :] = _silu(_ln(acc + bdw_ref[...], lng_ref[...], lnb_ref[...]))
    out = _dot(y_ref[...].astype(BF16), wpw_ref[...]) + bpw_ref[...]
    for s in range(SEQS_PER_STEP):
        o_ref[s] = out[s * SUBLANES:s * SUBLANES + t_new, :] * _silu(g_ref[s])


def _branch_b_sample(z3, state, layer, wdw, bdw, lng, lnb, wpw, bpw):
    nb, t_new, _ = z3.shape
    ns = SEQS_PER_STEP
    blk = lambda c: pl.BlockSpec((ns, t_new, BRANCH_W), lambda b, c=c: (b, 0, c))
    vec = pl.BlockSpec((1, BRANCH_W), lambda b: (0, 0))
    row = pl.BlockSpec((ns, t_new, BRANCH_W), lambda b: (b, 0, 0))
    return pl.pallas_call(
        functools.partial(_branch_b_sample_kernel, t_new=t_new),
        grid=(nb // ns,),
        in_specs=[blk(3), blk(4), blk(5),
                  pl.BlockSpec((ns, CONV_W - 1, BRANCH_W), lambda b: (layer * (nb // ns) + b, 0, 0)),
                  pl.BlockSpec((CONV_PAD, BRANCH_W), lambda b: (0, 0)), vec, vec, vec,
                  pl.BlockSpec((BRANCH_W, BRANCH_W), lambda b: (0, 0)), vec],
        out_specs=[row, row],
        out_shape=[jax.ShapeDtypeStruct((nb, t_new, BRANCH_W), F32)] * 2,
        scratch_shapes=[pltpu.VMEM((ns, CONV_PAD + SUBLANES, BRANCH_W), F32),
                        pltpu.VMEM((ns * SUBLANES, BRANCH_W), F32)],
        compiler_params=_cparams(1),
        name="branch_b_sample",
    )(z3, z3, z3, state, wdw, bdw, lng, lnb, wpw, bpw)


def _fox_kernel(q_ref, k_ref, v_ref, c_ref, g_ref, o_ref, vt_ref, s0_ref, s1_ref, m_ref, l_ref, acc_ref, *,
                tq, nk):
    qi = pl.program_id(2)

    @pl.when(qi == 0)
    def _():
        for j in range(nk):
            vt_ref[j] = v_ref[j * tq:(j + 1) * tq, :].astype(F32).T.astype(BF16)

    m_ref[...] = jnp.full_like(m_ref, NEG)
    l_ref[...] = jnp.zeros_like(l_ref)
    acc_ref[...] = jnp.zeros_like(acc_ref)
    qs = (q_ref[...].astype(F32) * (ATT_SCALE * LOG2E)).astype(BF16)

    def rows(kj):
        return pl.ds(pl.multiple_of(kj * tq, tq), tq)

    def scores(kj, s_ref):
        s_ref[...] = _dot_nt(k_ref[rows(kj), :].astype(BF16), qs)

    def softmax_pv(kj, s_ref, diagonal):
        c = c_ref[0, rows(kj), :]
        t = s_ref[...] - jnp.concatenate([c] * (tq // LANE), axis=1)
        if diagonal:
            key = lax.broadcasted_iota(jnp.int32, (tq, tq), 0)
            qry = lax.broadcasted_iota(jnp.int32, (tq, tq), 1)
            t = jnp.where(key <= qry, t, NEG)
        m_prev = m_ref[...]
        m_new = jnp.maximum(m_prev, jnp.max(t, axis=0, keepdims=True))
        alpha = jnp.exp2(m_prev - m_new)
        p = jnp.exp2(t - m_new)
        l_ref[...] = alpha * l_ref[...] + jnp.sum(p, axis=0, keepdims=True)
        acc_ref[...] = alpha * acc_ref[...] + _dot(vt_ref[kj], p.astype(BF16))
        m_ref[...] = m_new

    scores(0, s0_ref)

    def pair(p, carry):
        kj = 2 * p
        scores(kj + 1, s1_ref)
        softmax_pv(kj, s0_ref, False)
        scores(kj + 2, s0_ref)
        softmax_pv(kj + 1, s1_ref, False)
        return carry

    lax.fori_loop(0, lax.shift_right_logical(qi, 1), pair, 0)

    @pl.when((qi & 1) == 0)
    def _():
        softmax_pv(qi, s0_ref, True)

    @pl.when((qi & 1) == 1)
    def _():
        scores(qi, s1_ref)
        softmax_pv(qi - 1, s0_ref, False)
        softmax_pv(qi, s1_ref, True)

    o_ref[...] = ((acc_ref[...] / l_ref[...]).T * _silu(g_ref[...].astype(F32))).astype(o_ref.dtype)


def _fox_prompt(z, c_rep, *, batch, tq):
    m = z.shape[0]
    seq = m // batch
    nq = seq // tq
    qmap = lambda c: (lambda b, h, qi: (b * nq + qi, c + h))
    kvmap = lambda c: (lambda b, h, qi: (b, c + h))
    return pl.pallas_call(
        functools.partial(_fox_kernel, tq=tq, nk=nq),
        grid=(batch, H_C, nq),
        in_specs=[pl.BlockSpec((tq, HD), qmap(QC)),
                  pl.BlockSpec((seq, HD), kvmap(KC)),
                  pl.BlockSpec((seq, HD), kvmap(VC)),
                  pl.BlockSpec((1, seq, LANE), lambda b, h, qi: (h, b, 0)),
                  pl.BlockSpec((tq, HD), qmap(GC))],
        out_specs=pl.BlockSpec((tq, HD), lambda b, h, qi: (b * nq + qi, h)),
        out_shape=jax.ShapeDtypeStruct((m, BRANCH_W), BF16),
        scratch_shapes=[pltpu.VMEM((nq, HD, tq), BF16), pltpu.VMEM((tq, tq), F32), pltpu.VMEM((tq, tq), F32),
                        pltpu.VMEM((1, tq), F32), pltpu.VMEM((1, tq), F32), pltpu.VMEM((HD, tq), F32)],
        compiler_params=_cparams(3),
        name="fox_prompt",
    )(z, z, z, c_rep, z)


def _mem_attn_kernel(q_ref, k_ref, v_ref, g_ref, o_ref):
    for h in range(H_C):
        cs = slice(h * HD, (h + 1) * HD)
        s = _dot_nt(q_ref[:, cs].astype(BF16), k_ref[:, cs].astype(BF16)) * ATT_SCALE
        p = jnp.exp(s - jnp.max(s, axis=1, keepdims=True))
        o = _dot(p.astype(BF16), v_ref[:, cs].astype(BF16)) / jnp.sum(p, axis=1, keepdims=True)
        o_ref[:, cs] = (o * _silu(g_ref[:, cs].astype(F32))).astype(o_ref.dtype)


def _mem_attn_prompt(z, mkv, *, batch, tq):
    m = z.shape[0]
    nq = m // batch // tq
    wide = lambda c: pl.BlockSpec((tq, BRANCH_W), lambda b, qi, c=c: (b * nq + qi, c * LANE // BRANCH_W))
    return pl.pallas_call(
        _mem_attn_kernel,
        grid=(batch, nq),
        in_specs=[wide(QM),
                  pl.BlockSpec((N_MEM, BRANCH_W), lambda b, qi: (b, 0)),
                  pl.BlockSpec((N_MEM, BRANCH_W), lambda b, qi: (b, 1)),
                  wide(GM)],
        out_specs=pl.BlockSpec((tq, BRANCH_W), lambda b, qi: (b * nq + qi, 0)),
        out_shape=jax.ShapeDtypeStruct((m, BRANCH_W), BF16),
        compiler_params=_cparams(2),
        name="mem_attn_prompt",
    )(z, mkv, mkv, z)


def _head_match(rows, cols):
    r = lax.broadcasted_iota(jnp.int32, (rows, cols), 0)
    c = lax.broadcasted_iota(jnp.int32, (rows, cols), 1)
    return r, c, (r & (H_C - 1)) == (c & (H_C - 1))


def _mem_attn_sample_kernel(q_ref, g_ref, k_ref, v_ref, o_ref):
    _, _, same = _head_match(q_ref.shape[1], k_ref.shape[1])
    for i in range(SEQS_PER_STEP):
        s = _dot_nt(q_ref[i].astype(BF16), k_ref[i].astype(BF16)) * ATT_SCALE
        s = jnp.where(same, s, NEG)
        p = jnp.exp(s - jnp.max(s, axis=1, keepdims=True))
        o = _dot(p.astype(BF16), v_ref[i].astype(BF16)) / jnp.sum(p, axis=1, keepdims=True)
        o_ref[i] = o * _silu(g_ref[i])


def _mem_attn_sample(q16, g16, mk, mv, layer):
    nb, nr, _ = q16.shape
    nm = mk.shape[1]
    ns = SEQS_PER_STEP
    row = pl.BlockSpec((ns, nr, HD), lambda b: (b, 0, 0))
    mem = pl.BlockSpec((ns, nm, HD), lambda b: (layer * (nb // ns) + b, 0, 0))
    return pl.pallas_call(
        _mem_attn_sample_kernel,
        grid=(nb // ns,),
        in_specs=[row, row, mem, mem],
        out_specs=row,
        out_shape=jax.ShapeDtypeStruct((nb, nr, HD), F32),
        compiler_params=_cparams(1),
        name="mem_attn_sample",
    )(q16, g16, mk, mv)


def _logf_pages_kernel(x_ref, mc_ref, mt_ref, o_ref):
    x = x_ref[...]
    o_ref[:, 0, 0:PAGE_ROWS] = _dot_exact01(x, mc_ref[...])
    o_ref[:, 0, PAGE_ROWS:2 * PAGE_ROWS] = _dot_exact01(x, mt_ref[...])


def _logf_pages(lf_flat, mc, mt, *, tm):
    n = lf_flat.shape[0]
    mat = pl.BlockSpec((PAGE_ROWS, PAGE_ROWS), lambda i: (0, 0))
    return pl.pallas_call(
        _logf_pages_kernel,
        grid=(n // tm,),
        in_specs=[pl.BlockSpec((tm, PAGE_ROWS), lambda i: (i, 0)), mat, mat],
        out_specs=pl.BlockSpec((tm, 1, 2 * PAGE_ROWS), lambda i: (i, 0, 0)),
        out_shape=jax.ShapeDtypeStruct((n, 1, 2 * PAGE_ROWS), F32),
        compiler_params=_cparams(1),
        name="logf_pages",
    )(lf_flat, mc, mt)


def _softmax_update(state, s_list, v_list):
    m_prev, l_prev, acc_prev = state
    m_new = m_prev
    for s in s_list:
        m_new = jnp.maximum(m_new, jnp.max(s, axis=1, keepdims=True))
    alpha = jnp.exp(m_prev - m_new)
    l_new = alpha * l_prev
    acc = alpha * acc_prev
    for s, v in zip(s_list, v_list):
        p = jnp.exp(s - m_new)
        l_new = l_new + jnp.sum(p, axis=1, keepdims=True)
        acc = acc + _dot(p.astype(BF16), v)
    return m_new, l_new, acc


def _fox_sample_kernel(pt_ref, q_ref, g_ref, kn_ref, vn_ref, lfn_ref, mn_ref, k_hbm, v_hbm, wt_hbm, o_ref,
                       kbuf, vbuf, wtbuf, sem, m_ref, l_ref, acc_ref, carry_ref, *, steps_per_seq, n_new):
    np_ = PAGES_PER_STEP
    seq = pl.program_id(0)
    sub = pl.program_id(1)
    step = seq * steps_per_seq + sub
    n_steps = pl.num_programs(0) * steps_per_seq
    nr = q_ref.shape[1]

    def slot_copies(slot, page_of):
        copies = []
        for p in range(np_):
            page = page_of(p)
            copies.append(pltpu.make_async_copy(k_hbm.at[page], kbuf.at[slot, p], sem.at[slot]))
            copies.append(pltpu.make_async_copy(v_hbm.at[page], vbuf.at[slot, p], sem.at[slot]))
            copies.append(pltpu.make_async_copy(wt_hbm.at[page], wtbuf.at[slot, p], sem.at[slot]))
        return copies

    def start_step(for_step, slot):
        b = lax.shift_right_logical(for_step, steps_per_seq.bit_length() - 1)
        first_page = (for_step & (steps_per_seq - 1)) * np_
        for c in slot_copies(slot, lambda p: pt_ref[b, first_page + p]):
            c.start()

    slot = step & 1

    @pl.when(step == 0)
    def _():
        start_step(step, slot)

    @pl.when(step + 1 < n_steps)
    def _():
        start_step(step + 1, 1 - slot)

    for c in slot_copies(slot, lambda p: 0):
        c.wait()

    first = sub == 0
    q = q_ref[0].astype(BF16)
    state = (jnp.where(first, NEG, m_ref[...]), jnp.where(first, 0.0, l_ref[...]),
             jnp.where(first, 0.0, acc_ref[...]))
    carry = jnp.where(first, 0.0, carry_ref[...])
    _, _, same = _head_match(nr, PAGE_ROWS)
    s_list, v_list = [], []
    for p in range(np_):
        wt = wtbuf[slot, p]
        ck = carry + wt[:, 0:PAGE_ROWS]
        carry = carry + wt[:, PAGE_ROWS:2 * PAGE_ROWS]
        s = _dot_nt(q, kbuf[slot, p].astype(BF16)) * ATT_SCALE - ck
        s_list.append(jnp.where(same, s, NEG))
        v_list.append(vbuf[slot, p].astype(BF16))
    state = _softmax_update(state, s_list, v_list)
    m_ref[...], l_ref[...], acc_ref[...] = state
    carry_ref[...] = carry

    @pl.when(sub == steps_per_seq - 1)
    def _():
        r, c, same_n = _head_match(nr, LANE)
        cn = carry[:, 0:LANE] + _dot_exact01(lfn_ref[0], mn_ref[...])[0:1, :]
        s = _dot_nt(q, kn_ref[0].astype(BF16)) * ATT_SCALE - cn
        ok = same_n & (c < n_new * H_C) & ((c >> 2) <= (r >> 2))
        _, l_fin, acc_fin = _softmax_update(state, [jnp.where(ok, s, NEG)], [vn_ref[0].astype(BF16)])
        o_ref[0] = acc_fin / l_fin * _silu(g_ref[0])


def _fox_sample(pt, q16, g16, kn, vn, lfn, mn, kflat, vflat, wt3, *, n_new):
    nb, nr, _ = q16.shape
    n_pages = pt.shape[1]
    np_ = PAGES_PER_STEP
    steps_per_seq = n_pages // np_
    assert steps_per_seq * np_ == n_pages and steps_per_seq & (steps_per_seq - 1) == 0
    row = pl.BlockSpec((1, nr, HD), lambda b, s, pt: (b, 0, 0))
    new = pl.BlockSpec((1, LANE, HD), lambda b, s, pt: (b, 0, 0))
    pool = pl.BlockSpec(memory_space=pl.ANY)
    grid_spec = pltpu.PrefetchScalarGridSpec(
        num_scalar_prefetch=1,
        grid=(nb, steps_per_seq),
        in_specs=[row, row, new, new,
                  pl.BlockSpec((1, 8, LANE), lambda b, s, pt: (b, 0, 0)),
                  pl.BlockSpec((LANE, LANE), lambda b, s, pt: (0, 0)),
                  pool, pool, pool],
        out_specs=pl.BlockSpec((1, nr, HD), lambda b, s, pt: (b, 0, 0)),
        scratch_shapes=[pltpu.VMEM((2, np_, PAGE_ROWS, HD), F32), pltpu.VMEM((2, np_, PAGE_ROWS, HD), F32),
                        pltpu.VMEM((2, np_, 1, 2 * PAGE_ROWS), F32), pltpu.SemaphoreType.DMA((2,)),
                        pltpu.VMEM((nr, 1), F32), pltpu.VMEM((nr, 1), F32), pltpu.VMEM((nr, HD), F32),
                        pltpu.VMEM((1, PAGE_ROWS), F32)],
    )
    return pl.pallas_call(
        functools.partial(_fox_sample_kernel, steps_per_seq=steps_per_seq, n_new=n_new),
        grid_spec=grid_spec,
        out_shape=jax.ShapeDtypeStruct((nb, nr, HD), F32),
        compiler_params=_cparams(2),
        name="fox_sample",
    )(pt, q16, g16, kn, vn, lfn, mn, kflat, vflat, wt3)


def _merge_out_kernel(*refs, tm):
    o_refs, wb_ref, gate_refs = refs[0:4], refs[4], refs[5:9]
    wo_ref, x_ref, g_ref, b_ref, y_ref, h_ref = refs[9:15]
    outs = [o_refs[br][...].astype(BF16) for br in range(N_BRANCH)]
    for jc in range(D_MODEL // BRANCH_W):
        cs = slice(jc * BRANCH_W, (jc + 1) * BRANCH_W)
        acc = None
        for br in range(N_BRANCH):
            gate = 0.5 * jnp.tanh(0.5 * gate_refs[br][:, cs].astype(F32)) + 0.5
            term = gate * _dot(outs[br], wb_ref[br, :, cs])
            acc = term if acc is None else acc + term
        h_ref[:, cs] = acc.astype(h_ref.dtype)
    half = max(tm // 2, SUBLANES)
    for s in range(tm // half):
        rs = slice(s * half, (s + 1) * half)
        y = ALPHA * x_ref[rs, :] + _dot(h_ref[rs, :], wo_ref[...])
        y_ref[rs, :] = _ln(y, g_ref[...], b_ref[...])


def _merge_out(outs, z, wb, wo, x, lng, lnb, layer, *, tm):
    m = z.shape[0]
    o_spec = pl.BlockSpec((tm, BRANCH_W), lambda i: (i, 0))
    gate0 = GATE_BLK * BRANCH_W // D_MODEL
    gate = lambda br: pl.BlockSpec((tm, D_MODEL), lambda i, br=br: (i, gate0 + br))
    rows = pl.BlockSpec((tm, D_MODEL), lambda i: (i, 0))
    vec = pl.BlockSpec((1, D_MODEL), lambda i: (0, 0))
    once = pl.Buffered(1)
    return pl.pallas_call(
        functools.partial(_merge_out_kernel, tm=tm),
        grid=(m // tm,),
        in_specs=[o_spec] * 4
                 + [pl.BlockSpec((None, N_BRANCH, BRANCH_W, D_MODEL), lambda i: (layer, 0, 0, 0), pipeline_mode=once)]
                 + [gate(br) for br in range(N_BRANCH)]
                 + [pl.BlockSpec((None, D_MODEL, D_MODEL), lambda i: (layer, 0, 0), pipeline_mode=once),
                    rows, vec, vec],
        out_specs=rows,
        out_shape=jax.ShapeDtypeStruct((m, D_MODEL), F32),
        scratch_shapes=[pltpu.VMEM((tm, D_MODEL), BF16)],
        compiler_params=_cparams(1),
        name="merge_out",
    )(*outs, wb, z, z, z, z, wo, x, lng, lnb)


def _tok_head_matrices():
    i = jnp.arange(PAGE_ROWS)
    same = (i[:, None] % H_C) == (i[None, :] % H_C)
    mc = (same & (i[:, None] // H_C <= i[None, :] // H_C)).astype(BF16)
    mt = same.astype(BF16)
    return mc, mt


def kernel(x_prompt, x_sample, mem_prompt, cache_k, cache_v, cache_logf, cache_mem_k, cache_mem_v, state_conv,
           page_table, w_in, w_mem_k, w_mem_v, ln_v_g, ln_v_b, w_s, b_s, w_dw, b_dw, ln_c_g, ln_c_b, w_pw, b_pw,
           b_f, w_branch, w_out, ln_g, ln_b):
    bp, seq, _ = x_prompt.shape
    db, t_new, _ = x_sample.shape
    n_pool = cache_k.shape[1]

    w_main, wf = _prep_w(w_in)
    bfb = jnp.pad(b_f, ((0, 0), (0, LANE - H_C)))[:, None, :]
    wb = w_branch.astype(BF16)
    wo = w_out.astype(BF16)
    wpw = w_pw.astype(BF16)
    wmkv = jnp.concatenate([w_mem_k, w_mem_v], axis=2).astype(BF16)
    wdw = jnp.pad(w_dw, ((0, 0), (0, CONV_PAD - CONV_W), (0, 0)))
    vec = lambda a: a[:, None, :]
    ln_v_g, ln_v_b, b_dw, ln_c_g, ln_c_b, b_pw, ln_g, ln_b = map(
        vec, (ln_v_g, ln_v_b, b_dw, ln_c_g, ln_c_b, b_pw, ln_g, ln_b))

    idx = jnp.arange(CHUNK)
    mask_p = (idx[None, :] <= idx[:, None]).astype(F32)
    bsb_p = jnp.broadcast_to(b_s[:, :, :, None], (DEPTH, A_GROUPS, CHUNK, CHUNK))
    reps = CHUNK // t_new
    mask_s = ((idx[:, None] // t_new == idx[None, :] // t_new) & (idx[None, :] <= idx[:, None])).astype(F32)
    ws_s = jnp.tile(w_s[:, :, :t_new, :t_new], (1, 1, reps, reps))
    bsb_s = jnp.broadcast_to(jnp.tile(b_s[:, :, :t_new], (1, 1, reps))[:, :, :, None],
                             (DEPTH, A_GROUPS, CHUNK, CHUNK))

    kflat = cache_k.reshape(DEPTH * n_pool, PAGE_ROWS, HD)
    vflat = cache_v.reshape(DEPTH * n_pool, PAGE_ROWS, HD)
    mc, mt = _tok_head_matrices()
    wt3 = _logf_pages(cache_logf.reshape(DEPTH * n_pool, PAGE_ROWS), mc, mt, tm=512)
    mn = jnp.pad(mc[:t_new * H_C, :t_new * H_C], ((0, LANE - t_new * H_C),) * 2)
    memk = cache_mem_k.reshape(DEPTH * db, N_MEM * H_C, HD)
    memv = cache_mem_v.reshape(DEPTH * db, N_MEM * H_C, HD)
    state = state_conv.reshape(DEPTH * db, CONV_W - 1, BRANCH_W)

    xp = x_prompt.reshape(bp * seq, D_MODEL)
    xs = x_sample.reshape(db * t_new, D_MODEL)
    mem2d = mem_prompt.reshape(bp * N_MEM, D_MODEL)
    m_s = db * t_new
    outs = [[] for _ in range(11)]
    for l in range(DEPTH):
        z, lf, k_p, v_p, c_rep = _in_proj(xp, w_main, wf, bfb[l], l, tm=1024, tiles_per_seq=seq // 1024,
                                          emit_c=True, z_dtype=BF16)
        mkv = _mm(mem2d, wmkv[l], tm=bp * N_MEM, tn=2 * BRANCH_W)
        (oa,) = _branch_a(z, ln_v_g[l], ln_v_b[l], w_s[l], bsb_p[l], mask_p, tm=512, emit_v=False)
        ob, nconv = _branch_b(z, wdw[l], b_dw[l], ln_c_g[l], ln_c_b[l], wpw[l], b_pw[l], batch=bp, tm=512)
        oc = _fox_prompt(z, c_rep, batch=bp, tq=512)
        om = _mem_attn_prompt(z, mkv, batch=bp, tq=512)
        xp = _merge_out((oa, ob, oc, om), z, wb, wo, xp, ln_g[l], ln_b[l], l, tm=256)
        outs[0].append(k_p.reshape(bp, seq, H_C, HD))
        outs[1].append(v_p.reshape(bp, seq, H_C, HD))
        outs[2].append(lf[:, :H_C].reshape(bp, seq, H_C))
        outs[3].append(nconv)
        outs[4].append(mkv[:, :BRANCH_W].reshape(bp, N_MEM, H_C, HD))
        outs[5].append(mkv[:, BRANCH_W:].reshape(bp, N_MEM, H_C, HD))

        zs, lfs, k_s, v_s = _in_proj(xs, w_main, wf, bfb[l], l, tm=m_s, tiles_per_seq=1, emit_c=False,
                                     z_dtype=F32)
        oa_s, v_rows = _branch_a(zs, ln_v_g[l], ln_v_b[l], ws_s[l], bsb_s[l], mask_s, tm=m_s, emit_v=True)
        ob_s, h_glu = _branch_b_sample(zs.reshape(db, t_new, Z_COLS), state, l, wdw[l], b_dw[l], ln_c_g[l],
                                       ln_c_b[l], wpw[l], b_pw[l])
        heads = lambda c: zs[:, c * LANE:c * LANE + BRANCH_W].reshape(db, t_new * H_C, HD)
        pad_new = lambda a: jnp.pad(a.reshape(db, t_new * H_C, HD), ((0, 0), (0, LANE - t_new * H_C), (0, 0)))
        lfn = lfs[:, :H_C].reshape(db, 1, t_new * H_C)
        lfn = jnp.pad(lfn, ((0, 0), (0, 7), (0, LANE - t_new * H_C)))
        oc_s = _fox_sample(page_table + l * n_pool, heads(QC), heads(GC), pad_new(k_s), pad_new(v_s),
                           lfn, mn, kflat, vflat, wt3, n_new=t_new)
        om_s = _mem_attn_sample(heads(QM), heads(GM), memk, memv, l)
        flat = lambda a: a.reshape(m_s, BRANCH_W)
        xs = _merge_out((oa_s, flat(ob_s), flat(oc_s), flat(om_s)), zs, wb, wo, xs, ln_g[l], ln_b[l], l, tm=m_s)
        outs[6].append(k_s.reshape(db, t_new, H_C, HD))
        outs[7].append(v_s.reshape(db, t_new, H_C, HD))
        outs[8].append(lfs[:, :H_C].reshape(db, t_new, H_C))
        outs[9].append(jnp.concatenate([state_conv[l][:, t_new:], h_glu], axis=1))
        outs[10].append(v_rows.reshape(db, t_new, BRANCH_W))

    return (xp.reshape(bp, seq, D_MODEL), xs.reshape(db, t_new, D_MODEL)) + tuple(jnp.stack(o) for o in outs)
```

```python
import functools
import math

import jax
import jax.numpy as jnp
from jax import lax
from jax.experimental import pallas as pl
from jax.experimental.pallas import tpu as pltpu

F32 = jnp.float32
BF16 = jnp.bfloat16

D_MODEL = 2048
DEPTH = 2
BRANCH_W = 512
N_BRANCH = 4
CHUNK = 128
A_GROUPS = 4
CONV_W = 31
H_C = 4
HD = 128
N_MEM = 256
PAGE_SIZE = 128
LN_EPS = 1e-5
ALPHA = (2 * DEPTH) ** 0.25
ATT_SCALE = HD ** -0.5
LOG2E = math.log2(math.e)
NEG = -1e30

LANE = 128
SUBLANES = 8
PAGE_ROWS = PAGE_SIZE * H_C
F_COL = 9 * BRANCH_W
Z_COLS = 28 * BRANCH_W
QC, KC, VC, GC, QM, GM = 24, 28, 32, 36, 40, 44
GATE_BLK = 12
PAGES_PER_STEP = 16
VMEM_LIMIT = 48 * 1024 * 1024


def _cparams(n_axes, vmem=VMEM_LIMIT):
    return pltpu.CompilerParams(dimension_semantics=("arbitrary",) * n_axes, vmem_limit_bytes=vmem)


def _ln(x, g, b):
    mu = jnp.mean(x, axis=-1, keepdims=True)
    xc = x - mu
    var = jnp.mean(xc * xc, axis=-1, keepdims=True)
    return xc * lax.rsqrt(var + LN_EPS) * g + b


def _silu(x):
    return x * jax.nn.sigmoid(x)


def _log_sigmoid(x):
    return jnp.minimum(x, 0.0) - jnp.log1p(jnp.exp(-jnp.abs(x)))


def _dot(a, b):
    return jnp.dot(a, b, preferred_element_type=F32)


def _dot_nt(a, b):
    return lax.dot_general(a, b, (((1,), (1,)), ((), ())), preferred_element_type=F32)


def _split3(x):
    hi = x.astype(BF16)
    r = x - hi.astype(F32)
    mid = r.astype(BF16)
    lo = (r - mid.astype(F32)).astype(BF16)
    return hi, mid, lo


def _dot_exact01(x, m01):
    hi, mid, lo = _split3(x)
    return _dot(hi, m01) + _dot(mid, m01) + _dot(lo, m01)


def _dot_exact01_left(m01, x):
    hi, mid, lo = _split3(x)
    return _dot(m01, hi) + _dot(m01, mid) + _dot(m01, lo)


def _prep_w_kernel(a_ref, f_ref, o_ref, wf_ref):
    j = pl.program_id(0)
    for l in range(DEPTH):
        o_ref[l] = a_ref[:, l, :].T.astype(BF16)

    @pl.when(j == 0)
    def _():
        lane = lax.broadcasted_iota(jnp.int32, (D_MODEL, LANE), 1)
        for l in range(DEPTH):
            wf_ref[l] = jnp.where(lane < H_C, f_ref[:, l, :].T, 0.0).astype(BF16)


def _prep_w(w_in):
    wt = jnp.transpose(w_in, (2, 0, 1))
    elems = lambda rows: (pl.Element(rows), pl.Element(DEPTH), pl.Element(D_MODEL))
    tc = BRANCH_W
    return pl.pallas_call(
        _prep_w_kernel,
        grid=(Z_COLS // tc,),
        in_specs=[pl.BlockSpec(elems(tc), lambda j: (j * tc + jnp.where(j >= F_COL // tc, H_C, 0), 0, 0)),
                  pl.BlockSpec(elems(LANE), lambda j: (F_COL, 0, 0))],
        out_specs=[pl.BlockSpec((DEPTH, D_MODEL, tc), lambda j: (0, 0, j)),
                   pl.BlockSpec((DEPTH, D_MODEL, LANE), lambda j: (0, 0, 0))],
        out_shape=[jax.ShapeDtypeStruct((DEPTH, D_MODEL, Z_COLS), BF16),
                   jax.ShapeDtypeStruct((DEPTH, D_MODEL, LANE), BF16)],
        compiler_params=_cparams(1),
        name="prep_w",
    )(wt, wt)


def _in_proj_kernel(x_ref, w_ref, wf_ref, bf_ref, z_ref, lf_ref, k_ref, v_ref, *rest, tm, tn, tiles_per_seq, emit_c,
                    extra_work=None, extra_active=None):
    if emit_c:
        c_ref, xb_ref, carry_ref = rest
    else:
        (xb_ref,) = rest
    i = pl.program_id(0)
    j = pl.program_id(1)

    @pl.when(j == 0)
    def _():
        xb = x_ref[...].astype(BF16)
        xb_ref[...] = xb
        lf = _log_sigmoid(_dot(xb, wf_ref[...]) + bf_ref[...])
        lf_ref[...] = lf
        if emit_c:
            @pl.when(lax.rem(i, tiles_per_seq) == 0)
            def _():
                carry_ref[...] = jnp.zeros_like(carry_ref)

            row = lax.broadcasted_iota(jnp.int32, (LANE, LANE), 0)
            col = lax.broadcasted_iota(jnp.int32, (LANE, LANE), 1)
            lower = jnp.where(col <= row, 1.0, 0.0).astype(BF16)
            carry = carry_ref[...]
            for r in range(tm // LANE):
                rs = slice(r * LANE, (r + 1) * LANE)
                cblk = _dot_exact01_left(lower, lf[rs, :]) + carry
                carry = cblk[LANE - 1:LANE, :]
                c2 = cblk * LOG2E
                for h in range(H_C):
                    c_ref[h, rs, :] = jnp.broadcast_to(c2[:, h:h + 1], (LANE, LANE))
            carry_ref[...] = carry

    def main(extra_work):
        after_matmul = extra_work() if extra_work is not None else None
        acc = _dot(xb_ref[...], w_ref[...])
        z_ref[...] = acc.astype(z_ref.dtype)
        finish = after_matmul() if after_matmul is not None else None

        def heads_out(o_ref, col0):
            for h in range(H_C):
                o_ref[:, h, :] = acc[:, col0 + h * HD:col0 + (h + 1) * HD]

        @pl.when(j == KC * LANE // tn)
        def _():
            heads_out(k_ref, KC * LANE % tn)

        @pl.when(j == VC * LANE // tn)
        def _():
            heads_out(v_ref, VC * LANE % tn)

        if finish is not None:
            finish()

    if extra_work is None:
        main(None)
    else:
        @pl.when(extra_active)
        def _():
            main(extra_work)

        @pl.when(jnp.logical_not(extra_active))
        def _():
            main(None)


def _in_proj(x, w_main, wf, bfb, layer, *, tm, tiles_per_seq, emit_c, z_dtype):
    m = x.shape[0]
    tn = 2 * BRANCH_W
    kern = functools.partial(_in_proj_kernel, tm=tm, tn=tn, tiles_per_seq=tiles_per_seq, emit_c=emit_c)
    heads = pl.BlockSpec((tm, H_C, HD), lambda i, j: (i, 0, 0))
    out_specs = [pl.BlockSpec((tm, tn), lambda i, j: (i, j)),
                 pl.BlockSpec((tm, LANE), lambda i, j: (i, 0)), heads, heads]
    out_shape = [jax.ShapeDtypeStruct((m, Z_COLS), z_dtype), jax.ShapeDtypeStruct((m, LANE), F32),
                 jax.ShapeDtypeStruct((m, H_C, HD), F32), jax.ShapeDtypeStruct((m, H_C, HD), F32)]
    scratch = [pltpu.VMEM((tm, D_MODEL), BF16)]
    if emit_c:
        out_specs.append(pl.BlockSpec((H_C, tm, LANE), lambda i, j: (0, i, 0)))
        out_shape.append(jax.ShapeDtypeStruct((H_C, m, LANE), F32))
        scratch.append(pltpu.VMEM((1, LANE), F32))
    return pl.pallas_call(
        kern,
        grid=(m // tm, Z_COLS // tn),
        in_specs=[
            pl.BlockSpec((tm, D_MODEL), lambda i, j: (i, 0), pipeline_mode=pl.Buffered(1)),
            pl.BlockSpec((None, D_MODEL, tn), lambda i, j: (layer, 0, j)),
            pl.BlockSpec((None, D_MODEL, LANE), lambda i, j: (layer, 0, 0)),
            pl.BlockSpec((1, LANE), lambda i, j: (0, 0)),
        ],
        out_specs=out_specs,
        out_shape=out_shape,
        scratch_shapes=scratch,
        compiler_params=_cparams(2),
        name="in_proj",
    )(x, w_main, wf, bfb)


def _mm_kernel(x_ref, w_ref, o_ref):
    o_ref[...] = _dot(x_ref[...].astype(BF16), w_ref[...])


def _mm(x, w, *, tm, tn):
    m, k = x.shape
    n = w.shape[1]
    return pl.pallas_call(
        _mm_kernel,
        grid=(m // tm, n // tn),
        in_specs=[pl.BlockSpec((tm, k), lambda i, j: (i, 0)), pl.BlockSpec((k, tn), lambda i, j: (0, j))],
        out_specs=pl.BlockSpec((tm, tn), lambda i, j: (i, j)),
        out_shape=jax.ShapeDtypeStruct((m, n), F32),
        compiler_params=_cparams(2),
        name="mem_kv_proj",
    )(x, w)


def _branch_a_kernel(u_ref, v_ref, g_ref, lng_ref, lnb_ref, ws_ref, bsb_ref, mask_ref, o_ref, *rest,
                     tm, emit_v):
    u = jax.nn.gelu(u_ref[...].astype(F32))
    v = _ln(jax.nn.gelu(v_ref[...].astype(F32)), lng_ref[...], lnb_ref[...])
    if emit_v:
        rest[0][...] = v
    gate = _silu(g_ref[...].astype(F32))
    keep = mask_ref[...] > 0.0
    for g in range(A_GROUPS):
        wg = jnp.where(keep, ws_ref[g], 0.0).astype(BF16)
        cs = slice(g * LANE, (g + 1) * LANE)
        for c in range(tm // CHUNK):
            rs = slice(c * CHUNK, (c + 1) * CHUNK)
            s = _dot(wg, v[rs, cs].astype(BF16)) + bsb_ref[g]
            o_ref[rs, cs] = (u[rs, cs] * s * gate[rs, cs]).astype(o_ref.dtype)


def _branch_a(z, lng, lnb, ws, bsb, mask, *, tm, emit_v):
    m = z.shape[0]
    blk = lambda c: pl.BlockSpec((tm, BRANCH_W), lambda i, c=c: (i, c))
    vec = pl.BlockSpec((1, BRANCH_W), lambda i: (0, 0))
    cube = pl.BlockSpec((A_GROUPS, CHUNK, CHUNK), lambda i: (0, 0, 0))
    out_specs = [pl.BlockSpec((tm, BRANCH_W), lambda i: (i, 0))]
    out_shape = [jax.ShapeDtypeStruct((m, BRANCH_W), BF16)]
    if emit_v:
        out_specs.append(pl.BlockSpec((tm, BRANCH_W), lambda i: (i, 0)))
        out_shape.append(jax.ShapeDtypeStruct((m, BRANCH_W), F32))
    return pl.pallas_call(
        functools.partial(_branch_a_kernel, tm=tm, emit_v=emit_v),
        grid=(m // tm,),
        in_specs=[blk(0), blk(1), blk(2), vec, vec, cube, cube,
                  pl.BlockSpec((CHUNK, CHUNK), lambda i: (0, 0))],
        out_specs=out_specs,
        out_shape=out_shape,
        compiler_params=_cparams(1),
        name="branch_a",
    )(z, z, z, lng, lnb, ws, bsb, mask)


CONV_PAD = 32


def _conv_tail(y, bdw_ref, lng_ref, lnb_ref, wpw_ref, bpw_ref, gate):
    y = _silu(_ln(y + bdw_ref[...], lng_ref[...], lnb_ref[...]))
    return (_dot(y.astype(BF16), wpw_ref[...]) + bpw_ref[...]) * _silu(gate)


def _branch_b_kernel(a_ref, b_ref, g_ref, wdw_ref, bdw_ref, lng_ref, lnb_ref, wpw_ref, bpw_ref,
                     o_ref, nc_ref, hp_ref, sw_ref, *, tm):
    t = pl.program_id(1)

    @pl.when(t == 0)
    def _():
        hp_ref[0:CONV_PAD, :] = jnp.zeros((CONV_PAD, BRANCH_W), F32)

    hp_ref[CONV_PAD:CONV_PAD + tm, :] = a_ref[...].astype(F32) * jax.nn.sigmoid(b_ref[...].astype(F32))
    off = CONV_PAD - (CONV_W - 1)
    acc = None
    for r in range(SUBLANES):
        taps = [j for j in range(CONV_W) if (off + j) % SUBLANES == r]
        rows = tm + taps[-1] - taps[0]
        if r:
            sw_ref[0:rows, :] = hp_ref[pl.ds(off + taps[0], rows), :]
        for j in taps:
            src = sw_ref[j - taps[0]:j - taps[0] + tm, :] if r else hp_ref[off + j:off + j + tm, :]
            term = src * wdw_ref[j:j + 1, :]
            acc = term if acc is None else acc + term
    gate = g_ref[...].astype(F32)
    o_ref[...] = _conv_tail(acc, bdw_ref, lng_ref, lnb_ref, wpw_ref, bpw_ref, gate).astype(o_ref.dtype)
    nc_ref[0] = hp_ref[pl.ds(CONV_PAD + tm - (CONV_W - 1), CONV_W - 1), :]
    hp_ref[0:CONV_PAD, :] = hp_ref[tm:tm + CONV_PAD, :]


def _branch_b(z, wdw, bdw, lng, lnb, wpw, bpw, *, batch, tm):
    m = z.shape[0]
    nt = m // batch // tm
    blk = lambda c: pl.BlockSpec((tm, BRANCH_W), lambda b, t, c=c: (b * nt + t, c))
    vec = pl.BlockSpec((1, BRANCH_W), lambda b, t: (0, 0))
    return pl.pallas_call(
        functools.partial(_branch_b_kernel, tm=tm),
        grid=(batch, nt),
        in_specs=[blk(3), blk(4), blk(5),
                  pl.BlockSpec((CONV_PAD, BRANCH_W), lambda b, t: (0, 0)), vec, vec, vec,
                  pl.BlockSpec((BRANCH_W, BRANCH_W), lambda b, t: (0, 0)), vec],
        out_specs=[pl.BlockSpec((tm, BRANCH_W), lambda b, t: (b * nt + t, 0)),
                   pl.BlockSpec((1, CONV_W - 1, BRANCH_W), lambda b, t: (b, 0, 0))],
        out_shape=[jax.ShapeDtypeStruct((m, BRANCH_W), BF16),
                   jax.ShapeDtypeStruct((batch, CONV_W - 1, BRANCH_W), F32)],
        scratch_shapes=[pltpu.VMEM((tm + CONV_PAD, BRANCH_W), F32)] * 2,
        compiler_params=_cparams(2),
        name="branch_b",
    )(z, z, z, wdw, bdw, lng, lnb, wpw, bpw)


SEQS_PER_STEP = 8


def _branch_b_sample_kernel(a_ref, b_ref, g_ref, st_ref, wdw_ref, bdw_ref, lng_ref, lnb_ref, wpw_ref,
                            bpw_ref, o_ref, h_ref, hp_ref, y_ref, *, t_new):
    n_hist = CONV_W - 1
    for s in range(SEQS_PER_STEP):
        h = a_ref[s] * jax.nn.sigmoid(b_ref[s])
        h_ref[s] = h
        hp_ref[s, CONV_PAD:CONV_PAD + SUBLANES, :] = jnp.zeros((SUBLANES, BRANCH_W), F32)
        hp_ref[s, 0:n_hist, :] = st_ref[s]
        hp_ref[s, n_hist:n_hist + t_new, :] = h
        acc = hp_ref[s, pl.ds(0, SUBLANES), :] * wdw_ref[0:1, :]
        for j in range(1, CONV_W):
            acc = acc + hp_ref[s, pl.ds(j, SUBLANES), :] * wdw_ref[j:j + 1, :]
        y_ref[s * SUBLANES:(s + 1) * SUBLANES, :] = _silu(_ln(acc + bdw_ref[...], lng_ref[...], lnb_ref[...]))
    out = _dot(y_ref[...].astype(BF16), wpw_ref[...]) + bpw_ref[...]
    for s in range(SEQS_PER_STEP):
        o_ref[s] = out[s * SUBLANES:s * SUBLANES + t_new, :] * _silu(g_ref[s])


def _branch_b_sample(z3, state, layer, wdw, bdw, lng, lnb, wpw, bpw):
    nb, t_new, _ = z3.shape
    ns = SEQS_PER_STEP
    blk = lambda c: pl.BlockSpec((ns, t_new, BRANCH_W), lambda b, c=c: (b, 0, c))
    vec = pl.BlockSpec((1, BRANCH_W), lambda b: (0, 0))
    row = pl.BlockSpec((ns, t_new, BRANCH_W), lambda b: (b, 0, 0))
    return pl.pallas_call(
        functools.partial(_branch_b_sample_kernel, t_new=t_new),
        grid=(nb // ns,),
        in_specs=[blk(3), blk(4), blk(5),
                  pl.BlockSpec((ns, CONV_W - 1, BRANCH_W), lambda b: (layer * (nb // ns) + b, 0, 0)),
                  pl.BlockSpec((CONV_PAD, BRANCH_W), lambda b: (0, 0)), vec, vec, vec,
                  pl.BlockSpec((BRANCH_W, BRANCH_W), lambda b: (0, 0)), vec],
        out_specs=[row, row],
        out_shape=[jax.ShapeDtypeStruct((nb, t_new, BRANCH_W), F32)] * 2,
        scratch_shapes=[pltpu.VMEM((ns, CONV_PAD + SUBLANES, BRANCH_W), F32),
                        pltpu.VMEM((ns * SUBLANES, BRANCH_W), F32)],
        compiler_params=_cparams(1),
        name="branch_b_sample",
    )(z3, z3, z3, state, wdw, bdw, lng, lnb, wpw, bpw)


def _fox_kernel(q_ref, k_ref, v_ref, c_ref, g_ref, o_ref, vt_ref, s0_ref, s1_ref, m_ref, l_ref, acc_ref, *,
                tq, nk):
    qi = pl.program_id(2)

    @pl.when(qi == 0)
    def _():
        for j in range(nk):
            vt_ref[j] = v_ref[j * tq:(j + 1) * tq, :].astype(F32).T.astype(BF16)

    m_ref[...] = jnp.full_like(m_ref, NEG)
    l_ref[...] = jnp.zeros_like(l_ref)
    acc_ref[...] = jnp.zeros_like(acc_ref)
    qs = (q_ref[...].astype(F32) * (ATT_SCALE * LOG2E)).astype(BF16)

    def rows(kj):
        return pl.ds(pl.multiple_of(kj * tq, tq), tq)

    def scores(kj, s_ref):
        s_ref[...] = _dot_nt(k_ref[rows(kj), :].astype(BF16), qs)

    def softmax_pv(kj, s_ref, diagonal):
        c = c_ref[0, rows(kj), :]
        t = s_ref[...] - jnp.concatenate([c] * (tq // LANE), axis=1)
        if diagonal:
            key = lax.broadcasted_iota(jnp.int32, (tq, tq), 0)
            qry = lax.broadcasted_iota(jnp.int32, (tq, tq), 1)
            t = jnp.where(key <= qry, t, NEG)
        m_prev = m_ref[...]
        m_new = jnp.maximum(m_prev, jnp.max(t, axis=0, keepdims=True))
        alpha = jnp.exp2(m_prev - m_new)
        p = jnp.exp2(t - m_new)
        l_ref[...] = alpha * l_ref[...] + jnp.sum(p, axis=0, keepdims=True)
        acc_ref[...] = alpha * acc_ref[...] + _dot(vt_ref[kj], p.astype(BF16))
        m_ref[...] = m_new

    scores(0, s0_ref)

    def pair(p, carry):
        kj = 2 * p
        scores(kj + 1, s1_ref)
        softmax_pv(kj, s0_ref, False)
        scores(kj + 2, s0_ref)
        softmax_pv(kj + 1, s1_ref, False)
        return carry

    lax.fori_loop(0, lax.shift_right_logical(qi, 1), pair, 0)

    @pl.when((qi & 1) == 0)
    def _():
        softmax_pv(qi, s0_ref, True)

    @pl.when((qi & 1) == 1)
    def _():
        scores(qi, s1_ref)
        softmax_pv(qi - 1, s0_ref, False)
        softmax_pv(qi, s1_ref, True)

    o_ref[...] = ((acc_ref[...] / l_ref[...]).T * _silu(g_ref[...].astype(F32))).astype(o_ref.dtype)


def _fox_prompt(z, c_rep, *, batch, tq):
    m = z.shape[0]
    seq = m // batch
    nq = seq // tq
    qmap = lambda c: (lambda b, h, qi: (b * nq + qi, c + h))
    kvmap = lambda c: (lambda b, h, qi: (b, c + h))
    return pl.pallas_call(
        functools.partial(_fox_kernel, tq=tq, nk=nq),
        grid=(batch, H_C, nq),
        in_specs=[pl.BlockSpec((tq, HD), qmap(QC)),
                  pl.BlockSpec((seq, HD), kvmap(KC)),
                  pl.BlockSpec((seq, HD), kvmap(VC)),
                  pl.BlockSpec((1, seq, LANE), lambda b, h, qi: (h, b, 0)),
                  pl.BlockSpec((tq, HD), qmap(GC))],
        out_specs=pl.BlockSpec((tq, HD), lambda b, h, qi: (b * nq + qi, h)),
        out_shape=jax.ShapeDtypeStruct((m, BRANCH_W), BF16),
        scratch_shapes=[pltpu.VMEM((nq, HD, tq), BF16), pltpu.VMEM((tq, tq), F32), pltpu.VMEM((tq, tq), F32),
                        pltpu.VMEM((1, tq), F32), pltpu.VMEM((1, tq), F32), pltpu.VMEM((HD, tq), F32)],
        compiler_params=_cparams(3),
        name="fox_prompt",
    )(z, z, z, c_rep, z)


def _mem_attn_kernel(q_ref, k_ref, v_ref, g_ref, o_ref):
    for h in range(H_C):
        cs = slice(h * HD, (h + 1) * HD)
        s = _dot_nt(q_ref[:, cs].astype(BF16), k_ref[:, cs].astype(BF16)) * ATT_SCALE
        p = jnp.exp(s - jnp.max(s, axis=1, keepdims=True))
        o = _dot(p.astype(BF16), v_ref[:, cs].astype(BF16)) / jnp.sum(p, axis=1, keepdims=True)
        o_ref[:, cs] = (o * _silu(g_ref[:, cs].astype(F32))).astype(o_ref.dtype)


def _mem_attn_prompt(z, mkv, *, batch, tq):
    m = z.shape[0]
    nq = m // batch // tq
    wide = lambda c: pl.BlockSpec((tq, BRANCH_W), lambda b, qi, c=c: (b * nq + qi, c * LANE // BRANCH_W))
    return pl.pallas_call(
        _mem_attn_kernel,
        grid=(batch, nq),
        in_specs=[wide(QM),
                  pl.BlockSpec((N_MEM, BRANCH_W), lambda b, qi: (b, 0)),
                  pl.BlockSpec((N_MEM, BRANCH_W), lambda b, qi: (b, 1)),
                  wide(GM)],
        out_specs=pl.BlockSpec((tq, BRANCH_W), lambda b, qi: (b * nq + qi, 0)),
        out_shape=jax.ShapeDtypeStruct((m, BRANCH_W), BF16),
        compiler_params=_cparams(2),
        name="mem_attn_prompt",
    )(z, mkv, mkv, z)


def _head_match(rows, cols):
    r = lax.broadcasted_iota(jnp.int32, (rows, cols), 0)
    c = lax.broadcasted_iota(jnp.int32, (rows, cols), 1)
    return r, c, (r & (H_C - 1)) == (c & (H_C - 1))


def _mem_attn_sample_kernel(q_ref, g_ref, k_ref, v_ref, o_ref):
    _, _, same = _head_match(q_ref.shape[1], k_ref.shape[1])
    for i in range(SEQS_PER_STEP):
        s = _dot_nt(q_ref[i].astype(BF16), k_ref[i].astype(BF16)) * ATT_SCALE
        s = jnp.where(same, s, NEG)
        p = jnp.exp(s - jnp.max(s, axis=1, keepdims=True))
        o = _dot(p.astype(BF16), v_ref[i].astype(BF16)) / jnp.sum(p, axis=1, keepdims=True)
        o_ref[i] = o * _silu(g_ref[i])


def _mem_attn_sample(q16, g16, mk, mv, layer):
    nb, nr, _ = q16.shape
    nm = mk.shape[1]
    ns = SEQS_PER_STEP
    row = pl.BlockSpec((ns, nr, HD), lambda b: (b, 0, 0))
    mem = pl.BlockSpec((ns, nm, HD), lambda b: (layer * (nb // ns) + b, 0, 0))
    return pl.pallas_call(
        _mem_attn_sample_kernel,
        grid=(nb // ns,),
        in_specs=[row, row, mem, mem],
        out_specs=row,
        out_shape=jax.ShapeDtypeStruct((nb, nr, HD), F32),
        compiler_params=_cparams(1),
        name="mem_attn_sample",
    )(q16, g16, mk, mv)


def _logf_pages_kernel(x_ref, mc_ref, mt_ref, o_ref):
    x = x_ref[...]
    o_ref[:, 0, 0:PAGE_ROWS] = _dot_exact01(x, mc_ref[...])
    o_ref[:, 0, PAGE_ROWS:2 * PAGE_ROWS] = _dot_exact01(x, mt_ref[...])


def _logf_pages(lf_flat, mc, mt, *, tm):
    n = lf_flat.shape[0]
    mat = pl.BlockSpec((PAGE_ROWS, PAGE_ROWS), lambda i: (0, 0))
    return pl.pallas_call(
        _logf_pages_kernel,
        grid=(n // tm,),
        in_specs=[pl.BlockSpec((tm, PAGE_ROWS), lambda i: (i, 0)), mat, mat],
        out_specs=pl.BlockSpec((tm, 1, 2 * PAGE_ROWS), lambda i: (i, 0, 0)),
        out_shape=jax.ShapeDtypeStruct((n, 1, 2 * PAGE_ROWS), F32),
        compiler_params=_cparams(1),
        name="logf_pages",
    )(lf_flat, mc, mt)


def _softmax_update(state, s_list, v_list):
    m_prev, l_prev, acc_prev = state
    m_new = m_prev
    for s in s_list:
        m_new = jnp.maximum(m_new, jnp.max(s, axis=1, keepdims=True))
    alpha = jnp.exp(m_prev - m_new)
    l_new = alpha * l_prev
    acc = alpha * acc_prev
    for s, v in zip(s_list, v_list):
        p = jnp.exp(s - m_new)
        l_new = l_new + jnp.sum(p, axis=1, keepdims=True)
        acc = acc + _dot(p.astype(BF16), v)
    return m_new, l_new, acc


def _fox_paged_work(pt_ref, q_ref, g_ref, kn_ref, vn_ref, lfn_ref, mn_ref, k_hbm, v_hbm, wt_hbm, o_ref,
                    kbuf, vbuf, wtbuf, sem, m_ref, l_ref, acc_ref, carry_ref, *, step, n_steps, steps_per_seq, n_new):
    np_ = PAGES_PER_STEP
    sub = step & (steps_per_seq - 1)
    nr = q_ref.shape[1]

    def slot_copies(slot, page_of):
        copies = []
        for p in range(np_):
            page = page_of(p)
            copies.append(pltpu.make_async_copy(k_hbm.at[page], kbuf.at[slot, p], sem.at[slot]))
            copies.append(pltpu.make_async_copy(v_hbm.at[page], vbuf.at[slot, p], sem.at[slot]))
            copies.append(pltpu.make_async_copy(wt_hbm.at[page], wtbuf.at[slot, p], sem.at[slot]))
        return copies

    def start_step(for_step, slot):
        b = lax.shift_right_logical(for_step, steps_per_seq.bit_length() - 1)
        first_page = (for_step & (steps_per_seq - 1)) * np_
        for c in slot_copies(slot, lambda p: pt_ref[b, first_page + p]):
            c.start()

    def gather_and_scores():
        slot = step & 1

        @pl.when(step == 0)
        def _():
            start_step(step, slot)

        @pl.when(step + 1 < n_steps)
        def _():
            start_step(step + 1, 1 - slot)

        for c in slot_copies(slot, lambda p: 0):
            c.wait()

        first = sub == 0
        q = q_ref[0].astype(BF16)
        state = (jnp.where(first, NEG, m_ref[...]), jnp.where(first, 0.0, l_ref[...]),
                 jnp.where(first, 0.0, acc_ref[...]))
        carry = jnp.where(first, 0.0, carry_ref[...])
        _, _, same = _head_match(nr, PAGE_ROWS)
        s_list, v_list = [], []
        for p in range(np_):
            wt = wtbuf[slot, p]
            ck = carry + wt[:, 0:PAGE_ROWS]
            carry = carry + wt[:, PAGE_ROWS:2 * PAGE_ROWS]
            s = _dot_nt(q, kbuf[slot, p].astype(BF16)) * ATT_SCALE - ck
            s_list.append(jnp.where(same, s, NEG))
            v_list.append(vbuf[slot, p].astype(BF16))

        def softmax_and_values():
            new_state = _softmax_update(state, s_list, v_list)

            def finish():
                m_ref[...], l_ref[...], acc_ref[...] = new_state
                carry_ref[...] = carry

                @pl.when(sub == steps_per_seq - 1)
                def _():
                    r, c, same_n = _head_match(nr, LANE)
                    cn = carry[:, 0:LANE] + _dot_exact01(lfn_ref[0], mn_ref[...])[0:1, :]
                    s = _dot_nt(q, kn_ref[0].astype(BF16)) * ATT_SCALE - cn
                    ok = same_n & (c < n_new * H_C) & ((c >> 2) <= (r >> 2))
                    _, l_fin, acc_fin = _softmax_update(new_state, [jnp.where(ok, s, NEG)],
                                                        [vn_ref[0].astype(BF16)])
                    o_ref[0] = acc_fin / l_fin * _silu(g_ref[0])

            return finish

        return softmax_and_values

    return gather_and_scores


N_FOX_IN = 9


def _in_proj_fox_kernel(pt_ref, x_ref, w_ref, wf_ref, bf_ref, *rest, tm, tn, tiles_per_seq, nj, n_fox_steps,
                        steps_per_seq, n_new):
    fox_in, rest = rest[:N_FOX_IN], rest[N_FOX_IN:]
    (z_ref, lf_ref, ko_ref, vo_ref, c_ref, o_ref), rest = rest[:6], rest[6:]
    (xb_ref, ccarry_ref), fox_scratch = rest[:2], rest[2:]
    step = pl.program_id(0) * nj + pl.program_id(1)
    work = _fox_paged_work(pt_ref, *fox_in, o_ref, *fox_scratch, step=step, n_steps=n_fox_steps,
                           steps_per_seq=steps_per_seq, n_new=n_new)
    _in_proj_kernel(x_ref, w_ref, wf_ref, bf_ref, z_ref, lf_ref, ko_ref, vo_ref, c_ref, xb_ref, ccarry_ref,
                    tm=tm, tn=tn, tiles_per_seq=tiles_per_seq, emit_c=True, extra_work=work,
                    extra_active=step < n_fox_steps)


FUSED_VMEM_LIMIT = 56 * 1024 * 1024


def _in_proj_fox(x, w_main, wf, bfb, layer, pt, q16, g16, kn, vn, lfn, mn, kflat, vflat, wt3, *, tm, tiles_per_seq,
                 n_new):
    m = x.shape[0]
    tn = BRANCH_W
    nj = Z_COLS // tn
    n_seq, nr, _ = q16.shape
    n_pages = pt.shape[1]
    np_ = PAGES_PER_STEP
    steps_per_seq = n_pages // np_
    n_fox_steps = n_seq * steps_per_seq
    assert steps_per_seq * np_ == n_pages and steps_per_seq & (steps_per_seq - 1) == 0
    assert n_fox_steps <= (m // tm) * nj

    def seq_of(i, j):
        return jnp.minimum(i * nj + j, n_fox_steps - 1) // steps_per_seq

    per_seq = lambda rows: pl.BlockSpec((1, rows, HD), lambda i, j, pt: (seq_of(i, j), 0, 0))
    pool = pl.BlockSpec(memory_space=pl.ANY)
    heads = pl.BlockSpec((tm, H_C, HD), lambda i, j, pt: (i, 0, 0))
    grid_spec = pltpu.PrefetchScalarGridSpec(
        num_scalar_prefetch=1,
        grid=(m // tm, nj),
        in_specs=[
            pl.BlockSpec((tm, D_MODEL), lambda i, j, pt: (i, 0), pipeline_mode=pl.Buffered(1)),
            pl.BlockSpec((None, D_MODEL, tn), lambda i, j, pt: (layer, 0, j)),
            pl.BlockSpec((None, D_MODEL, LANE), lambda i, j, pt: (layer, 0, 0)),
            pl.BlockSpec((1, LANE), lambda i, j, pt: (0, 0)),
            per_seq(nr), per_seq(nr), per_seq(LANE), per_seq(LANE),
            pl.BlockSpec((1, 8, LANE), lambda i, j, pt: (seq_of(i, j), 0, 0)),
            pl.BlockSpec((LANE, LANE), lambda i, j, pt: (0, 0)),
            pool, pool, pool,
        ],
        out_specs=[pl.BlockSpec((tm, tn), lambda i, j, pt: (i, j)),
                   pl.BlockSpec((tm, LANE), lambda i, j, pt: (i, 0)), heads, heads,
                   pl.BlockSpec((H_C, tm, LANE), lambda i, j, pt: (0, i, 0)),
                   per_seq(nr)],
        scratch_shapes=[pltpu.VMEM((tm, D_MODEL), BF16), pltpu.VMEM((1, LANE), F32),
                        pltpu.VMEM((2, np_, PAGE_ROWS, HD), F32), pltpu.VMEM((2, np_, PAGE_ROWS, HD), F32),
                        pltpu.VMEM((2, np_, 1, 2 * PAGE_ROWS), F32), pltpu.SemaphoreType.DMA((2,)),
                        pltpu.VMEM((nr, 1), F32), pltpu.VMEM((nr, 1), F32), pltpu.VMEM((nr, HD), F32),
                        pltpu.VMEM((1, PAGE_ROWS), F32)],
    )
    kern = functools.partial(_in_proj_fox_kernel, tm=tm, tn=tn, tiles_per_seq=tiles_per_seq, nj=nj,
                             n_fox_steps=n_fox_steps, steps_per_seq=steps_per_seq, n_new=n_new)
    return pl.pallas_call(
        kern,
        grid_spec=grid_spec,
        out_shape=[jax.ShapeDtypeStruct((m, Z_COLS), BF16), jax.ShapeDtypeStruct((m, LANE), F32),
                   jax.ShapeDtypeStruct((m, H_C, HD), F32), jax.ShapeDtypeStruct((m, H_C, HD), F32),
                   jax.ShapeDtypeStruct((H_C, m, LANE), F32), jax.ShapeDtypeStruct((n_seq, nr, HD), F32)],
        compiler_params=_cparams(2, vmem=FUSED_VMEM_LIMIT),
        name="in_proj_fox",
    )(pt, x, w_main, wf, bfb, q16, g16, kn, vn, lfn, mn, kflat, vflat, wt3)


def _merge_out_kernel(*refs, tm):
    o_refs, wb_ref, gate_refs = refs[0:4], refs[4], refs[5:9]
    wo_ref, x_ref, g_ref, b_ref, y_ref, h_ref = refs[9:15]
    outs = [o_refs[br][...].astype(BF16) for br in range(N_BRANCH)]
    for jc in range(D_MODEL // BRANCH_W):
        cs = slice(jc * BRANCH_W, (jc + 1) * BRANCH_W)
        acc = None
        for br in range(N_BRANCH):
            gate = 0.5 * jnp.tanh(0.5 * gate_refs[br][:, cs].astype(F32)) + 0.5
            term = gate * _dot(outs[br], wb_ref[br, :, cs])
            acc = term if acc is None else acc + term
        h_ref[:, cs] = acc.astype(h_ref.dtype)
    half = max(tm // 2, SUBLANES)
    for s in range(tm // half):
        rs = slice(s * half, (s + 1) * half)
        y = ALPHA * x_ref[rs, :] + _dot(h_ref[rs, :], wo_ref[...])
        y_ref[rs, :] = _ln(y, g_ref[...], b_ref[...])


def _merge_out(outs, z, wb, wo, x, lng, lnb, layer, *, tm):
    m = z.shape[0]
    o_spec = pl.BlockSpec((tm, BRANCH_W), lambda i: (i, 0))
    gate0 = GATE_BLK * BRANCH_W // D_MODEL
    gate = lambda br: pl.BlockSpec((tm, D_MODEL), lambda i, br=br: (i, gate0 + br))
    rows = pl.BlockSpec((tm, D_MODEL), lambda i: (i, 0))
    vec = pl.BlockSpec((1, D_MODEL), lambda i: (0, 0))
    once = pl.Buffered(1)
    return pl.pallas_call(
        functools.partial(_merge_out_kernel, tm=tm),
        grid=(m // tm,),
        in_specs=[o_spec] * 4
                 + [pl.BlockSpec((None, N_BRANCH, BRANCH_W, D_MODEL), lambda i: (layer, 0, 0, 0), pipeline_mode=once)]
                 + [gate(br) for br in range(N_BRANCH)]
                 + [pl.BlockSpec((None, D_MODEL, D_MODEL), lambda i: (layer, 0, 0), pipeline_mode=once),
                    rows, vec, vec],
        out_specs=rows,
        out_shape=jax.ShapeDtypeStruct((m, D_MODEL), F32),
        scratch_shapes=[pltpu.VMEM((tm, D_MODEL), BF16)],
        compiler_params=_cparams(1),
        name="merge_out",
    )(*outs, wb, z, z, z, z, wo, x, lng, lnb)


def _tok_head_matrices():
    i = jnp.arange(PAGE_ROWS)
    same = (i[:, None] % H_C) == (i[None, :] % H_C)
    mc = (same & (i[:, None] // H_C <= i[None, :] // H_C)).astype(BF16)
    mt = same.astype(BF16)
    return mc, mt


def kernel(x_prompt, x_sample, mem_prompt, cache_k, cache_v, cache_logf, cache_mem_k, cache_mem_v, state_conv,
           page_table, w_in, w_mem_k, w_mem_v, ln_v_g, ln_v_b, w_s, b_s, w_dw, b_dw, ln_c_g, ln_c_b, w_pw, b_pw,
           b_f, w_branch, w_out, ln_g, ln_b):
    bp, seq, _ = x_prompt.shape
    db, t_new, _ = x_sample.shape
    n_pool = cache_k.shape[1]

    w_main, wf = _prep_w(w_in)
    bfb = jnp.pad(b_f, ((0, 0), (0, LANE - H_C)))[:, None, :]
    wb = w_branch.astype(BF16)
    wo = w_out.astype(BF16)
    wpw = w_pw.astype(BF16)
    wmkv = jnp.concatenate([w_mem_k, w_mem_v], axis=2).astype(BF16)
    wdw = jnp.pad(w_dw, ((0, 0), (0, CONV_PAD - CONV_W), (0, 0)))
    vec = lambda a: a[:, None, :]
    ln_v_g, ln_v_b, b_dw, ln_c_g, ln_c_b, b_pw, ln_g, ln_b = map(
        vec, (ln_v_g, ln_v_b, b_dw, ln_c_g, ln_c_b, b_pw, ln_g, ln_b))

    idx = jnp.arange(CHUNK)
    mask_p = (idx[None, :] <= idx[:, None]).astype(F32)
    bsb_p = jnp.broadcast_to(b_s[:, :, :, None], (DEPTH, A_GROUPS, CHUNK, CHUNK))
    reps = CHUNK // t_new
    mask_s = ((idx[:, None] // t_new == idx[None, :] // t_new) & (idx[None, :] <= idx[:, None])).astype(F32)
    pick = (idx[:, None] % t_new == jnp.arange(t_new)[None, :]).astype(F32)
    ws_s = jnp.einsum("rt,lgts,cs->lgrc", pick, w_s[:, :, :t_new, :t_new], pick, precision=lax.Precision.HIGHEST)
    bsb_s = jnp.broadcast_to(jnp.tile(b_s[:, :, :t_new], (1, 1, reps))[:, :, :, None],
                             (DEPTH, A_GROUPS, CHUNK, CHUNK))

    kflat = cache_k.reshape(DEPTH * n_pool, PAGE_ROWS, HD)
    vflat = cache_v.reshape(DEPTH * n_pool, PAGE_ROWS, HD)
    mc, mt = _tok_head_matrices()
    wt3 = _logf_pages(cache_logf.reshape(DEPTH * n_pool, PAGE_ROWS), mc, mt, tm=512)
    mn = jnp.pad(mc[:t_new * H_C, :t_new * H_C], ((0, LANE - t_new * H_C),) * 2)
    memk = cache_mem_k.reshape(DEPTH * db, N_MEM * H_C, HD)
    memv = cache_mem_v.reshape(DEPTH * db, N_MEM * H_C, HD)
    state = state_conv.reshape(DEPTH * db, CONV_W - 1, BRANCH_W)

    xp = x_prompt.reshape(bp * seq, D_MODEL)
    xs = x_sample.reshape(db * t_new, D_MODEL)
    mem2d = mem_prompt.reshape(bp * N_MEM, D_MODEL)
    m_s = db * t_new
    outs = [[] for _ in range(11)]
    for l in range(DEPTH):
        zs, lfs, k_s, v_s = _in_proj(xs, w_main, wf, bfb[l], l, tm=m_s, tiles_per_seq=1, emit_c=False,
                                     z_dtype=F32)
        heads = lambda c: zs[:, c * LANE:c * LANE + BRANCH_W].reshape(db, t_new * H_C, HD)
        pad_new = lambda a: jnp.pad(a.reshape(db, t_new * H_C, HD), ((0, 0), (0, LANE - t_new * H_C), (0, 0)))
        lfn = lfs[:, :H_C].reshape(db, 1, t_new * H_C)
        lfn = jnp.pad(lfn, ((0, 0), (0, 7), (0, LANE - t_new * H_C)))

        z, lf, k_p, v_p, c_rep, oc_s = _in_proj_fox(
            xp, w_main, wf, bfb[l], l, page_table + l * n_pool, heads(QC), heads(GC), pad_new(k_s), pad_new(v_s),
            lfn, mn, kflat, vflat, wt3, tm=1024, tiles_per_seq=seq // 1024, n_new=t_new)
        mkv = _mm(mem2d, wmkv[l], tm=bp * N_MEM, tn=2 * BRANCH_W)
        (oa,) = _branch_a(z, ln_v_g[l], ln_v_b[l], w_s[l], bsb_p[l], mask_p, tm=512, emit_v=False)
        ob, nconv = _branch_b(z, wdw[l], b_dw[l], ln_c_g[l], ln_c_b[l], wpw[l], b_pw[l], batch=bp, tm=512)
        oc = _fox_prompt(z, c_rep, batch=bp, tq=512)
        om = _mem_attn_prompt(z, mkv, batch=bp, tq=512)
        xp = _merge_out((oa, ob, oc, om), z, wb, wo, xp, ln_g[l], ln_b[l], l, tm=256)
        outs[0].append(k_p.reshape(bp, seq, H_C, HD))
        outs[1].append(v_p.reshape(bp, seq, H_C, HD))
        outs[2].append(lf[:, :H_C].reshape(bp, seq, H_C))
        outs[3].append(nconv)
        outs[4].append(mkv[:, :BRANCH_W].reshape(bp, N_MEM, H_C, HD))
        outs[5].append(mkv[:, BRANCH_W:].reshape(bp, N_MEM, H_C, HD))

        oa_s, v_rows = _branch_a(zs, ln_v_g[l], ln_v_b[l], ws_s[l], bsb_s[l], mask_s, tm=m_s, emit_v=True)
        ob_s, h_glu = _branch_b_sample(zs.reshape(db, t_new, Z_COLS), state, l, wdw[l], b_dw[l], ln_c_g[l],
                                       ln_c_b[l], wpw[l], b_pw[l])
        om_s = _mem_attn_sample(heads(QM), heads(GM), memk, memv, l)
        flat = lambda a: a.reshape(m_s, BRANCH_W)
        xs = _merge_out((oa_s, flat(ob_s), flat(oc_s), flat(om_s)), zs, wb, wo, xs, ln_g[l], ln_b[l], l, tm=m_s)
        outs[6].append(k_s.reshape(db, t_new, H_C, HD))
        outs[7].append(v_s.reshape(db, t_new, H_C, HD))
        outs[8].append(lfs[:, :H_C].reshape(db, t_new, H_C))
        outs[9].append(jnp.concatenate([state_conv[l][:, t_new:], h_glu], axis=1))
        outs[10].append(v_rows.reshape(db, t_new, BRANCH_W))

    return (xp.reshape(bp, seq, D_MODEL), xs.reshape(db, t_new, D_MODEL)) + tuple(jnp.stack(o) for o in outs)
```

```python
import functools
import math

import jax
import jax.numpy as jnp
from jax import lax
from jax.experimental import pallas as pl
from jax.experimental.pallas import tpu as pltpu

F32 = jnp.float32
BF16 = jnp.bfloat16

D_MODEL = 2048
DEPTH = 2
BRANCH_W = 512
N_BRANCH = 4
CHUNK = 128
A_GROUPS = 4
CONV_W = 31
H_C = 4
HD = 128
N_MEM = 256
PAGE_SIZE = 128
LN_EPS = 1e-5
ALPHA = (2 * DEPTH) ** 0.25
ATT_SCALE = HD ** -0.5
LOG2E = math.log2(math.e)
NEG = -1e30

LANE = 128
SUBLANES = 8
PAGE_ROWS = PAGE_SIZE * H_C
F_COL = 9 * BRANCH_W
Z_COLS = 28 * BRANCH_W
QC, KC, VC, GC, QM, GM = 24, 28, 32, 36, 40, 44
GATE_BLK = 12
PAGES_PER_STEP = 16
VMEM_LIMIT = 48 * 1024 * 1024


def _cparams(n_axes, vmem=VMEM_LIMIT):
    return pltpu.CompilerParams(dimension_semantics=("arbitrary",) * n_axes, vmem_limit_bytes=vmem)


def _ln(x, g, b):
    mu = jnp.mean(x, axis=-1, keepdims=True)
    xc = x - mu
    var = jnp.mean(xc * xc, axis=-1, keepdims=True)
    return xc * lax.rsqrt(var + LN_EPS) * g + b


def _silu(x):
    return x * jax.nn.sigmoid(x)


def _log_sigmoid(x):
    return jnp.minimum(x, 0.0) - jnp.log1p(jnp.exp(-jnp.abs(x)))


def _dot(a, b):
    return jnp.dot(a, b, preferred_element_type=F32)


def _dot_nt(a, b):
    return lax.dot_general(a, b, (((1,), (1,)), ((), ())), preferred_element_type=F32)


def _split3(x):
    hi = x.astype(BF16)
    r = x - hi.astype(F32)
    mid = r.astype(BF16)
    lo = (r - mid.astype(F32)).astype(BF16)
    return hi, mid, lo


def _dot_exact01(x, m01):
    hi, mid, lo = _split3(x)
    return _dot(hi, m01) + _dot(mid, m01) + _dot(lo, m01)


def _dot_exact01_left(m01, x):
    hi, mid, lo = _split3(x)
    return _dot(m01, hi) + _dot(m01, mid) + _dot(m01, lo)


def _prep_w_kernel(a_ref, f_ref, o_ref, wf_ref):
    j = pl.program_id(0)
    for l in range(DEPTH):
        o_ref[l] = a_ref[:, l, :].T.astype(BF16)

    @pl.when(j == 0)
    def _():
        lane = lax.broadcasted_iota(jnp.int32, (D_MODEL, LANE), 1)
        for l in range(DEPTH):
            wf_ref[l] = jnp.where(lane < H_C, f_ref[:, l, :].T, 0.0).astype(BF16)


def _prep_w(w_in):
    wt = jnp.transpose(w_in, (2, 0, 1))
    elems = lambda rows: (pl.Element(rows), pl.Element(DEPTH), pl.Element(D_MODEL))
    tc = BRANCH_W
    return pl.pallas_call(
        _prep_w_kernel,
        grid=(Z_COLS // tc,),
        in_specs=[pl.BlockSpec(elems(tc), lambda j: (j * tc + jnp.where(j >= F_COL // tc, H_C, 0), 0, 0)),
                  pl.BlockSpec(elems(LANE), lambda j: (F_COL, 0, 0))],
        out_specs=[pl.BlockSpec((DEPTH, D_MODEL, tc), lambda j: (0, 0, j)),
                   pl.BlockSpec((DEPTH, D_MODEL, LANE), lambda j: (0, 0, 0))],
        out_shape=[jax.ShapeDtypeStruct((DEPTH, D_MODEL, Z_COLS), BF16),
                   jax.ShapeDtypeStruct((DEPTH, D_MODEL, LANE), BF16)],
        compiler_params=_cparams(1),
        name="prep_w",
    )(wt, wt)


def _in_proj_kernel(x_ref, w_ref, wf_ref, bf_ref, z_ref, lf_ref, *rest, tm, tn, tiles_per_seq, emit_c, head_cols,
                    extra_work=None, extra_active=None):
    head_refs, rest = rest[:len(head_cols)], rest[len(head_cols):]
    if emit_c:
        c_ref, xb_ref, carry_ref = rest
    else:
        (xb_ref,) = rest
    i = pl.program_id(0)
    j = pl.program_id(1)

    @pl.when(j == 0)
    def _():
        xb = x_ref[...].astype(BF16)
        xb_ref[...] = xb
        lf = _log_sigmoid(_dot(xb, wf_ref[...]) + bf_ref[...])
        lf_ref[...] = lf
        if emit_c:
            @pl.when(lax.rem(i, tiles_per_seq) == 0)
            def _():
                carry_ref[...] = jnp.zeros_like(carry_ref)

            row = lax.broadcasted_iota(jnp.int32, (LANE, LANE), 0)
            col = lax.broadcasted_iota(jnp.int32, (LANE, LANE), 1)
            lower = jnp.where(col <= row, 1.0, 0.0).astype(BF16)
            carry = carry_ref[...]
            for r in range(tm // LANE):
                rs = slice(r * LANE, (r + 1) * LANE)
                cblk = _dot_exact01_left(lower, lf[rs, :]) + carry
                carry = cblk[LANE - 1:LANE, :]
                c2 = cblk * LOG2E
                for h in range(H_C):
                    c_ref[h, rs, :] = jnp.broadcast_to(c2[:, h:h + 1], (LANE, LANE))
            carry_ref[...] = carry

    def main(extra_work):
        after_matmul = extra_work() if extra_work is not None else None
        acc = _dot(xb_ref[...], w_ref[...])
        z_ref[...] = acc.astype(z_ref.dtype)
        finish = after_matmul() if after_matmul is not None else None

        def heads_out(o_ref, col0):
            for h in range(H_C):
                o_ref[:, h, :] = acc[:, col0 + h * HD:col0 + (h + 1) * HD]

        for col, o_ref in zip(head_cols, head_refs):
            pl.when(j == col * LANE // tn)(functools.partial(heads_out, o_ref, col * LANE % tn))

        if finish is not None:
            finish()

    if extra_work is None:
        main(None)
    else:
        @pl.when(extra_active)
        def _():
            main(extra_work)

        @pl.when(jnp.logical_not(extra_active))
        def _():
            main(None)


def _in_proj(x, w_main, wf, bfb, layer, *, tm, tiles_per_seq, emit_c, z_dtype, head_cols):
    m = x.shape[0]
    tn = 2 * BRANCH_W
    kern = functools.partial(_in_proj_kernel, tm=tm, tn=tn, tiles_per_seq=tiles_per_seq, emit_c=emit_c,
                             head_cols=head_cols)
    heads = pl.BlockSpec((tm, H_C, HD), lambda i, j: (i, 0, 0))
    out_specs = [pl.BlockSpec((tm, tn), lambda i, j: (i, j)),
                 pl.BlockSpec((tm, LANE), lambda i, j: (i, 0))] + [heads] * len(head_cols)
    out_shape = ([jax.ShapeDtypeStruct((m, Z_COLS), z_dtype), jax.ShapeDtypeStruct((m, LANE), F32)]
                 + [jax.ShapeDtypeStruct((m, H_C, HD), F32)] * len(head_cols))
    scratch = [pltpu.VMEM((tm, D_MODEL), BF16)]
    if emit_c:
        out_specs.append(pl.BlockSpec((H_C, tm, LANE), lambda i, j: (0, i, 0)))
        out_shape.append(jax.ShapeDtypeStruct((H_C, m, LANE), F32))
        scratch.append(pltpu.VMEM((1, LANE), F32))
    return pl.pallas_call(
        kern,
        grid=(m // tm, Z_COLS // tn),
        in_specs=[
            pl.BlockSpec((tm, D_MODEL), lambda i, j: (i, 0), pipeline_mode=pl.Buffered(1)),
            pl.BlockSpec((None, D_MODEL, tn), lambda i, j: (layer, 0, j)),
            pl.BlockSpec((None, D_MODEL, LANE), lambda i, j: (layer, 0, 0)),
            pl.BlockSpec((1, LANE), lambda i, j: (0, 0)),
        ],
        out_specs=out_specs,
        out_shape=out_shape,
        scratch_shapes=scratch,
        compiler_params=_cparams(2),
        name="in_proj",
    )(x, w_main, wf, bfb)


def _mm_kernel(x_ref, w_ref, o_ref):
    o_ref[...] = _dot(x_ref[...].astype(BF16), w_ref[...])


def _mm(x, w, *, tm, tn):
    m, k = x.shape
    n = w.shape[1]
    return pl.pallas_call(
        _mm_kernel,
        grid=(m // tm, n // tn),
        in_specs=[pl.BlockSpec((tm, k), lambda i, j: (i, 0)), pl.BlockSpec((k, tn), lambda i, j: (0, j))],
        out_specs=pl.BlockSpec((tm, tn), lambda i, j: (i, j)),
        out_shape=jax.ShapeDtypeStruct((m, n), F32),
        compiler_params=_cparams(2),
        name="mem_kv_proj",
    )(x, w)


def _branch_a_kernel(u_ref, v_ref, g_ref, lng_ref, lnb_ref, ws_ref, bsb_ref, mask_ref, o_ref, *rest,
                     tm, emit_v):
    u = jax.nn.gelu(u_ref[...].astype(F32))
    v = _ln(jax.nn.gelu(v_ref[...].astype(F32)), lng_ref[...], lnb_ref[...])
    if emit_v:
        rest[0][...] = v
    gate = _silu(g_ref[...].astype(F32))
    keep = mask_ref[...] > 0.0
    for g in range(A_GROUPS):
        wg = jnp.where(keep, ws_ref[g], 0.0).astype(BF16)
        cs = slice(g * LANE, (g + 1) * LANE)
        for c in range(tm // CHUNK):
            rs = slice(c * CHUNK, (c + 1) * CHUNK)
            s = _dot(wg, v[rs, cs].astype(BF16)) + bsb_ref[g]
            o_ref[rs, cs] = (u[rs, cs] * s * gate[rs, cs]).astype(o_ref.dtype)


def _branch_a(z, lng, lnb, ws, bsb, mask, *, tm, emit_v):
    m = z.shape[0]
    blk = lambda c: pl.BlockSpec((tm, BRANCH_W), lambda i, c=c: (i, c))
    vec = pl.BlockSpec((1, BRANCH_W), lambda i: (0, 0))
    cube = pl.BlockSpec((A_GROUPS, CHUNK, CHUNK), lambda i: (0, 0, 0))
    out_specs = [pl.BlockSpec((tm, BRANCH_W), lambda i: (i, 0))]
    out_shape = [jax.ShapeDtypeStruct((m, BRANCH_W), BF16)]
    if emit_v:
        out_specs.append(pl.BlockSpec((tm, BRANCH_W), lambda i: (i, 0)))
        out_shape.append(jax.ShapeDtypeStruct((m, BRANCH_W), F32))
    return pl.pallas_call(
        functools.partial(_branch_a_kernel, tm=tm, emit_v=emit_v),
        grid=(m // tm,),
        in_specs=[blk(0), blk(1), blk(2), vec, vec, cube, cube,
                  pl.BlockSpec((CHUNK, CHUNK), lambda i: (0, 0))],
        out_specs=out_specs,
        out_shape=out_shape,
        compiler_params=_cparams(1),
        name="branch_a",
    )(z, z, z, lng, lnb, ws, bsb, mask)


CONV_PAD = 32


def _conv_tail(y, bdw_ref, lng_ref, lnb_ref, wpw_ref, bpw_ref, gate):
    y = _silu(_ln(y + bdw_ref[...], lng_ref[...], lnb_ref[...]))
    return (_dot(y.astype(BF16), wpw_ref[...]) + bpw_ref[...]) * _silu(gate)


def _branch_b_kernel(a_ref, b_ref, g_ref, wdw_ref, bdw_ref, lng_ref, lnb_ref, wpw_ref, bpw_ref,
                     o_ref, nc_ref, hp_ref, sw_ref, *, tm):
    t = pl.program_id(1)

    @pl.when(t == 0)
    def _():
        hp_ref[0:CONV_PAD, :] = jnp.zeros((CONV_PAD, BRANCH_W), F32)

    hp_ref[CONV_PAD:CONV_PAD + tm, :] = a_ref[...].astype(F32) * jax.nn.sigmoid(b_ref[...].astype(F32))
    off = CONV_PAD - (CONV_W - 1)
    acc = None
    for r in range(SUBLANES):
        taps = [j for j in range(CONV_W) if (off + j) % SUBLANES == r]
        rows = tm + taps[-1] - taps[0]
        if r:
            sw_ref[0:rows, :] = hp_ref[pl.ds(off + taps[0], rows), :]
        for j in taps:
            src = sw_ref[j - taps[0]:j - taps[0] + tm, :] if r else hp_ref[off + j:off + j + tm, :]
            term = src * wdw_ref[j:j + 1, :]
            acc = term if acc is None else acc + term
    gate = g_ref[...].astype(F32)
    o_ref[...] = _conv_tail(acc, bdw_ref, lng_ref, lnb_ref, wpw_ref, bpw_ref, gate).astype(o_ref.dtype)
    nc_ref[0] = hp_ref[pl.ds(CONV_PAD + tm - (CONV_W - 1), CONV_W - 1), :]
    hp_ref[0:CONV_PAD, :] = hp_ref[tm:tm + CONV_PAD, :]


def _branch_b(z, wdw, bdw, lng, lnb, wpw, bpw, *, batch, tm):
    m = z.shape[0]
    nt = m // batch // tm
    blk = lambda c: pl.BlockSpec((tm, BRANCH_W), lambda b, t, c=c: (b * nt + t, c))
    vec = pl.BlockSpec((1, BRANCH_W), lambda b, t: (0, 0))
    return pl.pallas_call(
        functools.partial(_branch_b_kernel, tm=tm),
        grid=(batch, nt),
        in_specs=[blk(3), blk(4), blk(5),
                  pl.BlockSpec((CONV_PAD, BRANCH_W), lambda b, t: (0, 0)), vec, vec, vec,
                  pl.BlockSpec((BRANCH_W, BRANCH_W), lambda b, t: (0, 0)), vec],
        out_specs=[pl.BlockSpec((tm, BRANCH_W), lambda b, t: (b * nt + t, 0)),
                   pl.BlockSpec((1, CONV_W - 1, BRANCH_W), lambda b, t: (b, 0, 0))],
        out_shape=[jax.ShapeDtypeStruct((m, BRANCH_W), BF16),
                   jax.ShapeDtypeStruct((batch, CONV_W - 1, BRANCH_W), F32)],
        scratch_shapes=[pltpu.VMEM((tm + CONV_PAD, BRANCH_W), F32)] * 2,
        compiler_params=_cparams(2),
        name="branch_b",
    )(z, z, z, wdw, bdw, lng, lnb, wpw, bpw)


SEQS_PER_STEP = 8


def _branch_b_sample_kernel(a_ref, b_ref, g_ref, st_ref, wdw_ref, bdw_ref, lng_ref, lnb_ref, wpw_ref,
                            bpw_ref, o_ref, h_ref, hp_ref, y_ref, *, t_new):
    n_hist = CONV_W - 1
    for s in range(SEQS_PER_STEP):
        h = a_ref[s] * jax.nn.sigmoid(b_ref[s])
        h_ref[s] = h
        hp_ref[s, CONV_PAD:CONV_PAD + SUBLANES, :] = jnp.zeros((SUBLANES, BRANCH_W), F32)
        hp_ref[s, 0:n_hist, :] = st_ref[s]
        hp_ref[s, n_hist:n_hist + t_new, :] = h
        acc = hp_ref[s, pl.ds(0, SUBLANES), :] * wdw_ref[0:1, :]
        for j in range(1, CONV_W):
            acc = acc + hp_ref[s, pl.ds(j, SUBLANES), :] * wdw_ref[j:j + 1, :]
        y_ref[s * SUBLANES:(s + 1) * SUBLANES, :] = _silu(_ln(acc + bdw_ref[...], lng_ref[...], lnb_ref[...]))
    out = _dot(y_ref[...].astype(BF16), wpw_ref[...]) + bpw_ref[...]
    for s in range(SEQS_PER_STEP):
        o_ref[s] = out[s * SUBLANES:s * SUBLANES + t_new, :] * _silu(g_ref[s])


def _branch_b_sample(z3, state, layer, wdw, bdw, lng, lnb, wpw, bpw):
    nb, t_new, _ = z3.shape
    ns = SEQS_PER_STEP
    blk = lambda c: pl.BlockSpec((ns, t_new, BRANCH_W), lambda b, c=c: (b, 0, c))
    vec = pl.BlockSpec((1, BRANCH_W), lambda b: (0, 0))
    row = pl.BlockSpec((ns, t_new, BRANCH_W), lambda b: (b, 0, 0))
    return pl.pallas_call(
        functools.partial(_branch_b_sample_kernel, t_new=t_new),
        grid=(nb // ns,),
        in_specs=[blk(3), blk(4), blk(5),
                  pl.BlockSpec((ns, CONV_W - 1, BRANCH_W), lambda b: (layer * (nb // ns) + b, 0, 0)),
                  pl.BlockSpec((CONV_PAD, BRANCH_W), lambda b: (0, 0)), vec, vec, vec,
                  pl.BlockSpec((BRANCH_W, BRANCH_W), lambda b: (0, 0)), vec],
        out_specs=[row, row],
        out_shape=[jax.ShapeDtypeStruct((nb, t_new, BRANCH_W), F32)] * 2,
        scratch_shapes=[pltpu.VMEM((ns, CONV_PAD + SUBLANES, BRANCH_W), F32),
                        pltpu.VMEM((ns * SUBLANES, BRANCH_W), F32)],
        compiler_params=_cparams(1),
        name="branch_b_sample",
    )(z3, z3, z3, state, wdw, bdw, lng, lnb, wpw, bpw)


def _fox_kernel(q_ref, k_ref, v_ref, c_ref, g_ref, o_ref, vt_ref, s0_ref, s1_ref, m_ref, l_ref, acc_ref, *,
                tq, nk):
    qi = pl.program_id(2)

    @pl.when(qi == 0)
    def _():
        for j in range(nk):
            vt_ref[j] = v_ref[j * tq:(j + 1) * tq, :].astype(F32).T.astype(BF16)

    m_ref[...] = jnp.full_like(m_ref, NEG)
    l_ref[...] = jnp.zeros_like(l_ref)
    acc_ref[...] = jnp.zeros_like(acc_ref)
    qs = (q_ref[...].astype(F32) * (ATT_SCALE * LOG2E)).astype(BF16)

    def rows(kj):
        return pl.ds(pl.multiple_of(kj * tq, tq), tq)

    def scores(kj, s_ref):
        s_ref[...] = _dot_nt(k_ref[rows(kj), :].astype(BF16), qs)

    def softmax_pv(kj, s_ref, diagonal):
        c = c_ref[0, rows(kj), :]
        t = s_ref[...] - jnp.concatenate([c] * (tq // LANE), axis=1)
        if diagonal:
            key = lax.broadcasted_iota(jnp.int32, (tq, tq), 0)
            qry = lax.broadcasted_iota(jnp.int32, (tq, tq), 1)
            t = jnp.where(key <= qry, t, NEG)
        m_prev = m_ref[...]
        m_new = jnp.maximum(m_prev, jnp.max(t, axis=0, keepdims=True))
        alpha = jnp.exp2(m_prev - m_new)
        p = jnp.exp2(t - m_new)
        l_ref[...] = alpha * l_ref[...] + jnp.sum(p, axis=0, keepdims=True)
        acc_ref[...] = alpha * acc_ref[...] + _dot(vt_ref[kj], p.astype(BF16))
        m_ref[...] = m_new

    scores(0, s0_ref)

    def pair(p, carry):
        kj = 2 * p
        scores(kj + 1, s1_ref)
        softmax_pv(kj, s0_ref, False)
        scores(kj + 2, s0_ref)
        softmax_pv(kj + 1, s1_ref, False)
        return carry

    lax.fori_loop(0, lax.shift_right_logical(qi, 1), pair, 0)

    @pl.when((qi & 1) == 0)
    def _():
        softmax_pv(qi, s0_ref, True)

    @pl.when((qi & 1) == 1)
    def _():
        scores(qi, s1_ref)
        softmax_pv(qi - 1, s0_ref, False)
        softmax_pv(qi, s1_ref, True)

    o_ref[...] = ((acc_ref[...] / l_ref[...]).T * _silu(g_ref[...].astype(F32))).astype(o_ref.dtype)


def _fox_prompt(z, c_rep, *, batch, tq):
    m = z.shape[0]
    seq = m // batch
    nq = seq // tq
    qmap = lambda c: (lambda b, h, qi: (b * nq + qi, c + h))
    kvmap = lambda c: (lambda b, h, qi: (b, c + h))
    return pl.pallas_call(
        functools.partial(_fox_kernel, tq=tq, nk=nq),
        grid=(batch, H_C, nq),
        in_specs=[pl.BlockSpec((tq, HD), qmap(QC)),
                  pl.BlockSpec((seq, HD), kvmap(KC)),
                  pl.BlockSpec((seq, HD), kvmap(VC)),
                  pl.BlockSpec((1, seq, LANE), lambda b, h, qi: (h, b, 0)),
                  pl.BlockSpec((tq, HD), qmap(GC))],
        out_specs=pl.BlockSpec((tq, HD), lambda b, h, qi: (b * nq + qi, h)),
        out_shape=jax.ShapeDtypeStruct((m, BRANCH_W), BF16),
        scratch_shapes=[pltpu.VMEM((nq, HD, tq), BF16), pltpu.VMEM((tq, tq), F32), pltpu.VMEM((tq, tq), F32),
                        pltpu.VMEM((1, tq), F32), pltpu.VMEM((1, tq), F32), pltpu.VMEM((HD, tq), F32)],
        compiler_params=_cparams(3),
        name="fox_prompt",
    )(z, z, z, c_rep, z)


def _mem_attn_kernel(q_ref, k_ref, v_ref, g_ref, o_ref):
    for h in range(H_C):
        cs = slice(h * HD, (h + 1) * HD)
        s = _dot_nt(q_ref[:, cs].astype(BF16), k_ref[:, cs].astype(BF16)) * ATT_SCALE
        p = jnp.exp(s - jnp.max(s, axis=1, keepdims=True))
        o = _dot(p.astype(BF16), v_ref[:, cs].astype(BF16)) / jnp.sum(p, axis=1, keepdims=True)
        o_ref[:, cs] = (o * _silu(g_ref[:, cs].astype(F32))).astype(o_ref.dtype)


def _mem_attn_prompt(z, mkv, *, batch, tq):
    m = z.shape[0]
    nq = m // batch // tq
    wide = lambda c: pl.BlockSpec((tq, BRANCH_W), lambda b, qi, c=c: (b * nq + qi, c * LANE // BRANCH_W))
    return pl.pallas_call(
        _mem_attn_kernel,
        grid=(batch, nq),
        in_specs=[wide(QM),
                  pl.BlockSpec((N_MEM, BRANCH_W), lambda b, qi: (b, 0)),
                  pl.BlockSpec((N_MEM, BRANCH_W), lambda b, qi: (b, 1)),
                  wide(GM)],
        out_specs=pl.BlockSpec((tq, BRANCH_W), lambda b, qi: (b * nq + qi, 0)),
        out_shape=jax.ShapeDtypeStruct((m, BRANCH_W), BF16),
        compiler_params=_cparams(2),
        name="mem_attn_prompt",
    )(z, mkv, mkv, z)


def _head_match(rows, cols):
    r = lax.broadcasted_iota(jnp.int32, (rows, cols), 0)
    c = lax.broadcasted_iota(jnp.int32, (rows, cols), 1)
    return r, c, (r & (H_C - 1)) == (c & (H_C - 1))


def _mem_attn_sample_kernel(q_ref, g_ref, k_ref, v_ref, o_ref):
    _, _, same = _head_match(q_ref.shape[1], k_ref.shape[1])
    for i in range(SEQS_PER_STEP):
        s = _dot_nt(q_ref[i].astype(BF16), k_ref[i].astype(BF16)) * ATT_SCALE
        s = jnp.where(same, s, NEG)
        p = jnp.exp(s - jnp.max(s, axis=1, keepdims=True))
        o = _dot(p.astype(BF16), v_ref[i].astype(BF16)) / jnp.sum(p, axis=1, keepdims=True)
        o_ref[i] = o * _silu(g_ref[i])


def _mem_attn_sample(q16, g16, mk, mv, layer):
    nb, nr, _ = q16.shape
    nm = mk.shape[1]
    ns = SEQS_PER_STEP
    row = pl.BlockSpec((ns, nr, HD), lambda b: (b, 0, 0))
    mem = pl.BlockSpec((ns, nm, HD), lambda b: (layer * (nb // ns) + b, 0, 0))
    return pl.pallas_call(
        _mem_attn_sample_kernel,
        grid=(nb // ns,),
        in_specs=[row, row, mem, mem],
        out_specs=row,
        out_shape=jax.ShapeDtypeStruct((nb, nr, HD), F32),
        compiler_params=_cparams(1),
        name="mem_attn_sample",
    )(q16, g16, mk, mv)


def _logf_pages_kernel(x_ref, mc_ref, mt_ref, o_ref):
    x = x_ref[...]
    o_ref[:, 0, 0:PAGE_ROWS] = _dot_exact01(x, mc_ref[...])
    o_ref[:, 0, PAGE_ROWS:2 * PAGE_ROWS] = _dot_exact01(x, mt_ref[...])


def _logf_pages(lf_flat, mc, mt, *, tm):
    n = lf_flat.shape[0]
    mat = pl.BlockSpec((PAGE_ROWS, PAGE_ROWS), lambda i: (0, 0))
    return pl.pallas_call(
        _logf_pages_kernel,
        grid=(n // tm,),
        in_specs=[pl.BlockSpec((tm, PAGE_ROWS), lambda i: (i, 0)), mat, mat],
        out_specs=pl.BlockSpec((tm, 1, 2 * PAGE_ROWS), lambda i: (i, 0, 0)),
        out_shape=jax.ShapeDtypeStruct((n, 1, 2 * PAGE_ROWS), F32),
        compiler_params=_cparams(1),
        name="logf_pages",
    )(lf_flat, mc, mt)


def _softmax_update(state, s_list, v_list):
    m_prev, l_prev, acc_prev = state
    m_new = m_prev
    for s in s_list:
        m_new = jnp.maximum(m_new, jnp.max(s, axis=1, keepdims=True))
    alpha = jnp.exp(m_prev - m_new)
    l_new = alpha * l_prev
    acc = alpha * acc_prev
    for s, v in zip(s_list, v_list):
        p = jnp.exp(s - m_new)
        l_new = l_new + jnp.sum(p, axis=1, keepdims=True)
        acc = acc + _dot(p.astype(BF16), v)
    return m_new, l_new, acc


def _fox_paged_work(pt_ref, q_ref, g_ref, kn_ref, vn_ref, lfn_ref, mn_ref, k_hbm, v_hbm, wt_hbm, o_ref,
                    kbuf, vbuf, wtbuf, sem, m_ref, l_ref, acc_ref, carry_ref, *, step, n_steps, steps_per_seq, n_new):
    np_ = PAGES_PER_STEP
    sub = step & (steps_per_seq - 1)
    nr = q_ref.shape[1]

    def slot_copies(slot, page_of):
        copies = []
        for p in range(np_):
            page = page_of(p)
            copies.append(pltpu.make_async_copy(k_hbm.at[page], kbuf.at[slot, p], sem.at[slot]))
            copies.append(pltpu.make_async_copy(v_hbm.at[page], vbuf.at[slot, p], sem.at[slot]))
            copies.append(pltpu.make_async_copy(wt_hbm.at[page], wtbuf.at[slot, p], sem.at[slot]))
        return copies

    def start_step(for_step, slot):
        b = lax.shift_right_logical(for_step, steps_per_seq.bit_length() - 1)
        first_page = (for_step & (steps_per_seq - 1)) * np_
        for c in slot_copies(slot, lambda p: pt_ref[b, first_page + p]):
            c.start()

    def gather_and_scores():
        slot = step & 1

        @pl.when(step == 0)
        def _():
            start_step(step, slot)

        @pl.when(step + 1 < n_steps)
        def _():
            start_step(step + 1, 1 - slot)

        for c in slot_copies(slot, lambda p: 0):
            c.wait()

        first = sub == 0
        q = q_ref[0].astype(BF16)
        state = (jnp.where(first, NEG, m_ref[...]), jnp.where(first, 0.0, l_ref[...]),
                 jnp.where(first, 0.0, acc_ref[...]))
        carry = jnp.where(first, 0.0, carry_ref[...])
        _, _, same = _head_match(nr, PAGE_ROWS)
        s_list, v_list = [], []
        for p in range(np_):
            wt = wtbuf[slot, p]
            ck = carry + wt[:, 0:PAGE_ROWS]
            carry = carry + wt[:, PAGE_ROWS:2 * PAGE_ROWS]
            s = _dot_nt(q, kbuf[slot, p].astype(BF16)) * ATT_SCALE - ck
            s_list.append(jnp.where(same, s, NEG))
            v_list.append(vbuf[slot, p].astype(BF16))

        def softmax_and_values():
            new_state = _softmax_update(state, s_list, v_list)

            def finish():
                m_ref[...], l_ref[...], acc_ref[...] = new_state
                carry_ref[...] = carry

                @pl.when(sub == steps_per_seq - 1)
                def _():
                    r, c, same_n = _head_match(nr, LANE)
                    cn = carry[:, 0:LANE] + _dot_exact01(lfn_ref[0], mn_ref[...])[0:1, :]
                    s = _dot_nt(q, kn_ref[0].astype(BF16)) * ATT_SCALE - cn
                    ok = same_n & (c < n_new * H_C) & ((c >> 2) <= (r >> 2))
                    _, l_fin, acc_fin = _softmax_update(new_state, [jnp.where(ok, s, NEG)],
                                                        [vn_ref[0].astype(BF16)])
                    o_ref[0] = acc_fin / l_fin * _silu(g_ref[0])

            return finish

        return softmax_and_values

    return gather_and_scores


N_FOX_IN = 9


def _in_proj_fox_kernel(pt_ref, x_ref, w_ref, wf_ref, bf_ref, *rest, tm, tn, tiles_per_seq, nj, n_fox_steps,
                        steps_per_seq, n_new):
    fox_in, rest = rest[:N_FOX_IN], rest[N_FOX_IN:]
    (z_ref, lf_ref, ko_ref, vo_ref, c_ref, o_ref), rest = rest[:6], rest[6:]
    (xb_ref, ccarry_ref), fox_scratch = rest[:2], rest[2:]
    step = pl.program_id(0) * nj + pl.program_id(1)
    work = _fox_paged_work(pt_ref, *fox_in, o_ref, *fox_scratch, step=step, n_steps=n_fox_steps,
                           steps_per_seq=steps_per_seq, n_new=n_new)
    _in_proj_kernel(x_ref, w_ref, wf_ref, bf_ref, z_ref, lf_ref, ko_ref, vo_ref, c_ref, xb_ref, ccarry_ref,
                    tm=tm, tn=tn, tiles_per_seq=tiles_per_seq, emit_c=True, head_cols=(KC, VC), extra_work=work,
                    extra_active=step < n_fox_steps)


FUSED_VMEM_LIMIT = 56 * 1024 * 1024


def _in_proj_fox(x, w_main, wf, bfb, layer, pt, q16, g16, kn, vn, lfn, mn, kflat, vflat, wt3, *, tm, tiles_per_seq,
                 n_new):
    m = x.shape[0]
    tn = BRANCH_W
    nj = Z_COLS // tn
    n_seq, nr, _ = q16.shape
    n_pages = pt.shape[1]
    np_ = PAGES_PER_STEP
    steps_per_seq = n_pages // np_
    n_fox_steps = n_seq * steps_per_seq
    assert steps_per_seq * np_ == n_pages and steps_per_seq & (steps_per_seq - 1) == 0
    assert n_fox_steps <= (m // tm) * nj

    def seq_of(i, j):
        return jnp.minimum(i * nj + j, n_fox_steps - 1) // steps_per_seq

    per_seq = lambda rows: pl.BlockSpec((1, rows, HD), lambda i, j, pt: (seq_of(i, j), 0, 0))
    pool = pl.BlockSpec(memory_space=pl.ANY)
    heads = pl.BlockSpec((tm, H_C, HD), lambda i, j, pt: (i, 0, 0))
    grid_spec = pltpu.PrefetchScalarGridSpec(
        num_scalar_prefetch=1,
        grid=(m // tm, nj),
        in_specs=[
            pl.BlockSpec((tm, D_MODEL), lambda i, j, pt: (i, 0), pipeline_mode=pl.Buffered(1)),
            pl.BlockSpec((None, D_MODEL, tn), lambda i, j, pt: (layer, 0, j)),
            pl.BlockSpec((None, D_MODEL, LANE), lambda i, j, pt: (layer, 0, 0)),
            pl.BlockSpec((1, LANE), lambda i, j, pt: (0, 0)),
            per_seq(nr), per_seq(nr), per_seq(LANE), per_seq(LANE),
            pl.BlockSpec((1, 8, LANE), lambda i, j, pt: (seq_of(i, j), 0, 0)),
            pl.BlockSpec((LANE, LANE), lambda i, j, pt: (0, 0)),
            pool, pool, pool,
        ],
        out_specs=[pl.BlockSpec((tm, tn), lambda i, j, pt: (i, j)),
                   pl.BlockSpec((tm, LANE), lambda i, j, pt: (i, 0)), heads, heads,
                   pl.BlockSpec((H_C, tm, LANE), lambda i, j, pt: (0, i, 0)),
                   per_seq(nr)],
        scratch_shapes=[pltpu.VMEM((tm, D_MODEL), BF16), pltpu.VMEM((1, LANE), F32),
                        pltpu.VMEM((2, np_, PAGE_ROWS, HD), F32), pltpu.VMEM((2, np_, PAGE_ROWS, HD), F32),
                        pltpu.VMEM((2, np_, 1, 2 * PAGE_ROWS), F32), pltpu.SemaphoreType.DMA((2,)),
                        pltpu.VMEM((nr, 1), F32), pltpu.VMEM((nr, 1), F32), pltpu.VMEM((nr, HD), F32),
                        pltpu.VMEM((1, PAGE_ROWS), F32)],
    )
    kern = functools.partial(_in_proj_fox_kernel, tm=tm, tn=tn, tiles_per_seq=tiles_per_seq, nj=nj,
                             n_fox_steps=n_fox_steps, steps_per_seq=steps_per_seq, n_new=n_new)
    return pl.pallas_call(
        kern,
        grid_spec=grid_spec,
        out_shape=[jax.ShapeDtypeStruct((m, Z_COLS), BF16), jax.ShapeDtypeStruct((m, LANE), F32),
                   jax.ShapeDtypeStruct((m, H_C, HD), F32), jax.ShapeDtypeStruct((m, H_C, HD), F32),
                   jax.ShapeDtypeStruct((H_C, m, LANE), F32), jax.ShapeDtypeStruct((n_seq, nr, HD), F32)],
        compiler_params=_cparams(2, vmem=FUSED_VMEM_LIMIT),
        name="in_proj_fox",
    )(pt, x, w_main, wf, bfb, q16, g16, kn, vn, lfn, mn, kflat, vflat, wt3)


def _merge_out_kernel(*refs, tm):
    o_refs, wb_ref, gate_refs = refs[0:4], refs[4], refs[5:9]
    wo_ref, x_ref, g_ref, b_ref, y_ref, h_ref = refs[9:15]
    outs = [o_refs[br][...].astype(BF16) for br in range(N_BRANCH)]
    for jc in range(D_MODEL // BRANCH_W):
        cs = slice(jc * BRANCH_W, (jc + 1) * BRANCH_W)
        acc = None
        for br in range(N_BRANCH):
            gate = 0.5 * jnp.tanh(0.5 * gate_refs[br][:, cs].astype(F32)) + 0.5
            term = gate * _dot(outs[br], wb_ref[br, :, cs])
            acc = term if acc is None else acc + term
        h_ref[:, cs] = acc.astype(h_ref.dtype)
    half = max(tm // 2, SUBLANES)
    for s in range(tm // half):
        rs = slice(s * half, (s + 1) * half)
        y = ALPHA * x_ref[rs, :] + _dot(h_ref[rs, :], wo_ref[...])
        y_ref[rs, :] = _ln(y, g_ref[...], b_ref[...])


def _merge_out(outs, z, wb, wo, x, lng, lnb, layer, *, tm):
    m = z.shape[0]
    o_spec = pl.BlockSpec((tm, BRANCH_W), lambda i: (i, 0))
    gate0 = GATE_BLK * BRANCH_W // D_MODEL
    gate = lambda br: pl.BlockSpec((tm, D_MODEL), lambda i, br=br: (i, gate0 + br))
    rows = pl.BlockSpec((tm, D_MODEL), lambda i: (i, 0))
    vec = pl.BlockSpec((1, D_MODEL), lambda i: (0, 0))
    once = pl.Buffered(1)
    return pl.pallas_call(
        functools.partial(_merge_out_kernel, tm=tm),
        grid=(m // tm,),
        in_specs=[o_spec] * 4
                 + [pl.BlockSpec((None, N_BRANCH, BRANCH_W, D_MODEL), lambda i: (layer, 0, 0, 0), pipeline_mode=once)]
                 + [gate(br) for br in range(N_BRANCH)]
                 + [pl.BlockSpec((None, D_MODEL, D_MODEL), lambda i: (layer, 0, 0), pipeline_mode=once),
                    rows, vec, vec],
        out_specs=rows,
        out_shape=jax.ShapeDtypeStruct((m, D_MODEL), F32),
        scratch_shapes=[pltpu.VMEM((tm, D_MODEL), BF16)],
        compiler_params=_cparams(1),
        name="merge_out",
    )(*outs, wb, z, z, z, z, wo, x, lng, lnb)


def _tok_head_matrices():
    i = jnp.arange(PAGE_ROWS)
    same = (i[:, None] % H_C) == (i[None, :] % H_C)
    mc = (same & (i[:, None] // H_C <= i[None, :] // H_C)).astype(BF16)
    mt = same.astype(BF16)
    return mc, mt


def kernel(x_prompt, x_sample, mem_prompt, cache_k, cache_v, cache_logf, cache_mem_k, cache_mem_v, state_conv,
           page_table, w_in, w_mem_k, w_mem_v, ln_v_g, ln_v_b, w_s, b_s, w_dw, b_dw, ln_c_g, ln_c_b, w_pw, b_pw,
           b_f, w_branch, w_out, ln_g, ln_b):
    bp, seq, _ = x_prompt.shape
    db, t_new, _ = x_sample.shape
    n_pool = cache_k.shape[1]

    w_main, wf = _prep_w(w_in)
    bfb = jnp.pad(b_f, ((0, 0), (0, LANE - H_C)))[:, None, :]
    wb = w_branch.astype(BF16)
    wo = w_out.astype(BF16)
    wpw = w_pw.astype(BF16)
    wmkv = jnp.concatenate([w_mem_k, w_mem_v], axis=2).astype(BF16)
    wdw = jnp.pad(w_dw, ((0, 0), (0, CONV_PAD - CONV_W), (0, 0)))
    vec = lambda a: a[:, None, :]
    ln_v_g, ln_v_b, b_dw, ln_c_g, ln_c_b, b_pw, ln_g, ln_b = map(
        vec, (ln_v_g, ln_v_b, b_dw, ln_c_g, ln_c_b, b_pw, ln_g, ln_b))

    idx = jnp.arange(CHUNK)
    mask_p = (idx[None, :] <= idx[:, None]).astype(F32)
    bsb_p = jnp.broadcast_to(b_s[:, :, :, None], (DEPTH, A_GROUPS, CHUNK, CHUNK))
    reps = CHUNK // t_new
    mask_s = ((idx[:, None] // t_new == idx[None, :] // t_new) & (idx[None, :] <= idx[:, None])).astype(F32)
    pick = (idx[:, None] % t_new == jnp.arange(t_new)[None, :]).astype(F32)
    ws_s = jnp.einsum("rt,lgts,cs->lgrc", pick, w_s[:, :, :t_new, :t_new], pick, precision=lax.Precision.HIGHEST)
    bsb_s = jnp.broadcast_to(jnp.tile(b_s[:, :, :t_new], (1, 1, reps))[:, :, :, None],
                             (DEPTH, A_GROUPS, CHUNK, CHUNK))

    kflat = cache_k.reshape(DEPTH * n_pool, PAGE_ROWS, HD)
    vflat = cache_v.reshape(DEPTH * n_pool, PAGE_ROWS, HD)
    mc, mt = _tok_head_matrices()
    wt3 = _logf_pages(cache_logf.reshape(DEPTH * n_pool, PAGE_ROWS), mc, mt, tm=512)
    mn = jnp.pad(mc[:t_new * H_C, :t_new * H_C], ((0, LANE - t_new * H_C),) * 2)
    memk = cache_mem_k.reshape(DEPTH * db, N_MEM * H_C, HD)
    memv = cache_mem_v.reshape(DEPTH * db, N_MEM * H_C, HD)
    state = state_conv.reshape(DEPTH * db, CONV_W - 1, BRANCH_W)

    xp = x_prompt.reshape(bp * seq, D_MODEL)
    xs = x_sample.reshape(db * t_new, D_MODEL)
    mem2d = mem_prompt.reshape(bp * N_MEM, D_MODEL)
    m_s = db * t_new
    outs = [[] for _ in range(11)]
    for l in range(DEPTH):
        zs, lfs, *slabs = _in_proj(xs, w_main, wf, bfb[l], l, tm=m_s, tiles_per_seq=1, emit_c=False, z_dtype=F32,
                                   head_cols=(QC, KC, VC, GC, QM, GM))
        q_s, k_s, v_s, gc_s, qm_s, gm_s = (a.reshape(db, t_new * H_C, HD) for a in slabs)
        pad_new = lambda a: jnp.pad(a, ((0, 0), (0, LANE - t_new * H_C), (0, 0)))
        lfn = lfs[:, :H_C].reshape(db, 1, t_new * H_C)
        lfn = jnp.pad(lfn, ((0, 0), (0, 7), (0, LANE - t_new * H_C)))

        z, lf, k_p, v_p, c_rep, oc_s = _in_proj_fox(
            xp, w_main, wf, bfb[l], l, page_table + l * n_pool, q_s, gc_s, pad_new(k_s), pad_new(v_s),
            lfn, mn, kflat, vflat, wt3, tm=1024, tiles_per_seq=seq // 1024, n_new=t_new)
        mkv = _mm(mem2d, wmkv[l], tm=bp * N_MEM, tn=2 * BRANCH_W)
        (oa,) = _branch_a(z, ln_v_g[l], ln_v_b[l], w_s[l], bsb_p[l], mask_p, tm=512, emit_v=False)
        ob, nconv = _branch_b(z, wdw[l], b_dw[l], ln_c_g[l], ln_c_b[l], wpw[l], b_pw[l], batch=bp, tm=512)
        oc = _fox_prompt(z, c_rep, batch=bp, tq=512)
        om = _mem_attn_prompt(z, mkv, batch=bp, tq=512)
        xp = _merge_out((oa, ob, oc, om), z, wb, wo, xp, ln_g[l], ln_b[l], l, tm=256)
        outs[0].append(k_p.reshape(bp, seq, H_C, HD))
        outs[1].append(v_p.reshape(bp, seq, H_C, HD))
        outs[2].append(lf[:, :H_C].reshape(bp, seq, H_C))
        outs[3].append(nconv)
        outs[4].append(mkv[:, :BRANCH_W].reshape(bp, N_MEM, H_C, HD))
        outs[5].append(mkv[:, BRANCH_W:].reshape(bp, N_MEM, H_C, HD))

        oa_s, v_rows = _branch_a(zs, ln_v_g[l], ln_v_b[l], ws_s[l], bsb_s[l], mask_s, tm=m_s, emit_v=True)
        ob_s, h_glu = _branch_b_sample(zs.reshape(db, t_new, Z_COLS), state, l, wdw[l], b_dw[l], ln_c_g[l],
                                       ln_c_b[l], wpw[l], b_pw[l])
        om_s = _mem_attn_sample(qm_s, gm_s, memk, memv, l)
        flat = lambda a: a.reshape(m_s, BRANCH_W)
        xs = _merge_out((oa_s, flat(ob_s), flat(oc_s), flat(om_s)), zs, wb, wo, xs, ln_g[l], ln_b[l], l, tm=m_s)
        outs[6].append(k_s.reshape(db, t_new, H_C, HD))
        outs[7].append(v_s.reshape(db, t_new, H_C, HD))
        outs[8].append(lfs[:, :H_C].reshape(db, t_new, H_C))
        outs[9].append(jnp.concatenate([state_conv[l][:, t_new:], h_glu], axis=1))
        outs[10].append(v_rows.reshape(db, t_new, BRANCH_W))

    return (xp.reshape(bp, seq, D_MODEL), xs.reshape(db, t_new, D_MODEL)) + tuple(jnp.stack(o) for o in outs)
```

```python
import functools
import math

import jax
import jax.numpy as jnp
from jax import lax
from jax.experimental import pallas as pl
from jax.experimental.pallas import tpu as pltpu

F32 = jnp.float32
BF16 = jnp.bfloat16

D_MODEL = 2048
DEPTH = 2
BRANCH_W = 512
N_BRANCH = 4
CHUNK = 128
A_GROUPS = 4
CONV_W = 31
H_C = 4
HD = 128
N_MEM = 256
PAGE_SIZE = 128
LN_EPS = 1e-5
ALPHA = (2 * DEPTH) ** 0.25
ATT_SCALE = HD ** -0.5
LOG2E = math.log2(math.e)
NEG = -1e30

LANE = 128
SUBLANES = 8
PAGE_ROWS = PAGE_SIZE * H_C
F_COL = 9 * BRANCH_W
Z_COLS = 28 * BRANCH_W
QC, KC, VC, GC, QM, GM = 24, 28, 32, 36, 40, 44
GATE_BLK = 12
PAGES_PER_STEP = 16
VMEM_LIMIT = 48 * 1024 * 1024


def _cparams(n_axes, vmem=VMEM_LIMIT):
    return pltpu.CompilerParams(dimension_semantics=("arbitrary",) * n_axes, vmem_limit_bytes=vmem)


def _ln(x, g, b):
    mu = jnp.mean(x, axis=-1, keepdims=True)
    xc = x - mu
    var = jnp.mean(xc * xc, axis=-1, keepdims=True)
    return xc * lax.rsqrt(var + LN_EPS) * g + b


def _silu(x):
    return x * jax.nn.sigmoid(x)


def _log_sigmoid(x):
    return jnp.minimum(x, 0.0) - jnp.log1p(jnp.exp(-jnp.abs(x)))


def _dot(a, b):
    return jnp.dot(a, b, preferred_element_type=F32)


def _dot_nt(a, b):
    return lax.dot_general(a, b, (((1,), (1,)), ((), ())), preferred_element_type=F32)


def _split3(x):
    hi = x.astype(BF16)
    r = x - hi.astype(F32)
    mid = r.astype(BF16)
    lo = (r - mid.astype(F32)).astype(BF16)
    return hi, mid, lo


def _dot_exact01(x, m01):
    hi, mid, lo = _split3(x)
    return _dot(hi, m01) + _dot(mid, m01) + _dot(lo, m01)


def _dot_exact01_left(m01, x):
    hi, mid, lo = _split3(x)
    return _dot(m01, hi) + _dot(m01, mid) + _dot(m01, lo)


def _prep_w_kernel(a_ref, f_ref, o_ref, wf_ref):
    j = pl.program_id(0)
    for l in range(DEPTH):
        o_ref[l] = a_ref[:, l, :].T.astype(BF16)

    @pl.when(j == 0)
    def _():
        lane = lax.broadcasted_iota(jnp.int32, (D_MODEL, LANE), 1)
        for l in range(DEPTH):
            wf_ref[l] = jnp.where(lane < H_C, f_ref[:, l, :].T, 0.0).astype(BF16)


def _prep_w(w_in):
    wt = jnp.transpose(w_in, (2, 0, 1))
    elems = lambda rows: (pl.Element(rows), pl.Element(DEPTH), pl.Element(D_MODEL))
    tc = BRANCH_W
    return pl.pallas_call(
        _prep_w_kernel,
        grid=(Z_COLS // tc,),
        in_specs=[pl.BlockSpec(elems(tc), lambda j: (j * tc + jnp.where(j >= F_COL // tc, H_C, 0), 0, 0)),
                  pl.BlockSpec(elems(LANE), lambda j: (F_COL, 0, 0))],
        out_specs=[pl.BlockSpec((DEPTH, D_MODEL, tc), lambda j: (0, 0, j)),
                   pl.BlockSpec((DEPTH, D_MODEL, LANE), lambda j: (0, 0, 0))],
        out_shape=[jax.ShapeDtypeStruct((DEPTH, D_MODEL, Z_COLS), BF16),
                   jax.ShapeDtypeStruct((DEPTH, D_MODEL, LANE), BF16)],
        compiler_params=_cparams(1),
        name="prep_w",
    )(wt, wt)


def _in_proj_kernel(x_ref, w_ref, wf_ref, bf_ref, z_ref, lf_ref, *rest, tm, tn, tiles_per_seq, emit_c, head_cols,
                    extra_work=None, extra_active=None):
    head_refs, rest = rest[:len(head_cols)], rest[len(head_cols):]
    if emit_c:
        c_ref, xb_ref, carry_ref = rest
    else:
        (xb_ref,) = rest
    i = pl.program_id(0)
    j = pl.program_id(1)

    @pl.when(j == 0)
    def _():
        xb = x_ref[...].astype(BF16)
        xb_ref[...] = xb
        lf = _log_sigmoid(_dot(xb, wf_ref[...]) + bf_ref[...])
        lf_ref[...] = lf
        if emit_c:
            @pl.when(lax.rem(i, tiles_per_seq) == 0)
            def _():
                carry_ref[...] = jnp.zeros_like(carry_ref)

            row = lax.broadcasted_iota(jnp.int32, (LANE, LANE), 0)
            col = lax.broadcasted_iota(jnp.int32, (LANE, LANE), 1)
            lower = jnp.where(col <= row, 1.0, 0.0).astype(BF16)
            carry = carry_ref[...]
            for r in range(tm // LANE):
                rs = slice(r * LANE, (r + 1) * LANE)
                cblk = _dot_exact01_left(lower, lf[rs, :]) + carry
                carry = cblk[LANE - 1:LANE, :]
                c2 = cblk * LOG2E
                for h in range(H_C):
                    c_ref[h, rs, :] = jnp.broadcast_to(c2[:, h:h + 1], (LANE, LANE))
            carry_ref[...] = carry

    def main(extra_work):
        after_matmul = extra_work() if extra_work is not None else None
        acc = _dot(xb_ref[...], w_ref[...])
        z_ref[...] = acc.astype(z_ref.dtype)
        finish = after_matmul() if after_matmul is not None else None

        def heads_out(o_ref, col0):
            for h in range(H_C):
                o_ref[:, h, :] = acc[:, col0 + h * HD:col0 + (h + 1) * HD]

        for col, o_ref in zip(head_cols, head_refs):
            pl.when(j == col * LANE // tn)(functools.partial(heads_out, o_ref, col * LANE % tn))

        if finish is not None:
            finish()

    if extra_work is None:
        main(None)
    else:
        @pl.when(extra_active)
        def _():
            main(extra_work)

        @pl.when(jnp.logical_not(extra_active))
        def _():
            main(None)


def _in_proj(x, w_main, wf, bfb, layer, *, tm, tiles_per_seq, emit_c, z_dtype, head_cols):
    m = x.shape[0]
    tn = 2 * BRANCH_W
    kern = functools.partial(_in_proj_kernel, tm=tm, tn=tn, tiles_per_seq=tiles_per_seq, emit_c=emit_c,
                             head_cols=head_cols)
    heads = pl.BlockSpec((tm, H_C, HD), lambda i, j: (i, 0, 0))
    out_specs = [pl.BlockSpec((tm, tn), lambda i, j: (i, j)),
                 pl.BlockSpec((tm, LANE), lambda i, j: (i, 0))] + [heads] * len(head_cols)
    out_shape = ([jax.ShapeDtypeStruct((m, Z_COLS), z_dtype), jax.ShapeDtypeStruct((m, LANE), F32)]
                 + [jax.ShapeDtypeStruct((m, H_C, HD), F32)] * len(head_cols))
    scratch = [pltpu.VMEM((tm, D_MODEL), BF16)]
    if emit_c:
        out_specs.append(pl.BlockSpec((H_C, tm, LANE), lambda i, j: (0, i, 0)))
        out_shape.append(jax.ShapeDtypeStruct((H_C, m, LANE), F32))
        scratch.append(pltpu.VMEM((1, LANE), F32))
    return pl.pallas_call(
        kern,
        grid=(m // tm, Z_COLS // tn),
        in_specs=[
            pl.BlockSpec((tm, D_MODEL), lambda i, j: (i, 0), pipeline_mode=pl.Buffered(1)),
            pl.BlockSpec((None, D_MODEL, tn), lambda i, j: (layer, 0, j)),
            pl.BlockSpec((None, D_MODEL, LANE), lambda i, j: (layer, 0, 0)),
            pl.BlockSpec((1, LANE), lambda i, j: (0, 0)),
        ],
        out_specs=out_specs,
        out_shape=out_shape,
        scratch_shapes=scratch,
        compiler_params=_cparams(2),
        name="in_proj",
    )(x, w_main, wf, bfb)


def _mm_kernel(x_ref, w_ref, o_ref):
    o_ref[...] = _dot(x_ref[...].astype(BF16), w_ref[...])


def _mm(x, w, *, tm, tn):
    m, k = x.shape
    n = w.shape[1]
    return pl.pallas_call(
        _mm_kernel,
        grid=(m // tm, n // tn),
        in_specs=[pl.BlockSpec((tm, k), lambda i, j: (i, 0)), pl.BlockSpec((k, tn), lambda i, j: (0, j))],
        out_specs=pl.BlockSpec((tm, tn), lambda i, j: (i, j)),
        out_shape=jax.ShapeDtypeStruct((m, n), F32),
        compiler_params=_cparams(2),
        name="mem_kv_proj",
    )(x, w)


def _branch_a_kernel(u_ref, v_ref, g_ref, lng_ref, lnb_ref, ws_ref, bsb_ref, mask_ref, o_ref, *rest,
                     tm, emit_v):
    u = jax.nn.gelu(u_ref[...].astype(F32))
    v = _ln(jax.nn.gelu(v_ref[...].astype(F32)), lng_ref[...], lnb_ref[...])
    if emit_v:
        rest[0][...] = v
    gate = _silu(g_ref[...].astype(F32))
    keep = mask_ref[...] > 0.0
    for g in range(A_GROUPS):
        wg = jnp.where(keep, ws_ref[g], 0.0).astype(BF16)
        cs = slice(g * LANE, (g + 1) * LANE)
        for c in range(tm // CHUNK):
            rs = slice(c * CHUNK, (c + 1) * CHUNK)
            s = _dot(wg, v[rs, cs].astype(BF16)) + bsb_ref[g]
            o_ref[rs, cs] = (u[rs, cs] * s * gate[rs, cs]).astype(o_ref.dtype)


def _branch_a(z, lng, lnb, ws, bsb, mask, *, tm, emit_v):
    m = z.shape[0]
    blk = lambda c: pl.BlockSpec((tm, BRANCH_W), lambda i, c=c: (i, c))
    vec = pl.BlockSpec((1, BRANCH_W), lambda i: (0, 0))
    cube = pl.BlockSpec((A_GROUPS, CHUNK, CHUNK), lambda i: (0, 0, 0))
    out_specs = [pl.BlockSpec((tm, BRANCH_W), lambda i: (i, 0))]
    out_shape = [jax.ShapeDtypeStruct((m, BRANCH_W), BF16)]
    if emit_v:
        out_specs.append(pl.BlockSpec((tm, BRANCH_W), lambda i: (i, 0)))
        out_shape.append(jax.ShapeDtypeStruct((m, BRANCH_W), F32))
    return pl.pallas_call(
        functools.partial(_branch_a_kernel, tm=tm, emit_v=emit_v),
        grid=(m // tm,),
        in_specs=[blk(0), blk(1), blk(2), vec, vec, cube, cube,
                  pl.BlockSpec((CHUNK, CHUNK), lambda i: (0, 0))],
        out_specs=out_specs,
        out_shape=out_shape,
        compiler_params=_cparams(1),
        name="branch_a",
    )(z, z, z, lng, lnb, ws, bsb, mask)


CONV_PAD = 32


def _conv_tail(y, bdw_ref, lng_ref, lnb_ref, wpw_ref, bpw_ref, gate):
    y = _silu(_ln(y + bdw_ref[...], lng_ref[...], lnb_ref[...]))
    return (_dot(y.astype(BF16), wpw_ref[...]) + bpw_ref[...]) * _silu(gate)


def _branch_b_kernel(a_ref, b_ref, g_ref, wdw_ref, bdw_ref, lng_ref, lnb_ref, wpw_ref, bpw_ref,
                     o_ref, nc_ref, hp_ref, sw_ref, *, tm):
    t = pl.program_id(1)

    @pl.when(t == 0)
    def _():
        hp_ref[0:CONV_PAD, :] = jnp.zeros((CONV_PAD, BRANCH_W), F32)

    hp_ref[CONV_PAD:CONV_PAD + tm, :] = a_ref[...].astype(F32) * jax.nn.sigmoid(b_ref[...].astype(F32))
    off = CONV_PAD - (CONV_W - 1)
    acc = None
    for r in range(SUBLANES):
        taps = [j for j in range(CONV_W) if (off + j) % SUBLANES == r]
        rows = tm + taps[-1] - taps[0]
        if r:
            sw_ref[0:rows, :] = hp_ref[pl.ds(off + taps[0], rows), :]
        for j in taps:
            src = sw_ref[j - taps[0]:j - taps[0] + tm, :] if r else hp_ref[off + j:off + j + tm, :]
            term = src * wdw_ref[j:j + 1, :]
            acc = term if acc is None else acc + term
    gate = g_ref[...].astype(F32)
    o_ref[...] = _conv_tail(acc, bdw_ref, lng_ref, lnb_ref, wpw_ref, bpw_ref, gate).astype(o_ref.dtype)
    nc_ref[0] = hp_ref[pl.ds(CONV_PAD + tm - (CONV_W - 1), CONV_W - 1), :]
    hp_ref[0:CONV_PAD, :] = hp_ref[tm:tm + CONV_PAD, :]


def _branch_b(z, wdw, bdw, lng, lnb, wpw, bpw, *, batch, tm):
    m = z.shape[0]
    nt = m // batch // tm
    blk = lambda c: pl.BlockSpec((tm, BRANCH_W), lambda b, t, c=c: (b * nt + t, c))
    vec = pl.BlockSpec((1, BRANCH_W), lambda b, t: (0, 0))
    return pl.pallas_call(
        functools.partial(_branch_b_kernel, tm=tm),
        grid=(batch, nt),
        in_specs=[blk(3), blk(4), blk(5),
                  pl.BlockSpec((CONV_PAD, BRANCH_W), lambda b, t: (0, 0)), vec, vec, vec,
                  pl.BlockSpec((BRANCH_W, BRANCH_W), lambda b, t: (0, 0)), vec],
        out_specs=[pl.BlockSpec((tm, BRANCH_W), lambda b, t: (b * nt + t, 0)),
                   pl.BlockSpec((1, CONV_W - 1, BRANCH_W), lambda b, t: (b, 0, 0))],
        out_shape=[jax.ShapeDtypeStruct((m, BRANCH_W), BF16),
                   jax.ShapeDtypeStruct((batch, CONV_W - 1, BRANCH_W), F32)],
        scratch_shapes=[pltpu.VMEM((tm + CONV_PAD, BRANCH_W), F32)] * 2,
        compiler_params=_cparams(2),
        name="branch_b",
    )(z, z, z, wdw, bdw, lng, lnb, wpw, bpw)


SEQS_PER_STEP = 8


def _branch_b_sample_kernel(a_ref, b_ref, g_ref, st_ref, wdw_ref, bdw_ref, lng_ref, lnb_ref, wpw_ref,
                            bpw_ref, o_ref, h_ref, hp_ref, y_ref, *, t_new):
    n_hist = CONV_W - 1
    for s in range(SEQS_PER_STEP):
        h = a_ref[s] * jax.nn.sigmoid(b_ref[s])
        h_ref[s] = h
        hp_ref[s, CONV_PAD:CONV_PAD + SUBLANES, :] = jnp.zeros((SUBLANES, BRANCH_W), F32)
        hp_ref[s, 0:n_hist, :] = st_ref[s]
        hp_ref[s, n_hist:n_hist + t_new, :] = h
        acc = hp_ref[s, pl.ds(0, SUBLANES), :] * wdw_ref[0:1, :]
        for j in range(1, CONV_W):
            acc = acc + hp_ref[s, pl.ds(j, SUBLANES), :] * wdw_ref[j:j + 1, :]
        y_ref[s * SUBLANES:(s + 1) * SUBLANES, :] = _silu(_ln(acc + bdw_ref[...], lng_ref[...], lnb_ref[...]))
    out = _dot(y_ref[...].astype(BF16), wpw_ref[...]) + bpw_ref[...]
    for s in range(SEQS_PER_STEP):
        o_ref[s] = out[s * SUBLANES:s * SUBLANES + t_new, :] * _silu(g_ref[s])


def _branch_b_sample(z3, state, layer, wdw, bdw, lng, lnb, wpw, bpw):
    nb, t_new, _ = z3.shape
    ns = SEQS_PER_STEP
    blk = lambda c: pl.BlockSpec((ns, t_new, BRANCH_W), lambda b, c=c: (b, 0, c))
    vec = pl.BlockSpec((1, BRANCH_W), lambda b: (0, 0))
    row = pl.BlockSpec((ns, t_new, BRANCH_W), lambda b: (b, 0, 0))
    return pl.pallas_call(
        functools.partial(_branch_b_sample_kernel, t_new=t_new),
        grid=(nb // ns,),
        in_specs=[blk(3), blk(4), blk(5),
                  pl.BlockSpec((ns, CONV_W - 1, BRANCH_W), lambda b: (layer * (nb // ns) + b, 0, 0)),
                  pl.BlockSpec((CONV_PAD, BRANCH_W), lambda b: (0, 0)), vec, vec, vec,
                  pl.BlockSpec((BRANCH_W, BRANCH_W), lambda b: (0, 0)), vec],
        out_specs=[row, row],
        out_shape=[jax.ShapeDtypeStruct((nb, t_new, BRANCH_W), F32)] * 2,
        scratch_shapes=[pltpu.VMEM((ns, CONV_PAD + SUBLANES, BRANCH_W), F32),
                        pltpu.VMEM((ns * SUBLANES, BRANCH_W), F32)],
        compiler_params=_cparams(1),
        name="branch_b_sample",
    )(z3, z3, z3, state, wdw, bdw, lng, lnb, wpw, bpw)


def _fox_kernel(q_ref, k_ref, v_ref, c_ref, g_ref, o_ref, vt_ref, s0_ref, s1_ref, m_ref, l_ref, acc_ref, *,
                tq, nk):
    qi = pl.program_id(2)

    @pl.when(qi == 0)
    def _():
        for j in range(nk):
            vt_ref[j] = v_ref[j * tq:(j + 1) * tq, :].astype(F32).T.astype(BF16)

    m_ref[...] = jnp.full_like(m_ref, NEG)
    l_ref[...] = jnp.zeros_like(l_ref)
    acc_ref[...] = jnp.zeros_like(acc_ref)
    qs = (q_ref[...].astype(F32) * (ATT_SCALE * LOG2E)).astype(BF16)

    def rows(kj):
        return pl.ds(pl.multiple_of(kj * tq, tq), tq)

    def scores(kj, s_ref):
        s_ref[...] = _dot_nt(k_ref[rows(kj), :].astype(BF16), qs)

    def softmax_pv(kj, s_ref, diagonal):
        c = c_ref[0, rows(kj), :]
        t = s_ref[...] - jnp.concatenate([c] * (tq // LANE), axis=1)
        if diagonal:
            key = lax.broadcasted_iota(jnp.int32, (tq, tq), 0)
            qry = lax.broadcasted_iota(jnp.int32, (tq, tq), 1)
            t = jnp.where(key <= qry, t, NEG)
        m_prev = m_ref[...]
        m_new = jnp.maximum(m_prev, jnp.max(t, axis=0, keepdims=True))
        alpha = jnp.exp2(m_prev - m_new)
        p = jnp.exp2(t - m_new)
        l_ref[...] = alpha * l_ref[...] + jnp.sum(p, axis=0, keepdims=True)
        acc_ref[...] = alpha * acc_ref[...] + _dot(vt_ref[kj], p.astype(BF16))
        m_ref[...] = m_new

    scores(0, s0_ref)

    def pair(p, carry):
        kj = 2 * p
        scores(kj + 1, s1_ref)
        softmax_pv(kj, s0_ref, False)
        scores(kj + 2, s0_ref)
        softmax_pv(kj + 1, s1_ref, False)
        return carry

    lax.fori_loop(0, lax.shift_right_logical(qi, 1), pair, 0)

    @pl.when((qi & 1) == 0)
    def _():
        softmax_pv(qi, s0_ref, True)

    @pl.when((qi & 1) == 1)
    def _():
        scores(qi, s1_ref)
        softmax_pv(qi - 1, s0_ref, False)
        softmax_pv(qi, s1_ref, True)

    o_ref[...] = ((acc_ref[...] / l_ref[...]).T * _silu(g_ref[...].astype(F32))).astype(o_ref.dtype)


FOX_HEADS_PER_STEP = 4


def _fox_heads_kernel(q_ref, k_ref, v_ref, c_ref, g_ref, o_ref, vt_ref, *scratch, tq, nk):
    for hh in range(FOX_HEADS_PER_STEP):
        cs = pl.ds(hh * HD, HD)
        _fox_kernel(q_ref.at[:, cs], k_ref.at[:, cs], v_ref.at[:, cs], c_ref.at[pl.ds(hh, 1)], g_ref.at[:, cs],
                    o_ref.at[:, cs], vt_ref.at[hh], *scratch, tq=tq, nk=nk)


def _fox_prompt(z, c_rep, *, batch, tq):
    m = z.shape[0]
    seq = m // batch
    nq = seq // tq
    nh = FOX_HEADS_PER_STEP
    wide = nh * HD
    qmap = lambda c: (lambda b, h, qi: (b * nq + qi, c // nh + h))
    kvmap = lambda c: (lambda b, h, qi: (b, c // nh + h))
    return pl.pallas_call(
        functools.partial(_fox_heads_kernel, tq=tq, nk=nq),
        grid=(batch, H_C // nh, nq),
        in_specs=[pl.BlockSpec((tq, wide), qmap(QC)),
                  pl.BlockSpec((seq, wide), kvmap(KC)),
                  pl.BlockSpec((seq, wide), kvmap(VC)),
                  pl.BlockSpec((nh, seq, LANE), lambda b, h, qi: (h, b, 0)),
                  pl.BlockSpec((tq, wide), qmap(GC))],
        out_specs=pl.BlockSpec((tq, wide), lambda b, h, qi: (b * nq + qi, h)),
        out_shape=jax.ShapeDtypeStruct((m, BRANCH_W), BF16),
        scratch_shapes=[pltpu.VMEM((nh, nq, HD, tq), BF16), pltpu.VMEM((tq, tq), F32), pltpu.VMEM((tq, tq), F32),
                        pltpu.VMEM((1, tq), F32), pltpu.VMEM((1, tq), F32), pltpu.VMEM((HD, tq), F32)],
        compiler_params=_cparams(3),
        name="fox_prompt",
    )(z, z, z, c_rep, z)


def _mem_attn_kernel(q_ref, k_ref, v_ref, g_ref, o_ref):
    for h in range(H_C):
        cs = slice(h * HD, (h + 1) * HD)
        s = _dot_nt(q_ref[:, cs].astype(BF16), k_ref[:, cs].astype(BF16)) * ATT_SCALE
        p = jnp.exp(s - jnp.max(s, axis=1, keepdims=True))
        o = _dot(p.astype(BF16), v_ref[:, cs].astype(BF16)) / jnp.sum(p, axis=1, keepdims=True)
        o_ref[:, cs] = (o * _silu(g_ref[:, cs].astype(F32))).astype(o_ref.dtype)


def _mem_attn_prompt(z, mkv, *, batch, tq):
    m = z.shape[0]
    nq = m // batch // tq
    wide = lambda c: pl.BlockSpec((tq, BRANCH_W), lambda b, qi, c=c: (b * nq + qi, c * LANE // BRANCH_W))
    return pl.pallas_call(
        _mem_attn_kernel,
        grid=(batch, nq),
        in_specs=[wide(QM),
                  pl.BlockSpec((N_MEM, BRANCH_W), lambda b, qi: (b, 0)),
                  pl.BlockSpec((N_MEM, BRANCH_W), lambda b, qi: (b, 1)),
                  wide(GM)],
        out_specs=pl.BlockSpec((tq, BRANCH_W), lambda b, qi: (b * nq + qi, 0)),
        out_shape=jax.ShapeDtypeStruct((m, BRANCH_W), BF16),
        compiler_params=_cparams(2),
        name="mem_attn_prompt",
    )(z, mkv, mkv, z)


def _head_match(rows, cols):
    r = lax.broadcasted_iota(jnp.int32, (rows, cols), 0)
    c = lax.broadcasted_iota(jnp.int32, (rows, cols), 1)
    return r, c, (r & (H_C - 1)) == (c & (H_C - 1))


def _mem_attn_sample_kernel(q_ref, g_ref, k_ref, v_ref, o_ref):
    _, _, same = _head_match(q_ref.shape[1], k_ref.shape[1])
    for i in range(SEQS_PER_STEP):
        s = _dot_nt(q_ref[i].astype(BF16), k_ref[i].astype(BF16)) * ATT_SCALE
        s = jnp.where(same, s, NEG)
        p = jnp.exp(s - jnp.max(s, axis=1, keepdims=True))
        o = _dot(p.astype(BF16), v_ref[i].astype(BF16)) / jnp.sum(p, axis=1, keepdims=True)
        o_ref[i] = o * _silu(g_ref[i])


def _mem_attn_sample(q16, g16, mk, mv, layer):
    nb, nr, _ = q16.shape
    nm = mk.shape[1]
    ns = SEQS_PER_STEP
    row = pl.BlockSpec((ns, nr, HD), lambda b: (b, 0, 0))
    mem = pl.BlockSpec((ns, nm, HD), lambda b: (layer * (nb // ns) + b, 0, 0))
    return pl.pallas_call(
        _mem_attn_sample_kernel,
        grid=(nb // ns,),
        in_specs=[row, row, mem, mem],
        out_specs=row,
        out_shape=jax.ShapeDtypeStruct((nb, nr, HD), F32),
        compiler_params=_cparams(1),
        name="mem_attn_sample",
    )(q16, g16, mk, mv)


def _logf_pages_kernel(x_ref, mc_ref, mt_ref, o_ref):
    x = x_ref[...]
    o_ref[:, 0, 0:PAGE_ROWS] = _dot_exact01(x, mc_ref[...])
    o_ref[:, 0, PAGE_ROWS:2 * PAGE_ROWS] = _dot_exact01(x, mt_ref[...])


def _logf_pages(lf_flat, mc, mt, *, tm):
    n = lf_flat.shape[0]
    mat = pl.BlockSpec((PAGE_ROWS, PAGE_ROWS), lambda i: (0, 0))
    return pl.pallas_call(
        _logf_pages_kernel,
        grid=(n // tm,),
        in_specs=[pl.BlockSpec((tm, PAGE_ROWS), lambda i: (i, 0)), mat, mat],
        out_specs=pl.BlockSpec((tm, 1, 2 * PAGE_ROWS), lambda i: (i, 0, 0)),
        out_shape=jax.ShapeDtypeStruct((n, 1, 2 * PAGE_ROWS), F32),
        compiler_params=_cparams(1),
        name="logf_pages",
    )(lf_flat, mc, mt)


def _softmax_update(state, s_list, v_list):
    m_prev, l_prev, acc_prev = state
    m_new = m_prev
    for s in s_list:
        m_new = jnp.maximum(m_new, jnp.max(s, axis=1, keepdims=True))
    alpha = jnp.exp(m_prev - m_new)
    l_new = alpha * l_prev
    acc = alpha * acc_prev
    for s, v in zip(s_list, v_list):
        p = jnp.exp(s - m_new)
        l_new = l_new + jnp.sum(p, axis=1, keepdims=True)
        acc = acc + _dot(p.astype(BF16), v)
    return m_new, l_new, acc


def _fox_paged_work(pt_ref, q_ref, g_ref, kn_ref, vn_ref, lfn_ref, mn_ref, k_hbm, v_hbm, wt_hbm, o_ref,
                    kbuf, vbuf, wtbuf, sem, m_ref, l_ref, acc_ref, carry_ref, *, step, n_steps, steps_per_seq, n_new):
    np_ = PAGES_PER_STEP
    sub = step & (steps_per_seq - 1)
    nr = q_ref.shape[1]

    def slot_copies(slot, page_of):
        copies = []
        for p in range(np_):
            page = page_of(p)
            copies.append(pltpu.make_async_copy(k_hbm.at[page], kbuf.at[slot, p], sem.at[slot]))
            copies.append(pltpu.make_async_copy(v_hbm.at[page], vbuf.at[slot, p], sem.at[slot]))
            copies.append(pltpu.make_async_copy(wt_hbm.at[page], wtbuf.at[slot, p], sem.at[slot]))
        return copies

    def start_step(for_step, slot):
        b = lax.shift_right_logical(for_step, steps_per_seq.bit_length() - 1)
        first_page = (for_step & (steps_per_seq - 1)) * np_
        for c in slot_copies(slot, lambda p: pt_ref[b, first_page + p]):
            c.start()

    def gather_and_scores():
        slot = step & 1

        @pl.when(step == 0)
        def _():
            start_step(step, slot)

        @pl.when(step + 1 < n_steps)
        def _():
            start_step(step + 1, 1 - slot)

        for c in slot_copies(slot, lambda p: 0):
            c.wait()

        first = sub == 0
        q = q_ref[0].astype(BF16)
        state = (jnp.where(first, NEG, m_ref[...]), jnp.where(first, 0.0, l_ref[...]),
                 jnp.where(first, 0.0, acc_ref[...]))
        carry = jnp.where(first, 0.0, carry_ref[...])
        _, _, same = _head_match(nr, PAGE_ROWS)
        s_list, v_list = [], []
        for p in range(np_):
            wt = wtbuf[slot, p]
            ck = carry + wt[:, 0:PAGE_ROWS]
            carry = carry + wt[:, PAGE_ROWS:2 * PAGE_ROWS]
            s = _dot_nt(q, kbuf[slot, p].astype(BF16)) * ATT_SCALE - ck
            s_list.append(jnp.where(same, s, NEG))
            v_list.append(vbuf[slot, p].astype(BF16))

        def softmax_and_values():
            new_state = _softmax_update(state, s_list, v_list)

            def finish():
                m_ref[...], l_ref[...], acc_ref[...] = new_state
                carry_ref[...] = carry

                @pl.when(sub == steps_per_seq - 1)
                def _():
                    r, c, same_n = _head_match(nr, LANE)
                    cn = carry[:, 0:LANE] + _dot_exact01(lfn_ref[0], mn_ref[...])[0:1, :]
                    s = _dot_nt(q, kn_ref[0].astype(BF16)) * ATT_SCALE - cn
                    ok = same_n & (c < n_new * H_C) & ((c >> 2) <= (r >> 2))
                    _, l_fin, acc_fin = _softmax_update(new_state, [jnp.where(ok, s, NEG)],
                                                        [vn_ref[0].astype(BF16)])
                    o_ref[0] = acc_fin / l_fin * _silu(g_ref[0])

            return finish

        return softmax_and_values

    return gather_and_scores


N_FOX_IN = 9


def _in_proj_fox_kernel(pt_ref, x_ref, w_ref, wf_ref, bf_ref, *rest, tm, tn, tiles_per_seq, nj, n_fox_steps,
                        steps_per_seq, n_new):
    fox_in, rest = rest[:N_FOX_IN], rest[N_FOX_IN:]
    (z_ref, lf_ref, ko_ref, vo_ref, c_ref, o_ref), rest = rest[:6], rest[6:]
    (xb_ref, ccarry_ref), fox_scratch = rest[:2], rest[2:]
    step = pl.program_id(0) * nj + pl.program_id(1)
    work = _fox_paged_work(pt_ref, *fox_in, o_ref, *fox_scratch, step=step, n_steps=n_fox_steps,
                           steps_per_seq=steps_per_seq, n_new=n_new)
    _in_proj_kernel(x_ref, w_ref, wf_ref, bf_ref, z_ref, lf_ref, ko_ref, vo_ref, c_ref, xb_ref, ccarry_ref,
                    tm=tm, tn=tn, tiles_per_seq=tiles_per_seq, emit_c=True, head_cols=(KC, VC), extra_work=work,
                    extra_active=step < n_fox_steps)


FUSED_VMEM_LIMIT = 56 * 1024 * 1024


def _in_proj_fox(x, w_main, wf, bfb, layer, pt, q16, g16, kn, vn, lfn, mn, kflat, vflat, wt3, *, tm, tiles_per_seq,
                 n_new):
    m = x.shape[0]
    tn = BRANCH_W
    nj = Z_COLS // tn
    n_seq, nr, _ = q16.shape
    n_pages = pt.shape[1]
    np_ = PAGES_PER_STEP
    steps_per_seq = n_pages // np_
    n_fox_steps = n_seq * steps_per_seq
    assert steps_per_seq * np_ == n_pages and steps_per_seq & (steps_per_seq - 1) == 0
    assert n_fox_steps <= (m // tm) * nj

    def seq_of(i, j):
        return jnp.minimum(i * nj + j, n_fox_steps - 1) // steps_per_seq

    per_seq = lambda rows: pl.BlockSpec((1, rows, HD), lambda i, j, pt: (seq_of(i, j), 0, 0))
    pool = pl.BlockSpec(memory_space=pl.ANY)
    heads = pl.BlockSpec((tm, H_C, HD), lambda i, j, pt: (i, 0, 0))
    grid_spec = pltpu.PrefetchScalarGridSpec(
        num_scalar_prefetch=1,
        grid=(m // tm, nj),
        in_specs=[
            pl.BlockSpec((tm, D_MODEL), lambda i, j, pt: (i, 0), pipeline_mode=pl.Buffered(1)),
            pl.BlockSpec((None, D_MODEL, tn), lambda i, j, pt: (layer, 0, j)),
            pl.BlockSpec((None, D_MODEL, LANE), lambda i, j, pt: (layer, 0, 0)),
            pl.BlockSpec((1, LANE), lambda i, j, pt: (0, 0)),
            per_seq(nr), per_seq(nr), per_seq(LANE), per_seq(LANE),
            pl.BlockSpec((1, 8, LANE), lambda i, j, pt: (seq_of(i, j), 0, 0)),
            pl.BlockSpec((LANE, LANE), lambda i, j, pt: (0, 0)),
            pool, pool, pool,
        ],
        out_specs=[pl.BlockSpec((tm, tn), lambda i, j, pt: (i, j)),
                   pl.BlockSpec((tm, LANE), lambda i, j, pt: (i, 0)), heads, heads,
                   pl.BlockSpec((H_C, tm, LANE), lambda i, j, pt: (0, i, 0)),
                   per_seq(nr)],
        scratch_shapes=[pltpu.VMEM((tm, D_MODEL), BF16), pltpu.VMEM((1, LANE), F32),
                        pltpu.VMEM((2, np_, PAGE_ROWS, HD), F32), pltpu.VMEM((2, np_, PAGE_ROWS, HD), F32),
                        pltpu.VMEM((2, np_, 1, 2 * PAGE_ROWS), F32), pltpu.SemaphoreType.DMA((2,)),
                        pltpu.VMEM((nr, 1), F32), pltpu.VMEM((nr, 1), F32), pltpu.VMEM((nr, HD), F32),
                        pltpu.VMEM((1, PAGE_ROWS), F32)],
    )
    kern = functools.partial(_in_proj_fox_kernel, tm=tm, tn=tn, tiles_per_seq=tiles_per_seq, nj=nj,
                             n_fox_steps=n_fox_steps, steps_per_seq=steps_per_seq, n_new=n_new)
    return pl.pallas_call(
        kern,
        grid_spec=grid_spec,
        out_shape=[jax.ShapeDtypeStruct((m, Z_COLS), BF16), jax.ShapeDtypeStruct((m, LANE), F32),
                   jax.ShapeDtypeStruct((m, H_C, HD), F32), jax.ShapeDtypeStruct((m, H_C, HD), F32),
                   jax.ShapeDtypeStruct((H_C, m, LANE), F32), jax.ShapeDtypeStruct((n_seq, nr, HD), F32)],
        compiler_params=_cparams(2, vmem=FUSED_VMEM_LIMIT),
        name="in_proj_fox",
    )(pt, x, w_main, wf, bfb, q16, g16, kn, vn, lfn, mn, kflat, vflat, wt3)


def _merge_out_kernel(*refs, tm):
    o_refs, wb_ref, gate_refs = refs[0:4], refs[4], refs[5:9]
    wo_ref, x_ref, g_ref, b_ref, y_ref, h_ref = refs[9:15]
    outs = [o_refs[br][...].astype(BF16) for br in range(N_BRANCH)]
    for jc in range(D_MODEL // BRANCH_W):
        cs = slice(jc * BRANCH_W, (jc + 1) * BRANCH_W)
        acc = None
        for br in range(N_BRANCH):
            gate = 0.5 * jnp.tanh(0.5 * gate_refs[br][:, cs].astype(F32)) + 0.5
            term = gate * _dot(outs[br], wb_ref[br, :, cs])
            acc = term if acc is None else acc + term
        h_ref[:, cs] = acc.astype(h_ref.dtype)
    half = max(tm // 2, SUBLANES)
    for s in range(tm // half):
        rs = slice(s * half, (s + 1) * half)
        y = ALPHA * x_ref[rs, :] + _dot(h_ref[rs, :], wo_ref[...])
        y_ref[rs, :] = _ln(y, g_ref[...], b_ref[...])


def _merge_out(outs, z, wb, wo, x, lng, lnb, layer, *, tm):
    m = z.shape[0]
    o_spec = pl.BlockSpec((tm, BRANCH_W), lambda i: (i, 0))
    gate0 = GATE_BLK * BRANCH_W // D_MODEL
    gate = lambda br: pl.BlockSpec((tm, D_MODEL), lambda i, br=br: (i, gate0 + br))
    rows = pl.BlockSpec((tm, D_MODEL), lambda i: (i, 0))
    vec = pl.BlockSpec((1, D_MODEL), lambda i: (0, 0))
    once = pl.Buffered(1)
    return pl.pallas_call(
        functools.partial(_merge_out_kernel, tm=tm),
        grid=(m // tm,),
        in_specs=[o_spec] * 4
                 + [pl.BlockSpec((None, N_BRANCH, BRANCH_W, D_MODEL), lambda i: (layer, 0, 0, 0), pipeline_mode=once)]
                 + [gate(br) for br in range(N_BRANCH)]
                 + [pl.BlockSpec((None, D_MODEL, D_MODEL), lambda i: (layer, 0, 0), pipeline_mode=once),
                    rows, vec, vec],
        out_specs=rows,
        out_shape=jax.ShapeDtypeStruct((m, D_MODEL), F32),
        scratch_shapes=[pltpu.VMEM((tm, D_MODEL), BF16)],
        compiler_params=_cparams(1),
        name="merge_out",
    )(*outs, wb, z, z, z, z, wo, x, lng, lnb)


def _tok_head_matrices():
    i = jnp.arange(PAGE_ROWS)
    same = (i[:, None] % H_C) == (i[None, :] % H_C)
    mc = (same & (i[:, None] // H_C <= i[None, :] // H_C)).astype(BF16)
    mt = same.astype(BF16)
    return mc, mt


def kernel(x_prompt, x_sample, mem_prompt, cache_k, cache_v, cache_logf, cache_mem_k, cache_mem_v, state_conv,
           page_table, w_in, w_mem_k, w_mem_v, ln_v_g, ln_v_b, w_s, b_s, w_dw, b_dw, ln_c_g, ln_c_b, w_pw, b_pw,
           b_f, w_branch, w_out, ln_g, ln_b):
    bp, seq, _ = x_prompt.shape
    db, t_new, _ = x_sample.shape
    n_pool = cache_k.shape[1]

    w_main, wf = _prep_w(w_in)
    bfb = jnp.pad(b_f, ((0, 0), (0, LANE - H_C)))[:, None, :]
    wb = w_branch.astype(BF16)
    wo = w_out.astype(BF16)
    wpw = w_pw.astype(BF16)
    wmkv = jnp.concatenate([w_mem_k, w_mem_v], axis=2).astype(BF16)
    wdw = jnp.pad(w_dw, ((0, 0), (0, CONV_PAD - CONV_W), (0, 0)))
    vec = lambda a: a[:, None, :]
    ln_v_g, ln_v_b, b_dw, ln_c_g, ln_c_b, b_pw, ln_g, ln_b = map(
        vec, (ln_v_g, ln_v_b, b_dw, ln_c_g, ln_c_b, b_pw, ln_g, ln_b))

    idx = jnp.arange(CHUNK)
    mask_p = (idx[None, :] <= idx[:, None]).astype(F32)
    bsb_p = jnp.broadcast_to(b_s[:, :, :, None], (DEPTH, A_GROUPS, CHUNK, CHUNK))
    reps = CHUNK // t_new
    mask_s = ((idx[:, None] // t_new == idx[None, :] // t_new) & (idx[None, :] <= idx[:, None])).astype(F32)
    pick = (idx[:, None] % t_new == jnp.arange(t_new)[None, :]).astype(F32)
    ws_s = jnp.einsum("rt,lgts,cs->lgrc", pick, w_s[:, :, :t_new, :t_new], pick, precision=lax.Precision.HIGHEST)
    bsb_s = jnp.broadcast_to(jnp.tile(b_s[:, :, :t_new], (1, 1, reps))[:, :, :, None],
                             (DEPTH, A_GROUPS, CHUNK, CHUNK))

    kflat = cache_k.reshape(DEPTH * n_pool, PAGE_ROWS, HD)
    vflat = cache_v.reshape(DEPTH * n_pool, PAGE_ROWS, HD)
    mc, mt = _tok_head_matrices()
    wt3 = _logf_pages(cache_logf.reshape(DEPTH * n_pool, PAGE_ROWS), mc, mt, tm=512)
    mn = jnp.pad(mc[:t_new * H_C, :t_new * H_C], ((0, LANE - t_new * H_C),) * 2)
    memk = cache_mem_k.reshape(DEPTH * db, N_MEM * H_C, HD)
    memv = cache_mem_v.reshape(DEPTH * db, N_MEM * H_C, HD)
    state = state_conv.reshape(DEPTH * db, CONV_W - 1, BRANCH_W)

    xp = x_prompt.reshape(bp * seq, D_MODEL)
    xs = x_sample.reshape(db * t_new, D_MODEL)
    mem2d = mem_prompt.reshape(bp * N_MEM, D_MODEL)
    m_s = db * t_new
    outs = [[] for _ in range(11)]
    for l in range(DEPTH):
        zs, lfs, *slabs = _in_proj(xs, w_main, wf, bfb[l], l, tm=m_s, tiles_per_seq=1, emit_c=False, z_dtype=F32,
                                   head_cols=(QC, KC, VC, GC, QM, GM))
        q_s, k_s, v_s, gc_s, qm_s, gm_s = (a.reshape(db, t_new * H_C, HD) for a in slabs)
        pad_new = lambda a: jnp.pad(a, ((0, 0), (0, LANE - t_new * H_C), (0, 0)))
        lfn = lfs[:, :H_C].reshape(db, 1, t_new * H_C)
        lfn = jnp.pad(lfn, ((0, 0), (0, 7), (0, LANE - t_new * H_C)))

        z, lf, k_p, v_p, c_rep, oc_s = _in_proj_fox(
            xp, w_main, wf, bfb[l], l, page_table + l * n_pool, q_s, gc_s, pad_new(k_s), pad_new(v_s),
            lfn, mn, kflat, vflat, wt3, tm=1024, tiles_per_seq=seq // 1024, n_new=t_new)
        mkv = _mm(mem2d, wmkv[l], tm=bp * N_MEM, tn=2 * BRANCH_W)
        (oa,) = _branch_a(z, ln_v_g[l], ln_v_b[l], w_s[l], bsb_p[l], mask_p, tm=512, emit_v=False)
        ob, nconv = _branch_b(z, wdw[l], b_dw[l], ln_c_g[l], ln_c_b[l], wpw[l], b_pw[l], batch=bp, tm=512)
        oc = _fox_prompt(z, c_rep, batch=bp, tq=512)
        om = _mem_attn_prompt(z, mkv, batch=bp, tq=512)
        xp = _merge_out((oa, ob, oc, om), z, wb, wo, xp, ln_g[l], ln_b[l], l, tm=256)
        outs[0].append(k_p.reshape(bp, seq, H_C, HD))
        outs[1].append(v_p.reshape(bp, seq, H_C, HD))
        outs[2].append(lf[:, :H_C].reshape(bp, seq, H_C))
        outs[3].append(nconv)
        outs[4].append(mkv[:, :BRANCH_W].reshape(bp, N_MEM, H_C, HD))
        outs[5].append(mkv[:, BRANCH_W:].reshape(bp, N_MEM, H_C, HD))

        oa_s, v_rows = _branch_a(zs, ln_v_g[l], ln_v_b[l], ws_s[l], bsb_s[l], mask_s, tm=m_s, emit_v=True)
        ob_s, h_glu = _branch_b_sample(zs.reshape(db, t_new, Z_COLS), state, l, wdw[l], b_dw[l], ln_c_g[l],
                                       ln_c_b[l], wpw[l], b_pw[l])
        om_s = _mem_attn_sample(qm_s, gm_s, memk, memv, l)
        flat = lambda a: a.reshape(m_s, BRANCH_W)
        xs = _merge_out((oa_s, flat(ob_s), flat(oc_s), flat(om_s)), zs, wb, wo, xs, ln_g[l], ln_b[l], l, tm=m_s)
        outs[6].append(k_s.reshape(db, t_new, H_C, HD))
        outs[7].append(v_s.reshape(db, t_new, H_C, HD))
        outs[8].append(lfs[:, :H_C].reshape(db, t_new, H_C))
        outs[9].append(jnp.concatenate([state_conv[l][:, t_new:], h_glu], axis=1))
        outs[10].append(v_rows.reshape(db, t_new, BRANCH_W))

    return (xp.reshape(bp, seq, D_MODEL), xs.reshape(db, t_new, D_MODEL)) + tuple(jnp.stack(o) for o in outs)
```

```python
import functools
import math

import jax
import jax.numpy as jnp
from jax import lax
from jax.experimental import pallas as pl
from jax.experimental.pallas import tpu as pltpu

F32 = jnp.float32
BF16 = jnp.bfloat16

D_MODEL = 2048
DEPTH = 2
BRANCH_W = 512
N_BRANCH = 4
CHUNK = 128
A_GROUPS = 4
CONV_W = 31
H_C = 4
HD = 128
N_MEM = 256
PAGE_SIZE = 128
LN_EPS = 1e-5
ALPHA = (2 * DEPTH) ** 0.25
ATT_SCALE = HD ** -0.5
LOG2E = math.log2(math.e)
NEG = -1e30

LANE = 128
SUBLANES = 8
PAGE_ROWS = PAGE_SIZE * H_C
F_COL = 9 * BRANCH_W
Z_COLS = 28 * BRANCH_W
QC, KC, VC, GC, QM, GM = 24, 28, 32, 36, 40, 44
GATE_BLK = 12
PAGES_PER_STEP = 16
VMEM_LIMIT = 48 * 1024 * 1024


def _cparams(n_axes, vmem=VMEM_LIMIT):
    return pltpu.CompilerParams(dimension_semantics=("arbitrary",) * n_axes, vmem_limit_bytes=vmem)


def _ln(x, g, b):
    mu = jnp.mean(x, axis=-1, keepdims=True)
    xc = x - mu
    var = jnp.mean(xc * xc, axis=-1, keepdims=True)
    return xc * lax.rsqrt(var + LN_EPS) * g + b


def _silu(x):
    return x * jax.nn.sigmoid(x)


def _log_sigmoid(x):
    return jnp.minimum(x, 0.0) - jnp.log1p(jnp.exp(-jnp.abs(x)))


def _dot(a, b):
    return jnp.dot(a, b, preferred_element_type=F32)


def _dot_nt(a, b):
    return lax.dot_general(a, b, (((1,), (1,)), ((), ())), preferred_element_type=F32)


def _split3(x):
    hi = x.astype(BF16)
    r = x - hi.astype(F32)
    mid = r.astype(BF16)
    lo = (r - mid.astype(F32)).astype(BF16)
    return hi, mid, lo


def _dot_exact01(x, m01):
    hi, mid, lo = _split3(x)
    return _dot(hi, m01) + _dot(mid, m01) + _dot(lo, m01)


def _dot_exact01_left(m01, x):
    hi, mid, lo = _split3(x)
    return _dot(m01, hi) + _dot(m01, mid) + _dot(m01, lo)


def _prep_w_kernel(a_ref, f_ref, o_ref, wf_ref):
    j = pl.program_id(0)
    for l in range(DEPTH):
        o_ref[l] = a_ref[:, l, :].T.astype(BF16)

    @pl.when(j == 0)
    def _():
        lane = lax.broadcasted_iota(jnp.int32, (D_MODEL, LANE), 1)
        for l in range(DEPTH):
            wf_ref[l] = jnp.where(lane < H_C, f_ref[:, l, :].T, 0.0).astype(BF16)


def _prep_w(w_in):
    wt = jnp.transpose(w_in, (2, 0, 1))
    elems = lambda rows: (pl.Element(rows), pl.Element(DEPTH), pl.Element(D_MODEL))
    tc = BRANCH_W
    return pl.pallas_call(
        _prep_w_kernel,
        grid=(Z_COLS // tc,),
        in_specs=[pl.BlockSpec(elems(tc), lambda j: (j * tc + jnp.where(j >= F_COL // tc, H_C, 0), 0, 0)),
                  pl.BlockSpec(elems(LANE), lambda j: (F_COL, 0, 0))],
        out_specs=[pl.BlockSpec((DEPTH, D_MODEL, tc), lambda j: (0, 0, j)),
                   pl.BlockSpec((DEPTH, D_MODEL, LANE), lambda j: (0, 0, 0))],
        out_shape=[jax.ShapeDtypeStruct((DEPTH, D_MODEL, Z_COLS), BF16),
                   jax.ShapeDtypeStruct((DEPTH, D_MODEL, LANE), BF16)],
        compiler_params=_cparams(1),
        name="prep_w",
    )(wt, wt)


def _in_proj_kernel(x_ref, w_ref, wf_ref, bf_ref, z_ref, lf_ref, *rest, tm, tn, tiles_per_seq, emit_c, head_cols,
                    extra_work=None, extra_active=None):
    head_refs, rest = rest[:len(head_cols)], rest[len(head_cols):]
    if emit_c:
        c_ref, xb_ref, carry_ref = rest
    else:
        (xb_ref,) = rest
    i = pl.program_id(0)
    j = pl.program_id(1)

    @pl.when(j == 0)
    def _():
        xb = x_ref[...].astype(BF16)
        xb_ref[...] = xb
        lf = _log_sigmoid(_dot(xb, wf_ref[...]) + bf_ref[...])
        lf_ref[...] = lf
        if emit_c:
            @pl.when(lax.rem(i, tiles_per_seq) == 0)
            def _():
                carry_ref[...] = jnp.zeros_like(carry_ref)

            row = lax.broadcasted_iota(jnp.int32, (LANE, LANE), 0)
            col = lax.broadcasted_iota(jnp.int32, (LANE, LANE), 1)
            lower = jnp.where(col <= row, 1.0, 0.0).astype(BF16)
            carry = carry_ref[...]
            for r in range(tm // LANE):
                rs = slice(r * LANE, (r + 1) * LANE)
                cblk = _dot_exact01_left(lower, lf[rs, :]) + carry
                carry = cblk[LANE - 1:LANE, :]
                c2 = cblk * LOG2E
                for h in range(H_C):
                    c_ref[h, rs, :] = jnp.broadcast_to(c2[:, h:h + 1], (LANE, LANE))
            carry_ref[...] = carry

    def main(extra_work):
        after_matmul = extra_work() if extra_work is not None else None
        acc = _dot(xb_ref[...], w_ref[...])
        z_ref[...] = acc.astype(z_ref.dtype)
        finish = after_matmul() if after_matmul is not None else None

        def heads_out(o_ref, col0):
            for h in range(H_C):
                o_ref[:, h, :] = acc[:, col0 + h * HD:col0 + (h + 1) * HD]

        for col, o_ref in zip(head_cols, head_refs):
            pl.when(j == col * LANE // tn)(functools.partial(heads_out, o_ref, col * LANE % tn))

        if finish is not None:
            finish()

    if extra_work is None:
        main(None)
    else:
        @pl.when(extra_active)
        def _():
            main(extra_work)

        @pl.when(jnp.logical_not(extra_active))
        def _():
            main(None)


def _in_proj(x, w_main, wf, bfb, layer, *, tm, tiles_per_seq, emit_c, z_dtype, head_cols):
    m = x.shape[0]
    tn = 2 * BRANCH_W
    kern = functools.partial(_in_proj_kernel, tm=tm, tn=tn, tiles_per_seq=tiles_per_seq, emit_c=emit_c,
                             head_cols=head_cols)
    heads = pl.BlockSpec((tm, H_C, HD), lambda i, j: (i, 0, 0))
    out_specs = [pl.BlockSpec((tm, tn), lambda i, j: (i, j)),
                 pl.BlockSpec((tm, LANE), lambda i, j: (i, 0))] + [heads] * len(head_cols)
    out_shape = ([jax.ShapeDtypeStruct((m, Z_COLS), z_dtype), jax.ShapeDtypeStruct((m, LANE), F32)]
                 + [jax.ShapeDtypeStruct((m, H_C, HD), F32)] * len(head_cols))
    scratch = [pltpu.VMEM((tm, D_MODEL), BF16)]
    if emit_c:
        out_specs.append(pl.BlockSpec((H_C, tm, LANE), lambda i, j: (0, i, 0)))
        out_shape.append(jax.ShapeDtypeStruct((H_C, m, LANE), F32))
        scratch.append(pltpu.VMEM((1, LANE), F32))
    return pl.pallas_call(
        kern,
        grid=(m // tm, Z_COLS // tn),
        in_specs=[
            pl.BlockSpec((tm, D_MODEL), lambda i, j: (i, 0), pipeline_mode=pl.Buffered(1)),
            pl.BlockSpec((None, D_MODEL, tn), lambda i, j: (layer, 0, j)),
            pl.BlockSpec((None, D_MODEL, LANE), lambda i, j: (layer, 0, 0)),
            pl.BlockSpec((1, LANE), lambda i, j: (0, 0)),
        ],
        out_specs=out_specs,
        out_shape=out_shape,
        scratch_shapes=scratch,
        compiler_params=_cparams(2),
        name="in_proj",
    )(x, w_main, wf, bfb)


def _mm_kernel(x_ref, w_ref, o_ref):
    o_ref[...] = _dot(x_ref[...].astype(BF16), w_ref[...])


def _mm(x, w, *, tm, tn):
    m, k = x.shape
    n = w.shape[1]
    return pl.pallas_call(
        _mm_kernel,
        grid=(m // tm, n // tn),
        in_specs=[pl.BlockSpec((tm, k), lambda i, j: (i, 0)), pl.BlockSpec((k, tn), lambda i, j: (0, j))],
        out_specs=pl.BlockSpec((tm, tn), lambda i, j: (i, j)),
        out_shape=jax.ShapeDtypeStruct((m, n), F32),
        compiler_params=_cparams(2),
        name="mem_kv_proj",
    )(x, w)


def _branch_a_kernel(u_ref, v_ref, g_ref, lng_ref, lnb_ref, ws_ref, bsb_ref, mask_ref, o_ref, *rest,
                     tm, emit_v):
    u = jax.nn.gelu(u_ref[...].astype(F32))
    v = _ln(jax.nn.gelu(v_ref[...].astype(F32)), lng_ref[...], lnb_ref[...])
    if emit_v:
        rest[0][...] = v
    gate = _silu(g_ref[...].astype(F32))
    keep = mask_ref[...] > 0.0
    for g in range(A_GROUPS):
        wg = jnp.where(keep, ws_ref[g], 0.0).astype(BF16)
        cs = slice(g * LANE, (g + 1) * LANE)
        for c in range(tm // CHUNK):
            rs = slice(c * CHUNK, (c + 1) * CHUNK)
            s = _dot(wg, v[rs, cs].astype(BF16)) + bsb_ref[g]
            o_ref[rs, cs] = (u[rs, cs] * s * gate[rs, cs]).astype(o_ref.dtype)


def _branch_a(z, lng, lnb, ws, bsb, mask, *, tm, emit_v):
    m = z.shape[0]
    blk = lambda c: pl.BlockSpec((tm, BRANCH_W), lambda i, c=c: (i, c))
    vec = pl.BlockSpec((1, BRANCH_W), lambda i: (0, 0))
    cube = pl.BlockSpec((A_GROUPS, CHUNK, CHUNK), lambda i: (0, 0, 0))
    out_specs = [pl.BlockSpec((tm, BRANCH_W), lambda i: (i, 0))]
    out_shape = [jax.ShapeDtypeStruct((m, BRANCH_W), BF16)]
    if emit_v:
        out_specs.append(pl.BlockSpec((tm, BRANCH_W), lambda i: (i, 0)))
        out_shape.append(jax.ShapeDtypeStruct((m, BRANCH_W), F32))
    return pl.pallas_call(
        functools.partial(_branch_a_kernel, tm=tm, emit_v=emit_v),
        grid=(m // tm,),
        in_specs=[blk(0), blk(1), blk(2), vec, vec, cube, cube,
                  pl.BlockSpec((CHUNK, CHUNK), lambda i: (0, 0))],
        out_specs=out_specs,
        out_shape=out_shape,
        compiler_params=_cparams(1),
        name="branch_a",
    )(z, z, z, lng, lnb, ws, bsb, mask)


CONV_PAD = 32


def _conv_tail(y, bdw_ref, lng_ref, lnb_ref, wpw_ref, bpw_ref, gate):
    y = _silu(_ln(y + bdw_ref[...], lng_ref[...], lnb_ref[...]))
    return (_dot(y.astype(BF16), wpw_ref[...]) + bpw_ref[...]) * _silu(gate)


def _branch_b_kernel(a_ref, b_ref, g_ref, wdw_ref, bdw_ref, lng_ref, lnb_ref, wpw_ref, bpw_ref,
                     o_ref, nc_ref, hp_ref, sw_ref, *, tm):
    t = pl.program_id(1)

    @pl.when(t == 0)
    def _():
        hp_ref[0:CONV_PAD, :] = jnp.zeros((CONV_PAD, BRANCH_W), F32)

    hp_ref[CONV_PAD:CONV_PAD + tm, :] = a_ref[...].astype(F32) * jax.nn.sigmoid(b_ref[...].astype(F32))
    off = CONV_PAD - (CONV_W - 1)
    acc = None
    for r in range(SUBLANES):
        taps = [j for j in range(CONV_W) if (off + j) % SUBLANES == r]
        rows = tm + taps[-1] - taps[0]
        if r:
            sw_ref[0:rows, :] = hp_ref[pl.ds(off + taps[0], rows), :]
        for j in taps:
            src = sw_ref[j - taps[0]:j - taps[0] + tm, :] if r else hp_ref[off + j:off + j + tm, :]
            term = src * wdw_ref[j:j + 1, :]
            acc = term if acc is None else acc + term
    gate = g_ref[...].astype(F32)
    o_ref[...] = _conv_tail(acc, bdw_ref, lng_ref, lnb_ref, wpw_ref, bpw_ref, gate).astype(o_ref.dtype)
    nc_ref[0] = hp_ref[pl.ds(CONV_PAD + tm - (CONV_W - 1), CONV_W - 1), :]
    hp_ref[0:CONV_PAD, :] = hp_ref[tm:tm + CONV_PAD, :]


def _branch_b(z, wdw, bdw, lng, lnb, wpw, bpw, *, batch, tm):
    m = z.shape[0]
    nt = m // batch // tm
    blk = lambda c: pl.BlockSpec((tm, BRANCH_W), lambda b, t, c=c: (b * nt + t, c))
    vec = pl.BlockSpec((1, BRANCH_W), lambda b, t: (0, 0))
    return pl.pallas_call(
        functools.partial(_branch_b_kernel, tm=tm),
        grid=(batch, nt),
        in_specs=[blk(3), blk(4), blk(5),
                  pl.BlockSpec((CONV_PAD, BRANCH_W), lambda b, t: (0, 0)), vec, vec, vec,
                  pl.BlockSpec((BRANCH_W, BRANCH_W), lambda b, t: (0, 0)), vec],
        out_specs=[pl.BlockSpec((tm, BRANCH_W), lambda b, t: (b * nt + t, 0)),
                   pl.BlockSpec((1, CONV_W - 1, BRANCH_W), lambda b, t: (b, 0, 0))],
        out_shape=[jax.ShapeDtypeStruct((m, BRANCH_W), BF16),
                   jax.ShapeDtypeStruct((batch, CONV_W - 1, BRANCH_W), F32)],
        scratch_shapes=[pltpu.VMEM((tm + CONV_PAD, BRANCH_W), F32)] * 2,
        compiler_params=_cparams(2),
        name="branch_b",
    )(z, z, z, wdw, bdw, lng, lnb, wpw, bpw)


SEQS_PER_STEP = 8


def _branch_b_sample_kernel(a_ref, b_ref, g_ref, st_ref, wdw_ref, bdw_ref, lng_ref, lnb_ref, wpw_ref,
                            bpw_ref, o_ref, h_ref, hp_ref, y_ref, *, t_new):
    n_hist = CONV_W - 1
    for s in range(SEQS_PER_STEP):
        h = a_ref[s] * jax.nn.sigmoid(b_ref[s])
        h_ref[s] = h
        hp_ref[s, CONV_PAD:CONV_PAD + SUBLANES, :] = jnp.zeros((SUBLANES, BRANCH_W), F32)
        hp_ref[s, 0:n_hist, :] = st_ref[s]
        hp_ref[s, n_hist:n_hist + t_new, :] = h
        acc = hp_ref[s, pl.ds(0, SUBLANES), :] * wdw_ref[0:1, :]
        for j in range(1, CONV_W):
            acc = acc + hp_ref[s, pl.ds(j, SUBLANES), :] * wdw_ref[j:j + 1, :]
        y_ref[s * SUBLANES:(s + 1) * SUBLANES, :] = _silu(_ln(acc + bdw_ref[...], lng_ref[...], lnb_ref[...]))
    out = _dot(y_ref[...].astype(BF16), wpw_ref[...]) + bpw_ref[...]
    for s in range(SEQS_PER_STEP):
        o_ref[s] = out[s * SUBLANES:s * SUBLANES + t_new, :] * _silu(g_ref[s])


def _branch_b_sample(z3, state, layer, wdw, bdw, lng, lnb, wpw, bpw):
    nb, t_new, _ = z3.shape
    ns = SEQS_PER_STEP
    blk = lambda c: pl.BlockSpec((ns, t_new, BRANCH_W), lambda b, c=c: (b, 0, c))
    vec = pl.BlockSpec((1, BRANCH_W), lambda b: (0, 0))
    row = pl.BlockSpec((ns, t_new, BRANCH_W), lambda b: (b, 0, 0))
    return pl.pallas_call(
        functools.partial(_branch_b_sample_kernel, t_new=t_new),
        grid=(nb // ns,),
        in_specs=[blk(3), blk(4), blk(5),
                  pl.BlockSpec((ns, CONV_W - 1, BRANCH_W), lambda b: (layer * (nb // ns) + b, 0, 0)),
                  pl.BlockSpec((CONV_PAD, BRANCH_W), lambda b: (0, 0)), vec, vec, vec,
                  pl.BlockSpec((BRANCH_W, BRANCH_W), lambda b: (0, 0)), vec],
        out_specs=[row, row],
        out_shape=[jax.ShapeDtypeStruct((nb, t_new, BRANCH_W), F32)] * 2,
        scratch_shapes=[pltpu.VMEM((ns, CONV_PAD + SUBLANES, BRANCH_W), F32),
                        pltpu.VMEM((ns * SUBLANES, BRANCH_W), F32)],
        compiler_params=_cparams(1),
        name="branch_b_sample",
    )(z3, z3, z3, state, wdw, bdw, lng, lnb, wpw, bpw)


def _fox_kernel(q_ref, k_ref, v_ref, c_ref, g_ref, o_ref, vt_ref, s0_ref, s1_ref, m_ref, l_ref, acc_ref, *,
                tq, nk):
    qi = pl.program_id(2)

    @pl.when(qi == 0)
    def _():
        for j in range(nk):
            vt_ref[j] = v_ref[j * tq:(j + 1) * tq, :].astype(F32).T.astype(BF16)

    m_ref[...] = jnp.full_like(m_ref, NEG)
    l_ref[...] = jnp.zeros_like(l_ref)
    acc_ref[...] = jnp.zeros_like(acc_ref)
    qs = (q_ref[...].astype(F32) * (ATT_SCALE * LOG2E)).astype(BF16)

    def rows(kj):
        return pl.ds(pl.multiple_of(kj * tq, tq), tq)

    def scores(kj, s_ref):
        s_ref[...] = _dot_nt(k_ref[rows(kj), :].astype(BF16), qs)

    def softmax_pv(kj, s_ref, diagonal):
        c = c_ref[0, rows(kj), :]
        t = s_ref[...] - jnp.concatenate([c] * (tq // LANE), axis=1)
        if diagonal:
            key = lax.broadcasted_iota(jnp.int32, (tq, tq), 0)
            qry = lax.broadcasted_iota(jnp.int32, (tq, tq), 1)
            t = jnp.where(key <= qry, t, NEG)
        m_prev = m_ref[...]
        m_new = jnp.maximum(m_prev, jnp.max(t, axis=0, keepdims=True))
        alpha = jnp.exp2(m_prev - m_new)
        p = jnp.exp2(t - m_new)
        l_ref[...] = alpha * l_ref[...] + jnp.sum(p, axis=0, keepdims=True)
        acc_ref[...] = alpha * acc_ref[...] + _dot(vt_ref[kj], p.astype(BF16))
        m_ref[...] = m_new

    scores(0, s0_ref)

    def pair(p, carry):
        kj = 2 * p
        scores(kj + 1, s1_ref)
        softmax_pv(kj, s0_ref, False)
        scores(kj + 2, s0_ref)
        softmax_pv(kj + 1, s1_ref, False)
        return carry

    lax.fori_loop(0, lax.shift_right_logical(qi, 1), pair, 0)

    @pl.when((qi & 1) == 0)
    def _():
        softmax_pv(qi, s0_ref, True)

    @pl.when((qi & 1) == 1)
    def _():
        scores(qi, s1_ref)
        softmax_pv(qi - 1, s0_ref, False)
        softmax_pv(qi, s1_ref, True)

    o_ref[...] = ((acc_ref[...] / l_ref[...]).T * _silu(g_ref[...].astype(F32))).astype(o_ref.dtype)


FOX_HEADS_PER_STEP = 4


def _fox_heads_kernel(q_ref, k_ref, v_ref, c_ref, g_ref, o_ref, vt_ref, *scratch, tq, nk):
    for hh in range(FOX_HEADS_PER_STEP):
        cs = pl.ds(hh * HD, HD)
        _fox_kernel(q_ref.at[:, cs], k_ref.at[:, cs], v_ref.at[:, cs], c_ref.at[pl.ds(hh, 1)], g_ref.at[:, cs],
                    o_ref.at[:, cs], vt_ref.at[hh], *scratch, tq=tq, nk=nk)


def _fox_prompt(z, c_rep, *, batch, tq):
    m = z.shape[0]
    seq = m // batch
    nq = seq // tq
    nh = FOX_HEADS_PER_STEP
    wide = nh * HD
    qmap = lambda c: (lambda b, h, qi: (b * nq + qi, c // nh + h))
    kvmap = lambda c: (lambda b, h, qi: (b, c // nh + h))
    return pl.pallas_call(
        functools.partial(_fox_heads_kernel, tq=tq, nk=nq),
        grid=(batch, H_C // nh, nq),
        in_specs=[pl.BlockSpec((tq, wide), qmap(QC)),
                  pl.BlockSpec((seq, wide), kvmap(KC)),
                  pl.BlockSpec((seq, wide), kvmap(VC)),
                  pl.BlockSpec((nh, seq, LANE), lambda b, h, qi: (h, b, 0)),
                  pl.BlockSpec((tq, wide), qmap(GC))],
        out_specs=pl.BlockSpec((tq, wide), lambda b, h, qi: (b * nq + qi, h)),
        out_shape=jax.ShapeDtypeStruct((m, BRANCH_W), BF16),
        scratch_shapes=[pltpu.VMEM((nh, nq, HD, tq), BF16), pltpu.VMEM((tq, tq), F32), pltpu.VMEM((tq, tq), F32),
                        pltpu.VMEM((1, tq), F32), pltpu.VMEM((1, tq), F32), pltpu.VMEM((HD, tq), F32)],
        compiler_params=_cparams(3),
        name="fox_prompt",
    )(z, z, z, c_rep, z)


def _mem_attn_kernel(q_ref, k_ref, v_ref, g_ref, o_ref):
    for h in range(H_C):
        cs = slice(h * HD, (h + 1) * HD)
        s = _dot_nt(q_ref[:, cs].astype(BF16), k_ref[:, cs].astype(BF16)) * ATT_SCALE
        p = jnp.exp(s - jnp.max(s, axis=1, keepdims=True))
        o = _dot(p.astype(BF16), v_ref[:, cs].astype(BF16)) / jnp.sum(p, axis=1, keepdims=True)
        o_ref[:, cs] = (o * _silu(g_ref[:, cs].astype(F32))).astype(o_ref.dtype)


def _mem_attn_prompt(z, mkv, *, batch, tq):
    m = z.shape[0]
    nq = m // batch // tq
    wide = lambda c: pl.BlockSpec((tq, BRANCH_W), lambda b, qi, c=c: (b * nq + qi, c * LANE // BRANCH_W))
    return pl.pallas_call(
        _mem_attn_kernel,
        grid=(batch, nq),
        in_specs=[wide(QM),
                  pl.BlockSpec((N_MEM, BRANCH_W), lambda b, qi: (b, 0)),
                  pl.BlockSpec((N_MEM, BRANCH_W), lambda b, qi: (b, 1)),
                  wide(GM)],
        out_specs=pl.BlockSpec((tq, BRANCH_W), lambda b, qi: (b * nq + qi, 0)),
        out_shape=jax.ShapeDtypeStruct((m, BRANCH_W), BF16),
        compiler_params=_cparams(2),
        name="mem_attn_prompt",
    )(z, mkv, mkv, z)


def _head_match(rows, cols):
    r = lax.broadcasted_iota(jnp.int32, (rows, cols), 0)
    c = lax.broadcasted_iota(jnp.int32, (rows, cols), 1)
    return r, c, (r & (H_C - 1)) == (c & (H_C - 1))


def _mem_attn_sample_kernel(q_ref, g_ref, k_ref, v_ref, o_ref):
    _, _, same = _head_match(q_ref.shape[1], k_ref.shape[1])
    for i in range(SEQS_PER_STEP):
        s = _dot_nt(q_ref[i].astype(BF16), k_ref[i].astype(BF16)) * ATT_SCALE
        s = jnp.where(same, s, NEG)
        p = jnp.exp(s - jnp.max(s, axis=1, keepdims=True))
        o = _dot(p.astype(BF16), v_ref[i].astype(BF16)) / jnp.sum(p, axis=1, keepdims=True)
        o_ref[i] = o * _silu(g_ref[i])


def _mem_attn_sample(q16, g16, mk, mv, layer):
    nb, nr, _ = q16.shape
    nm = mk.shape[1]
    ns = SEQS_PER_STEP
    row = pl.BlockSpec((ns, nr, HD), lambda b: (b, 0, 0))
    mem = pl.BlockSpec((ns, nm, HD), lambda b: (layer * (nb // ns) + b, 0, 0))
    return pl.pallas_call(
        _mem_attn_sample_kernel,
        grid=(nb // ns,),
        in_specs=[row, row, mem, mem],
        out_specs=row,
        out_shape=jax.ShapeDtypeStruct((nb, nr, HD), F32),
        compiler_params=_cparams(1),
        name="mem_attn_sample",
    )(q16, g16, mk, mv)


def _logf_pages_kernel(x_ref, mc_ref, mt_ref, o_ref):
    x = x_ref[...]
    o_ref[:, 0, 0:PAGE_ROWS] = _dot_exact01(x, mc_ref[...])
    o_ref[:, 0, PAGE_ROWS:2 * PAGE_ROWS] = _dot_exact01(x, mt_ref[...])


def _logf_pages(lf_flat, mc, mt, *, tm):
    n = lf_flat.shape[0]
    mat = pl.BlockSpec((PAGE_ROWS, PAGE_ROWS), lambda i: (0, 0))
    return pl.pallas_call(
        _logf_pages_kernel,
        grid=(n // tm,),
        in_specs=[pl.BlockSpec((tm, PAGE_ROWS), lambda i: (i, 0)), mat, mat],
        out_specs=pl.BlockSpec((tm, 1, 2 * PAGE_ROWS), lambda i: (i, 0, 0)),
        out_shape=jax.ShapeDtypeStruct((n, 1, 2 * PAGE_ROWS), F32),
        compiler_params=_cparams(1),
        name="logf_pages",
    )(lf_flat, mc, mt)


def _softmax_update(state, s_list, v_list):
    m_prev, l_prev, acc_prev = state
    m_new = m_prev
    for s in s_list:
        m_new = jnp.maximum(m_new, jnp.max(s, axis=1, keepdims=True))
    alpha = jnp.exp(m_prev - m_new)
    l_new = alpha * l_prev
    acc = alpha * acc_prev
    for s, v in zip(s_list, v_list):
        p = jnp.exp(s - m_new)
        l_new = l_new + jnp.sum(p, axis=1, keepdims=True)
        acc = acc + _dot(p.astype(BF16), v)
    return m_new, l_new, acc


def _fox_paged_work(pt_ref, q_ref, g_ref, kn_ref, vn_ref, lfn_ref, mn_ref, k_hbm, v_hbm, wt_hbm, o_ref,
                    kbuf, vbuf, wtbuf, sem, m_ref, l_ref, acc_ref, carry_ref, *, step, n_steps, steps_per_seq, n_new):
    np_ = PAGES_PER_STEP
    sub = step & (steps_per_seq - 1)
    nr = q_ref.shape[1]

    def slot_copies(slot, page_of):
        copies = []
        for p in range(np_):
            page = page_of(p)
            copies.append(pltpu.make_async_copy(k_hbm.at[page], kbuf.at[slot, p], sem.at[slot]))
            copies.append(pltpu.make_async_copy(v_hbm.at[page], vbuf.at[slot, p], sem.at[slot]))
            copies.append(pltpu.make_async_copy(wt_hbm.at[page], wtbuf.at[slot, p], sem.at[slot]))
        return copies

    def start_step(for_step, slot):
        b = lax.shift_right_logical(for_step, steps_per_seq.bit_length() - 1)
        first_page = (for_step & (steps_per_seq - 1)) * np_
        for n, c in enumerate(slot_copies(slot, lambda p: pt_ref[b, first_page + p])):
            c.start(priority=n % 2)

    def gather_and_scores():
        slot = step & 1

        @pl.when(step == 0)
        def _():
            start_step(step, slot)

        @pl.when(step + 1 < n_steps)
        def _():
            start_step(step + 1, 1 - slot)

        for c in slot_copies(slot, lambda p: 0):
            c.wait()

        first = sub == 0
        q = q_ref[0].astype(BF16)
        state = (jnp.where(first, NEG, m_ref[...]), jnp.where(first, 0.0, l_ref[...]),
                 jnp.where(first, 0.0, acc_ref[...]))
        carry = jnp.where(first, 0.0, carry_ref[...])
        _, _, same = _head_match(nr, PAGE_ROWS)
        s_list, v_list = [], []
        for p in range(np_):
            wt = wtbuf[slot, p]
            ck = carry + wt[:, 0:PAGE_ROWS]
            carry = carry + wt[:, PAGE_ROWS:2 * PAGE_ROWS]
            s = _dot_nt(q, kbuf[slot, p].astype(BF16)) * ATT_SCALE - ck
            s_list.append(jnp.where(same, s, NEG))
            v_list.append(vbuf[slot, p].astype(BF16))

        def softmax_and_values():
            new_state = _softmax_update(state, s_list, v_list)

            def finish():
                m_ref[...], l_ref[...], acc_ref[...] = new_state
                carry_ref[...] = carry

                @pl.when(sub == steps_per_seq - 1)
                def _():
                    r, c, same_n = _head_match(nr, LANE)
                    cn = carry[:, 0:LANE] + _dot_exact01(lfn_ref[0], mn_ref[...])[0:1, :]
                    s = _dot_nt(q, kn_ref[0].astype(BF16)) * ATT_SCALE - cn
                    ok = same_n & (c < n_new * H_C) & ((c >> 2) <= (r >> 2))
                    _, l_fin, acc_fin = _softmax_update(new_state, [jnp.where(ok, s, NEG)],
                                                        [vn_ref[0].astype(BF16)])
                    o_ref[0] = acc_fin / l_fin * _silu(g_ref[0])

            return finish

        return softmax_and_values

    return gather_and_scores


N_FOX_IN = 9


def _in_proj_fox_kernel(pt_ref, x_ref, w_ref, wf_ref, bf_ref, *rest, tm, tn, tiles_per_seq, nj, n_fox_steps,
                        steps_per_seq, n_new):
    fox_in, rest = rest[:N_FOX_IN], rest[N_FOX_IN:]
    (z_ref, lf_ref, ko_ref, vo_ref, c_ref, o_ref), rest = rest[:6], rest[6:]
    (xb_ref, ccarry_ref), fox_scratch = rest[:2], rest[2:]
    step = pl.program_id(0) * nj + pl.program_id(1)
    work = _fox_paged_work(pt_ref, *fox_in, o_ref, *fox_scratch, step=step, n_steps=n_fox_steps,
                           steps_per_seq=steps_per_seq, n_new=n_new)
    _in_proj_kernel(x_ref, w_ref, wf_ref, bf_ref, z_ref, lf_ref, ko_ref, vo_ref, c_ref, xb_ref, ccarry_ref,
                    tm=tm, tn=tn, tiles_per_seq=tiles_per_seq, emit_c=True, head_cols=(KC, VC), extra_work=work,
                    extra_active=step < n_fox_steps)


FUSED_VMEM_LIMIT = 56 * 1024 * 1024


def _in_proj_fox(x, w_main, wf, bfb, layer, pt, q16, g16, kn, vn, lfn, mn, kflat, vflat, wt3, *, tm, tiles_per_seq,
                 n_new):
    m = x.shape[0]
    tn = BRANCH_W
    nj = Z_COLS // tn
    n_seq, nr, _ = q16.shape
    n_pages = pt.shape[1]
    np_ = PAGES_PER_STEP
    steps_per_seq = n_pages // np_
    n_fox_steps = n_seq * steps_per_seq
    assert steps_per_seq * np_ == n_pages and steps_per_seq & (steps_per_seq - 1) == 0
    assert n_fox_steps <= (m // tm) * nj

    def seq_of(i, j):
        return jnp.minimum(i * nj + j, n_fox_steps - 1) // steps_per_seq

    per_seq = lambda rows: pl.BlockSpec((1, rows, HD), lambda i, j, pt: (seq_of(i, j), 0, 0))
    pool = pl.BlockSpec(memory_space=pl.ANY)
    heads = pl.BlockSpec((tm, H_C, HD), lambda i, j, pt: (i, 0, 0))
    grid_spec = pltpu.PrefetchScalarGridSpec(
        num_scalar_prefetch=1,
        grid=(m // tm, nj),
        in_specs=[
            pl.BlockSpec((tm, D_MODEL), lambda i, j, pt: (i, 0), pipeline_mode=pl.Buffered(1)),
            pl.BlockSpec((None, D_MODEL, tn), lambda i, j, pt: (layer, 0, j)),
            pl.BlockSpec((None, D_MODEL, LANE), lambda i, j, pt: (layer, 0, 0)),
            pl.BlockSpec((1, LANE), lambda i, j, pt: (0, 0)),
            per_seq(nr), per_seq(nr), per_seq(LANE), per_seq(LANE),
            pl.BlockSpec((1, 8, LANE), lambda i, j, pt: (seq_of(i, j), 0, 0)),
            pl.BlockSpec((LANE, LANE), lambda i, j, pt: (0, 0)),
            pool, pool, pool,
        ],
        out_specs=[pl.BlockSpec((tm, tn), lambda i, j, pt: (i, j)),
                   pl.BlockSpec((tm, LANE), lambda i, j, pt: (i, 0)), heads, heads,
                   pl.BlockSpec((H_C, tm, LANE), lambda i, j, pt: (0, i, 0)),
                   per_seq(nr)],
        scratch_shapes=[pltpu.VMEM((tm, D_MODEL), BF16), pltpu.VMEM((1, LANE), F32),
                        pltpu.VMEM((2, np_, PAGE_ROWS, HD), F32), pltpu.VMEM((2, np_, PAGE_ROWS, HD), F32),
                        pltpu.VMEM((2, np_, 1, 2 * PAGE_ROWS), F32), pltpu.SemaphoreType.DMA((2,)),
                        pltpu.VMEM((nr, 1), F32), pltpu.VMEM((nr, 1), F32), pltpu.VMEM((nr, HD), F32),
                        pltpu.VMEM((1, PAGE_ROWS), F32)],
    )
    kern = functools.partial(_in_proj_fox_kernel, tm=tm, tn=tn, tiles_per_seq=tiles_per_seq, nj=nj,
                             n_fox_steps=n_fox_steps, steps_per_seq=steps_per_seq, n_new=n_new)
    return pl.pallas_call(
        kern,
        grid_spec=grid_spec,
        out_shape=[jax.ShapeDtypeStruct((m, Z_COLS), BF16), jax.ShapeDtypeStruct((m, LANE), F32),
                   jax.ShapeDtypeStruct((m, H_C, HD), F32), jax.ShapeDtypeStruct((m, H_C, HD), F32),
                   jax.ShapeDtypeStruct((H_C, m, LANE), F32), jax.ShapeDtypeStruct((n_seq, nr, HD), F32)],
        compiler_params=_cparams(2, vmem=FUSED_VMEM_LIMIT),
        name="in_proj_fox",
    )(pt, x, w_main, wf, bfb, q16, g16, kn, vn, lfn, mn, kflat, vflat, wt3)


def _merge_out_kernel(*refs, tm):
    o_refs, wb_ref, gate_refs = refs[0:4], refs[4], refs[5:9]
    wo_ref, x_ref, g_ref, b_ref, y_ref, h_ref = refs[9:15]
    outs = [o_refs[br][...].astype(BF16) for br in range(N_BRANCH)]
    for jc in range(D_MODEL // BRANCH_W):
        cs = slice(jc * BRANCH_W, (jc + 1) * BRANCH_W)
        acc = None
        for br in range(N_BRANCH):
            gate = 0.5 * jnp.tanh(0.5 * gate_refs[br][:, cs].astype(F32)) + 0.5
            term = gate * _dot(outs[br], wb_ref[br, :, cs])
            acc = term if acc is None else acc + term
        h_ref[:, cs] = acc.astype(h_ref.dtype)
    half = max(tm // 2, SUBLANES)
    for s in range(tm // half):
        rs = slice(s * half, (s + 1) * half)
        y = ALPHA * x_ref[rs, :] + _dot(h_ref[rs, :], wo_ref[...])
        y_ref[rs, :] = _ln(y, g_ref[...], b_ref[...])


def _merge_out(outs, z, wb, wo, x, lng, lnb, layer, *, tm):
    m = z.shape[0]
    o_spec = pl.BlockSpec((tm, BRANCH_W), lambda i: (i, 0))
    gate0 = GATE_BLK * BRANCH_W // D_MODEL
    gate = lambda br: pl.BlockSpec((tm, D_MODEL), lambda i, br=br: (i, gate0 + br))
    rows = pl.BlockSpec((tm, D_MODEL), lambda i: (i, 0))
    vec = pl.BlockSpec((1, D_MODEL), lambda i: (0, 0))
    once = pl.Buffered(1)
    return pl.pallas_call(
        functools.partial(_merge_out_kernel, tm=tm),
        grid=(m // tm,),
        in_specs=[o_spec] * 4
                 + [pl.BlockSpec((None, N_BRANCH, BRANCH_W, D_MODEL), lambda i: (layer, 0, 0, 0), pipeline_mode=once)]
                 + [gate(br) for br in range(N_BRANCH)]
                 + [pl.BlockSpec((None, D_MODEL, D_MODEL), lambda i: (layer, 0, 0), pipeline_mode=once),
                    rows, vec, vec],
        out_specs=rows,
        out_shape=jax.ShapeDtypeStruct((m, D_MODEL), F32),
        scratch_shapes=[pltpu.VMEM((tm, D_MODEL), BF16)],
        compiler_params=_cparams(1),
        name="merge_out",
    )(*outs, wb, z, z, z, z, wo, x, lng, lnb)


def _tok_head_matrices():
    i = jnp.arange(PAGE_ROWS)
    same = (i[:, None] % H_C) == (i[None, :] % H_C)
    mc = (same & (i[:, None] // H_C <= i[None, :] // H_C)).astype(BF16)
    mt = same.astype(BF16)
    return mc, mt


def kernel(x_prompt, x_sample, mem_prompt, cache_k, cache_v, cache_logf, cache_mem_k, cache_mem_v, state_conv,
           page_table, w_in, w_mem_k, w_mem_v, ln_v_g, ln_v_b, w_s, b_s, w_dw, b_dw, ln_c_g, ln_c_b, w_pw, b_pw,
           b_f, w_branch, w_out, ln_g, ln_b):
    bp, seq, _ = x_prompt.shape
    db, t_new, _ = x_sample.shape
    n_pool = cache_k.shape[1]

    w_main, wf = _prep_w(w_in)
    bfb = jnp.pad(b_f, ((0, 0), (0, LANE - H_C)))[:, None, :]
    wb = w_branch.astype(BF16)
    wo = w_out.astype(BF16)
    wpw = w_pw.astype(BF16)
    wmkv = jnp.concatenate([w_mem_k, w_mem_v], axis=2).astype(BF16)
    wdw = jnp.pad(w_dw, ((0, 0), (0, CONV_PAD - CONV_W), (0, 0)))
    vec = lambda a: a[:, None, :]
    ln_v_g, ln_v_b, b_dw, ln_c_g, ln_c_b, b_pw, ln_g, ln_b = map(
        vec, (ln_v_g, ln_v_b, b_dw, ln_c_g, ln_c_b, b_pw, ln_g, ln_b))

    idx = jnp.arange(CHUNK)
    mask_p = (idx[None, :] <= idx[:, None]).astype(F32)
    bsb_p = jnp.broadcast_to(b_s[:, :, :, None], (DEPTH, A_GROUPS, CHUNK, CHUNK))
    reps = CHUNK // t_new
    mask_s = ((idx[:, None] // t_new == idx[None, :] // t_new) & (idx[None, :] <= idx[:, None])).astype(F32)
    pick = (idx[:, None] % t_new == jnp.arange(t_new)[None, :]).astype(F32)
    ws_s = jnp.einsum("rt,lgts,cs->lgrc", pick, w_s[:, :, :t_new, :t_new], pick, precision=lax.Precision.HIGHEST)
    bsb_s = jnp.broadcast_to(jnp.tile(b_s[:, :, :t_new], (1, 1, reps))[:, :, :, None],
                             (DEPTH, A_GROUPS, CHUNK, CHUNK))

    kflat = cache_k.reshape(DEPTH * n_pool, PAGE_ROWS, HD)
    vflat = cache_v.reshape(DEPTH * n_pool, PAGE_ROWS, HD)
    mc, mt = _tok_head_matrices()
    wt3 = _logf_pages(cache_logf.reshape(DEPTH * n_pool, PAGE_ROWS), mc, mt, tm=512)
    mn = jnp.pad(mc[:t_new * H_C, :t_new * H_C], ((0, LANE - t_new * H_C),) * 2)
    memk = cache_mem_k.reshape(DEPTH * db, N_MEM * H_C, HD)
    memv = cache_mem_v.reshape(DEPTH * db, N_MEM * H_C, HD)
    state = state_conv.reshape(DEPTH * db, CONV_W - 1, BRANCH_W)

    xp = x_prompt.reshape(bp * seq, D_MODEL)
    xs = x_sample.reshape(db * t_new, D_MODEL)
    mem2d = mem_prompt.reshape(bp * N_MEM, D_MODEL)
    m_s = db * t_new
    outs = [[] for _ in range(11)]
    for l in range(DEPTH):
        zs, lfs, *slabs = _in_proj(xs, w_main, wf, bfb[l], l, tm=m_s, tiles_per_seq=1, emit_c=False, z_dtype=F32,
                                   head_cols=(QC, KC, VC, GC, QM, GM))
        q_s, k_s, v_s, gc_s, qm_s, gm_s = (a.reshape(db, t_new * H_C, HD) for a in slabs)
        pad_new = lambda a: jnp.pad(a, ((0, 0), (0, LANE - t_new * H_C), (0, 0)))
        lfn = lfs[:, :H_C].reshape(db, 1, t_new * H_C)
        lfn = jnp.pad(lfn, ((0, 0), (0, 7), (0, LANE - t_new * H_C)))

        z, lf, k_p, v_p, c_rep, oc_s = _in_proj_fox(
            xp, w_main, wf, bfb[l], l, page_table + l * n_pool, q_s, gc_s, pad_new(k_s), pad_new(v_s),
            lfn, mn, kflat, vflat, wt3, tm=1024, tiles_per_seq=seq // 1024, n_new=t_new)
        mkv = _mm(mem2d, wmkv[l], tm=bp * N_MEM, tn=2 * BRANCH_W)
        (oa,) = _branch_a(z, ln_v_g[l], ln_v_b[l], w_s[l], bsb_p[l], mask_p, tm=512, emit_v=False)
        ob, nconv = _branch_b(z, wdw[l], b_dw[l], ln_c_g[l], ln_c_b[l], wpw[l], b_pw[l], batch=bp, tm=512)
        oc = _fox_prompt(z, c_rep, batch=bp, tq=512)
        om = _mem_attn_prompt(z, mkv, batch=bp, tq=512)
        xp = _merge_out((oa, ob, oc, om), z, wb, wo, xp, ln_g[l], ln_b[l], l, tm=256)
        outs[0].append(k_p.reshape(bp, seq, H_C, HD))
        outs[1].append(v_p.reshape(bp, seq, H_C, HD))
        outs[2].append(lf[:, :H_C].reshape(bp, seq, H_C))
        outs[3].append(nconv)
        outs[4].append(mkv[:, :BRANCH_W].reshape(bp, N_MEM, H_C, HD))
        outs[5].append(mkv[:, BRANCH_W:].reshape(bp, N_MEM, H_C, HD))

        oa_s, v_rows = _branch_a(zs, ln_v_g[l], ln_v_b[l], ws_s[l], bsb_s[l], mask_s, tm=m_s, emit_v=True)
        ob_s, h_glu = _branch_b_sample(zs.reshape(db, t_new, Z_COLS), state, l, wdw[l], b_dw[l], ln_c_g[l],
                                       ln_c_b[l], wpw[l], b_pw[l])
        om_s = _mem_attn_sample(qm_s, gm_s, memk, memv, l)
        flat = lambda a: a.reshape(m_s, BRANCH_W)
        xs = _merge_out((oa_s, flat(ob_s), flat(oc_s), flat(om_s)), zs, wb, wo, xs, ln_g[l], ln_b[l], l, tm=m_s)
        outs[6].append(k_s.reshape(db, t_new, H_C, HD))
        outs[7].append(v_s.reshape(db, t_new, H_C, HD))
        outs[8].append(lfs[:, :H_C].reshape(db, t_new, H_C))
        outs[9].append(jnp.concatenate([state_conv[l][:, t_new:], h_glu], axis=1))
        outs[10].append(v_rows.reshape(db, t_new, BRANCH_W))

    return (xp.reshape(bp, seq, D_MODEL), xs.reshape(db, t_new, D_MODEL)) + tuple(jnp.stack(o) for o in outs)
```
